```python
import jax
import jax.numpy as jnp
from jax import lax
import numpy as np

D_MODEL = 2048
BATCH = 2
SEQ = 4096
DEPTH = 2

GRID_W = 64
CTX_LEN = 256
EPS = 1e-6
N_MOD = 6
N_DIR = 2
GROUP_W = D_MODEL // 4

SC_WIDTH = GROUP_W
CM_WIDTH = GROUP_W
CM_GROUPS = 4
CM_GROUP_DIM = CM_WIDTH // CM_GROUPS
CM_CHUNK = 128
MB_INNER = GROUP_W
MB_HEAD_DIM = 64
MB_HEADS = MB_INNER // MB_HEAD_DIM
MB_GROUPS = 2
MB_STATE = 128
MB_BC = MB_GROUPS * MB_STATE
MB_CONV_DIM = MB_INNER + 2 * MB_BC
SSD_CHUNK = 128
GLA_HEADS = 4
GLA_V = GROUP_W
GLA_QK = GLA_V // 2
GLA_DK = GLA_QK // GLA_HEADS
GLA_DV = GLA_V // GLA_HEADS
GLA_RANK = 16
GLA_GATE_NORM = 16.0
GLA_CHUNK = 64

P_SC = 3 * SC_WIDTH
P_CM = 2 * CM_WIDTH
P_MB = MB_INNER + MB_CONV_DIM + N_DIR * MB_HEADS
P_GLA = 2 * GLA_QK + 2 * GLA_V + N_DIR * GLA_RANK
P_TOTAL = P_SC + P_CM + P_MB + P_GLA
REC_OFF = P_SC + P_CM
SPLITS = [P_SC, P_SC + P_CM, P_SC + P_CM + P_MB]
D_MIX = SC_WIDTH + CM_WIDTH + MB_INNER + GLA_V
GLA_SPLITS = [GLA_QK, 2 * GLA_QK, 2 * GLA_QK + GLA_V, 2 * GLA_QK + 2 * GLA_V]

D_FF = ((8 * D_MODEL // 3 + 255) // 256) * 256
N_EXPERTS = 8
TOP_K = 2
D_FF_EXPERT = 7 * D_MODEL // 2
N_DENSE = (DEPTH + 1) // 2
N_MOE = DEPTH // 2

kernel_name = "hybrid_parallel_mixer_dit_block"


def rmsnorm(t, g):
    tf = t.astype(jnp.float32)
    tf = tf * lax.rsqrt(jnp.mean(tf * tf, axis=-1, keepdims=True) + EPS)
    return (tf * g.astype(jnp.float32)).astype(t.dtype)


def modulate(t, shift, scale):
    return t * (1 + scale) + shift


def _dir(t, d):
    return t if d == 0 else jnp.flip(t, axis=1)


def conv3(t, w, rows):
    b, L, ch = t.shape
    tp = jnp.pad(t.reshape(b, rows, L // rows, ch), ((0, 0), (0, 0), (1, 1), (0, 0)))
    y = tp[:, :, :-2] * w[0] + tp[:, :, 1:-1] * w[1] + tp[:, :, 2:] * w[2]
    return y.reshape(b, L, ch)


def short_conv_mixer(p, w, rows):
    bg, cg, xa = jnp.split(p, 3, axis=-1)
    return bg * conv3(cg * xa, w, rows)


def chunk_mlp_mixer(p, w_s, b_s):
    b, L, _ = p.shape
    u, v = jnp.split(jax.nn.gelu(p), 2, axis=-1)
    v = v.reshape(b, L // CM_CHUNK, CM_CHUNK, CM_GROUPS, CM_GROUP_DIM)
    s = jnp.einsum('gij,bcjgd->bcigd', w_s, v) + b_s.T[:, :, None]
    return u * s.reshape(b, L, CM_WIDTH)


def ssd_scan(xs, dt, a_neg, bm, cm, s0, return_y):
    b, L, h, p = xs.shape
    g, n = bm.shape[2], bm.shape[3]
    q = SSD_CHUNK
    nc = L // q
    bh = jnp.repeat(bm, h // g, axis=2).reshape(b, nc, q, h, n)
    xc = xs.reshape(b, nc, q, h, p)
    dtc = dt.reshape(b, nc, q, h)
    acs = jnp.cumsum(dtc * a_neg, axis=2)
    a_last = acs[:, :, -1]
    w_end = jnp.exp(a_last[:, :, None] - acs) * dtc
    chunk_states = jnp.einsum('bcjhn,bcjh,bcjhp->bchpn', bh, w_end, xc)

    def step(s, inp):
        st, al = inp
        return jnp.exp(al)[:, :, None, None] * s + st, s

    s_fin, s_start = lax.scan(step, s0, (jnp.moveaxis(chunk_states, 1, 0), jnp.moveaxis(a_last, 1, 0)))
    if not return_y:
        return s_fin
    s_start = jnp.moveaxis(s_start, 0, 1)
    ch = jnp.repeat(cm, h // g, axis=2).reshape(b, nc, q, h, n)
    acs_t = jnp.moveaxis(acs, 3, 2)
    seg = acs_t[..., :, None] - acs_t[..., None, :]
    lower = jnp.tril(jnp.ones((q, q), dtype=bool))
    decay = jnp.exp(jnp.where(lower, seg, -jnp.inf))
    scores = jnp.einsum('bcihn,bcjhn->bchij', ch, bh) * decay * jnp.moveaxis(dtc, 3, 2)[:, :, :, None, :]
    y = (jnp.einsum('bchij,bcjhp->bcihp', scores, xc)
         + jnp.einsum('bcihn,bchpn->bcihp', ch, s_start) * jnp.exp(acs)[..., None])
    return y.reshape(b, L, h, p), s_fin


def mamba_prep(p, conv_w, conv_b, dt_bias, rows):
    b, L, _ = p.shape
    z, xbc, dt = jnp.split(p, [MB_INNER, MB_INNER + MB_CONV_DIM], axis=-1)
    xbc = jax.nn.silu(conv3(xbc, conv_w, rows) + conv_b)
    xs, bm, cm = jnp.split(xbc, [MB_INNER, MB_INNER + MB_BC], axis=-1)
    xs = xs.astype(jnp.float32).reshape(b, L, MB_HEADS, MB_HEAD_DIM)
    bm = bm.astype(jnp.float32).reshape(b, L, MB_GROUPS, MB_STATE)
    cm = cm.astype(jnp.float32).reshape(b, L, MB_GROUPS, MB_STATE)
    dt = jax.nn.softplus(dt.astype(jnp.float32).reshape(b, L, N_DIR, MB_HEADS) + dt_bias.astype(jnp.float32))
    return z, xs, bm, cm, dt


def mamba_out(y, z, norm_g):
    b, L = z.shape[:2]
    y = y.reshape(b, L, MB_INNER) * jax.nn.silu(z.astype(jnp.float32))
    return rmsnorm(y, norm_g).astype(z.dtype)


def mamba_mixer(p, pc, conv_w, conv_b, a_log, dt_bias, d_skip, norm_g, rows, ctx_out):
    z, xs, bm, cm, dt = mamba_prep(p, conv_w, conv_b, dt_bias, rows)
    zc, xsc, bmc, cmc, dtc = mamba_prep(pc, conv_w, conv_b, dt_bias, 1)
    s0 = jnp.zeros((p.shape[0], MB_HEADS, MB_HEAD_DIM, MB_STATE), jnp.float32)
    y, yc = 0.0, 0.0
    for d in range(N_DIR):
        a_neg = -jnp.exp(a_log[d].astype(jnp.float32))
        skip = d_skip[d].astype(jnp.float32)[:, None]
        ctx_args = (_dir(xsc, d), _dir(dtc[:, :, d], d), a_neg, _dir(bmc, d), _dir(cmc, d), s0)
        if ctx_out:
            yc_d, s_ctx = ssd_scan(*ctx_args, True)
            yc = yc + _dir(yc_d, d) + xsc * skip
        else:
            s_ctx = ssd_scan(*ctx_args, False)
        y_d, _ = ssd_scan(_dir(xs, d), _dir(dt[:, :, d], d), a_neg, _dir(bm, d), _dir(cm, d), s_ctx, True)
        y = y + _dir(y_d, d) + xs * skip
    return mamba_out(y, z, norm_g), (mamba_out(yc, zc, norm_g) if ctx_out else None)


def gla_scan(q, k, v, gk, s0, return_y):
    b, L, h, dk = k.shape
    dv = v.shape[-1]
    qn = GLA_CHUNK
    nc = L // qn
    kc = k.reshape(b, nc, qn, h, dk)
    vc = v.reshape(b, nc, qn, h, dv)
    G = jnp.cumsum(gk.reshape(b, nc, qn, h, dk), axis=2)
    G_last = G[:, :, -1]
    chunk_states = jnp.einsum('bcjhk,bcjhv->bchkv', kc * jnp.exp(G_last[:, :, None] - G), vc)

    def step(s, inp):
        st, gl = inp
        return jnp.exp(gl)[..., None] * s + st, s

    s_fin, s_start = lax.scan(step, s0, (jnp.moveaxis(chunk_states, 1, 0), jnp.moveaxis(G_last, 1, 0)))
    if not return_y:
        return s_fin
    s_start = jnp.moveaxis(s_start, 0, 1)
    qc = q.reshape(b, nc, qn, h, dk)
    y_inter = jnp.einsum('bcihk,bchkv->bcihv', qc * jnp.exp(G), s_start)
    G_ref = G[:, :, qn // 2][:, :, None]
    att = jnp.einsum('bcihk,bcjhk->bchij', qc * jnp.exp(G - G_ref), kc * jnp.exp(G_ref - G))
    lower = jnp.tril(jnp.ones((qn, qn), dtype=bool))
    att = jnp.where(lower, att, 0.0)
    y = y_inter + jnp.einsum('bchij,bcjhv->bcihv', att, vc)
    return y.reshape(b, L, h, dv), s_fin


def gla_prep(p, w_gate, b_gate):
    b, L, _ = p.shape
    q, k, v, g, glr = jnp.split(p, GLA_SPLITS, axis=-1)
    q = q.astype(jnp.float32).reshape(b, L, GLA_HEADS, GLA_DK) * GLA_DK ** -0.5
    k = k.astype(jnp.float32).reshape(b, L, GLA_HEADS, GLA_DK)
    v = v.astype(jnp.float32).reshape(b, L, GLA_HEADS, GLA_DV)
    glr = glr.astype(jnp.float32).reshape(b, L, N_DIR, GLA_RANK)
    gk = jax.nn.log_sigmoid(jnp.einsum('bldr,drk->bldk', glr, w_gate.astype(jnp.float32))
                            + b_gate.astype(jnp.float32)) / GLA_GATE_NORM
    return q, k, v, gk.reshape(b, L, N_DIR, GLA_HEADS, GLA_DK), g


def gla_out(o, g, norm_g):
    b, L = g.shape[:2]
    return (rmsnorm(o, norm_g).reshape(b, L, GLA_V) * jax.nn.silu(g.astype(jnp.float32))).astype(g.dtype)


def gla_mixer(p, pc, w_gate, b_gate, norm_g, ctx_out):
    q, k, v, gk, g = gla_prep(p, w_gate, b_gate)
    qc, kc, vc, gkc, gc = gla_prep(pc, w_gate, b_gate)
    s0 = jnp.zeros((p.shape[0], GLA_HEADS, GLA_DK, GLA_DV), jnp.float32)
    o, oc = 0.0, 0.0
    for d in range(N_DIR):
        if ctx_out:
            oc_d, s_ctx = gla_scan(_dir(qc, d), _dir(kc, d), _dir(vc, d), _dir(gkc[:, :, d], d), s0, True)
            oc = oc + _dir(oc_d, d)
        else:
            s_ctx = gla_scan(None, _dir(kc, d), _dir(vc, d), _dir(gkc[:, :, d], d), s0, False)
        o_d, _ = gla_scan(_dir(q, d), _dir(k, d), _dir(v, d), _dir(gk[:, :, d], d), s_ctx, True)
        o = o + _dir(o_d, d)
    return gla_out(o, g, norm_g), (gla_out(oc, gc, norm_g) if ctx_out else None)


def swiglu(t, wg, wu, wd):
    return (jax.nn.silu(t @ wg) * (t @ wu)) @ wd


def moe_swiglu(t, w_router, wg, wu, wd):
    b, L, d = t.shape
    tf = t.reshape(b * L, d)
    logits = (tf @ w_router).astype(jnp.float32)
    top_v, top_i = lax.top_k(logits, TOP_K)
    top_w = jax.nn.softmax(top_v, axis=-1)
    gates = jnp.sum(jax.nn.one_hot(top_i, N_EXPERTS, dtype=jnp.float32) * top_w[..., None], axis=1).astype(t.dtype)
    out = jnp.zeros_like(tf)
    for e in range(N_EXPERTS):
        out = out + gates[:, e:e + 1] * swiglu(tf, wg[e], wu[e], wd[e])
    return out.reshape(b, L, d)


def setup_inputs(seed: int = 0) -> dict:
    key = jax.random.key(seed)
    ks = iter(jax.random.split(key, 32))
    f32 = jnp.float32

    def nrm(shape, scale):
        return jax.random.normal(next(ks), shape, f32) * scale

    def gain(shape):
        return 1.0 + nrm(shape, 0.02)

    dt0 = jnp.exp(jax.random.uniform(next(ks), (DEPTH, N_DIR, MB_HEADS), f32, np.log(1e-3), np.log(1e-1)))
    return {
        "x": nrm((BATCH, SEQ, D_MODEL), 1.0),
        "c": nrm((BATCH, D_MODEL), 1.0),
        "ctx": nrm((BATCH, CTX_LEN, D_MODEL), 1.0),
        "c_ctx": nrm((D_MODEL,), 1.0),
        "w_ada": nrm((DEPTH, D_MODEL, N_MOD * D_MODEL), 0.5 * D_MODEL ** -0.5),
        "b_ada": nrm((DEPTH, N_MOD * D_MODEL), 0.01),
        "g_mix": gain((DEPTH, D_MODEL)),
        "g_ffn": gain((DEPTH, D_MODEL)),
        "w_in": nrm((DEPTH, D_MODEL, P_TOTAL), D_MODEL ** -0.5),
        "w_out": nrm((DEPTH, D_MIX, D_MODEL), D_MIX ** -0.5),
        "sc_conv_w": nrm((DEPTH, 3, SC_WIDTH), 3 ** -0.5),
        "cm_w_s": nrm((DEPTH, CM_GROUPS, CM_CHUNK, CM_CHUNK), CM_CHUNK ** -0.5),
        "cm_b_s": 1.0 + nrm((DEPTH, CM_GROUPS, CM_CHUNK), 0.01),
        "mb_conv_w": nrm((DEPTH, 3, MB_CONV_DIM), 3 ** -0.5),
        "mb_conv_b": nrm((DEPTH, MB_CONV_DIM), 0.01),
        "mb_a_log": jnp.log(jax.random.uniform(next(ks), (DEPTH, N_DIR, MB_HEADS), f32, 1.0, 16.0)),
        "mb_dt_bias": dt0 + jnp.log(-jnp.expm1(-dt0)),
        "mb_d": 1.0 + nrm((DEPTH, N_DIR, MB_HEADS), 0.1),
        "mb_norm_g": gain((DEPTH, MB_INNER)),
        "gla_w_gate": nrm((DEPTH, N_DIR, GLA_RANK, GLA_QK), GLA_RANK ** -0.5),
        "gla_b_gate": nrm((DEPTH, N_DIR, GLA_QK), 0.1),
        "gla_norm_g": gain((DEPTH, GLA_DV)),
        "ffn_w_gate": nrm((N_DENSE, D_MODEL, D_FF), D_MODEL ** -0.5),
        "ffn_w_up": nrm((N_DENSE, D_MODEL, D_FF), D_MODEL ** -0.5),
        "ffn_w_down": nrm((N_DENSE, D_FF, D_MODEL), D_FF ** -0.5),
        "moe_router": nrm((N_MOE, D_MODEL, N_EXPERTS), D_MODEL ** -0.5),
        "moe_w_gate": nrm((N_MOE, N_EXPERTS, D_MODEL, D_FF_EXPERT), D_MODEL ** -0.5),
        "moe_w_up": nrm((N_MOE, N_EXPERTS, D_MODEL, D_FF_EXPERT), D_MODEL ** -0.5),
        "moe_w_down": nrm((N_MOE, N_EXPERTS, D_FF_EXPERT, D_MODEL), D_FF_EXPERT ** -0.5),
        "g_final": gain((D_MODEL,)),
    }


def reference(x, c, ctx, c_ctx, w_ada, b_ada, g_mix, g_ffn, w_in, w_out, sc_conv_w, cm_w_s, cm_b_s,
              mb_conv_w, mb_conv_b, mb_a_log, mb_dt_bias, mb_d, mb_norm_g, gla_w_gate, gla_b_gate, gla_norm_g,
              ffn_w_gate, ffn_w_up, ffn_w_down, moe_router, moe_w_gate, moe_w_up, moe_w_down, g_final):
    rows = x.shape[1] // GRID_W
    h, hc = x, ctx
    s_c = jax.nn.silu(c)
    s_cc = jax.nn.silu(c_ctx)

    def channel_mixer(i, t):
        j = i // 2
        if i % 2 == 0:
            return swiglu(t, ffn_w_gate[j], ffn_w_up[j], ffn_w_down[j])
        return moe_swiglu(t, moe_router[j], moe_w_gate[j], moe_w_up[j], moe_w_down[j])

    for i in range(DEPTH):
        last = i == DEPTH - 1
        mod = jnp.split((s_c @ w_ada[i] + b_ada[i])[:, None, :], N_MOD, axis=-1)
        mod_c = jnp.split(s_cc @ w_ada[i] + b_ada[i], N_MOD, axis=-1)
        xn = modulate(rmsnorm(h, g_mix[i]), mod[0], mod[1])
        xcn = modulate(rmsnorm(hc, g_mix[i]), mod_c[0], mod_c[1])
        p_sc, p_cm, p_mb, p_gla = jnp.split(xn @ w_in[i], SPLITS, axis=-1)
        if last:
            pc_mb, pc_gla = jnp.split(xcn @ w_in[i][:, REC_OFF:], [P_MB], axis=-1)
        else:
            pc_sc, pc_cm, pc_mb, pc_gla = jnp.split(xcn @ w_in[i], SPLITS, axis=-1)
        y_mb, yc_mb = mamba_mixer(p_mb, pc_mb, mb_conv_w[i], mb_conv_b[i], mb_a_log[i], mb_dt_bias[i],
                                  mb_d[i], mb_norm_g[i], rows, not last)
        y_gla, yc_gla = gla_mixer(p_gla, pc_gla, gla_w_gate[i], gla_b_gate[i], gla_norm_g[i], not last)
        y = jnp.concatenate([short_conv_mixer(p_sc, sc_conv_w[i], rows),
                             chunk_mlp_mixer(p_cm, cm_w_s[i], cm_b_s[i]), y_mb, y_gla], axis=-1)
        h = h + mod[2] * (y @ w_out[i])
        h = h + mod[5] * channel_mixer(i, modulate(rmsnorm(h, g_ffn[i]), mod[3], mod[4]))
        if not last:
            yc = jnp.concatenate([short_conv_mixer(pc_sc, sc_conv_w[i], 1),
                                  chunk_mlp_mixer(pc_cm, cm_w_s[i], cm_b_s[i]), yc_mb, yc_gla], axis=-1)
            hc = hc + mod_c[2] * (yc @ w_out[i])
            hc = hc + mod_c[5] * channel_mixer(i, modulate(rmsnorm(hc, g_ffn[i]), mod_c[3], mod_c[4]))
    return rmsnorm(h, g_final)
```

```python
import functools

import jax
import jax.numpy as jnp
from jax import lax
from jax.experimental import pallas as pl
from jax.experimental.pallas import tpu as pltpu

F32 = jnp.float32
BF16 = jnp.bfloat16
EPS = 1e-6

D = 2048
GW = D // 4
GRID_W = 64
N_MOD = 6
LANE = 128
P_PAD = 5760
COL_SMALL = 5632

SSD_Q = 128
SSD_HEADS = 8
SSD_HD = 64
GLA_C = 64
GLA_HEADS = 4
GLA_DK = 64
GLA_DV = 128
GLA_RANK = 16
N_EXPERTS = 8

MOE_TM = 1024
MOE_SB = 256
MOE_TF = 256
MOE_NT = 2 * 8192 // MOE_TM + N_EXPERTS
GATHER_R = 256


def _silu(x):
    return x * jax.nn.sigmoid(x)


def _softplus(x):
    return jnp.maximum(x, 0.0) + jnp.log1p(jnp.exp(-jnp.abs(x)))


def _split3(x):
    hi = x.astype(BF16)
    r = x - hi.astype(F32)
    mid = r.astype(BF16)
    lo = (r - mid.astype(F32)).astype(BF16)
    return hi, mid, lo


def _dot(a, b):
    return jnp.dot(a, b, preferred_element_type=F32)


def _dot_nt(a, b):
    return lax.dot_general(a, b, (((1,), (1,)), ((), ())), preferred_element_type=F32)


def _dot_tn(a, b):
    return lax.dot_general(a, b, (((0,), (0,)), ((), ())), preferred_element_type=F32)


def _dot01_left(m01, x):
    hi, mid, lo = _split3(x)
    return _dot(m01, hi) + _dot(m01, mid) + _dot(m01, lo)


def _dot01_right(x, m01):
    hi, mid, lo = _split3(x)
    return _dot(hi, m01) + _dot(mid, m01) + _dot(lo, m01)


def _rms(x, g):
    return x * lax.rsqrt(jnp.mean(x * x, axis=-1, keepdims=True) + EPS) * g


def _conv3(x, w, seg):
    n = x.shape[0]
    t = lax.broadcasted_iota(jnp.int32, (n, 1), 0) & (seg - 1)
    prev = jnp.where(t == 0, 0.0, pltpu.roll(x, 1, 0))
    nxt = jnp.where(t == seg - 1, 0.0, pltpu.roll(x, n - 1, 0))
    return prev * w[0:1] + x * w[1:2] + nxt * w[2:3]


def _ada_kernel(cb_ref, w_ref, b_ref, o_ref, s_ref):
    @pl.when((pl.program_id(0) == 0) & (pl.program_id(1) == 0))
    def _():
        s_ref[...] = _silu(cb_ref[...])

    tn = w_ref.shape[1]
    o_ref[...] = jnp.zeros_like(o_ref)
    for j in range(tn // LANE):
        wj = w_ref[:, j * LANE:(j + 1) * LANE]
        for m in range(3):
            o_ref[m:m + 1, j * LANE:(j + 1) * LANE] = (
                jnp.sum(wj * s_ref[m], axis=0, keepdims=True) + b_ref[:, j * LANE:(j + 1) * LANE])


def _ada(cb, w_ada, b_ada):
    depth = w_ada.shape[0]
    n = w_ada.shape[2]
    tn = 1024
    return pl.pallas_call(
        _ada_kernel,
        out_shape=jax.ShapeDtypeStruct((depth, 8, n), F32),
        grid=(depth, n // tn),
        in_specs=[pl.BlockSpec((3, D, LANE), lambda l, j: (0, 0, 0)),
                  pl.BlockSpec((None, D, tn), lambda l, j: (l, 0, j)),
                  pl.BlockSpec((None, 1, tn), lambda l, j: (l, 0, j))],
        out_specs=pl.BlockSpec((None, 8, tn), lambda l, j: (l, 0, j)),
        scratch_shapes=[pltpu.VMEM((3, D, LANE), F32)],
        name="ada",
    )(cb, w_ada, b_ada.reshape(depth, 1, n))


def _inproj_kernel(h_ref, g_ref, sh_ref, sc_ref, w_ref, o_ref, xn_ref):
    @pl.when(pl.program_id(1) == 0)
    def _():
        xn = _rms(h_ref[...], g_ref[...])
        xn_ref[...] = (xn * (1.0 + sc_ref[...]) + sh_ref[...]).astype(BF16)

    o_ref[...] = _dot(xn_ref[...], w_ref[...])


def _mod_spec(row_fn, k):
    return pl.BlockSpec((None, None, 1, D), lambda i, j: (row_fn(i), k, 0, 0))


def _inproj(h, g, modr, row_fn, w, tm):
    m = h.shape[0]
    tn = 1152
    return pl.pallas_call(
        _inproj_kernel,
        out_shape=jax.ShapeDtypeStruct((m, P_PAD), F32),
        grid=(m // tm, P_PAD // tn),
        in_specs=[pl.BlockSpec((tm, D), lambda i, j: (i, 0)),
                  pl.BlockSpec((1, D), lambda i, j: (0, 0)),
                  _mod_spec(row_fn, 0), _mod_spec(row_fn, 1),
                  pl.BlockSpec((D, tn), lambda i, j: (0, j))],
        out_specs=pl.BlockSpec((tm, tn), lambda i, j: (i, j)),
        scratch_shapes=[pltpu.VMEM((tm, D), BF16)],
        name="inproj",
    )(h, g, modr, modr, w)


def _gelu_tanh(x):
    c = 0.7978845608028654
    return x * (0.5 * (1.0 + jnp.tanh(c * (x + 0.044715 * (x * x * x)))))


def _scm_kernel(sc_ref, u_ref, v_ref, cw_ref, ws_ref, bs_ref, o_ref, *, seg):
    t = sc_ref.shape[0]
    bgate = sc_ref[:, 0:GW]
    y_sc = bgate * _conv3(sc_ref[:, GW:2 * GW] * sc_ref[:, 2 * GW:3 * GW], cw_ref[...], seg)
    o_ref[:, 0:GW] = y_sc.astype(BF16)

    u = _gelu_tanh(u_ref[...])
    v = _gelu_tanh(v_ref[...]).astype(BF16)
    for c in range(t // 128):
        rows = slice(c * 128, (c + 1) * 128)
        for g in range(4):
            cols = slice(g * 128, (g + 1) * 128)
            s = _dot(ws_ref[g], v[rows, cols]) + bs_ref[:, cols]
            o_ref[rows, GW + g * 128:GW + (g + 1) * 128] = (u[rows, cols] * s).astype(BF16)


def _scm(p, cw, ws, bs, seg):
    m = p.shape[0]
    t = 512
    return pl.pallas_call(
        functools.partial(_scm_kernel, seg=seg),
        out_shape=jax.ShapeDtypeStruct((m, 2 * GW), BF16),
        grid=(m // t,),
        in_specs=[pl.BlockSpec((t, 3 * GW), lambda i: (i, 0)),
                  pl.BlockSpec((t, GW), lambda i: (i, 3)),
                  pl.BlockSpec((t, GW), lambda i: (i, 4)),
                  pl.BlockSpec((3, GW), lambda i: (0, 0)),
                  pl.BlockSpec((4, 128, 128), lambda i: (0, 0, 0)),
                  pl.BlockSpec((128, GW), lambda i: (0, 0))],
        out_specs=pl.BlockSpec((t, 2 * GW), lambda i: (i, 0)),
        name="scm",
    )(p, p, p, cw, ws, bs)


def _ssd_kernel(*refs, direction, final, seg):
    if final:
        (xbc_ref, sm_ref, cw_ref, cb_ref, alog_ref, dtb_ref, e_ref, s0_ref,
         y0_ref, z_ref, skip_ref, ng_ref, y_ref, sfin_ref, st_ref) = refs
    else:
        (xbc_ref, sm_ref, cw_ref, cb_ref, alog_ref, dtb_ref, e_ref, s0_ref,
         y_ref, sfin_ref, st_ref) = refs
    j = pl.program_id(1)
    tb = xbc_ref.shape[0]
    q = SSD_Q

    @pl.when(j == 0)
    def _():
        st_ref[...] = s0_ref[...]

    xc = _silu(_conv3(xbc_ref[...], cw_ref[...], seg) + cb_ref[...])

    a_neg = -jnp.exp(alog_ref[...])
    ii = lax.broadcasted_iota(jnp.int32, (q, q), 0)
    jj = lax.broadcasted_iota(jnp.int32, (q, q), 1)
    mask = (jj <= ii) if direction == 0 else (jj >= ii)
    tri = mask.astype(BF16)
    last = q - 1 if direction == 0 else 0
    lane0 = 8 * direction
    upper_half = lax.broadcasted_iota(jnp.int32, (q, 128), 1) >= SSD_HD
    expand = e_ref[...]

    nchunk = tb // q
    order = range(nchunk) if direction == 0 else range(nchunk - 1, -1, -1)
    for c in order:
        rows = slice(c * q, (c + 1) * q)
        dt = _softplus(sm_ref[rows, :] + dtb_ref[...])
        acs = _dot01_left(tri, dt * a_neg)
        tot = acs[last:last + 1, :]
        acs_t = acs.T
        dt_t = dt.T
        wx = _dot01_right(jnp.exp(tot - acs) * dt, expand)
        ea = _dot01_right(jnp.exp(acs), expand)
        xs = xc[rows, 0:GW]
        xw = (wx * xs).astype(BF16)
        st = st_ref[...]
        stb = st.astype(BF16)
        ys = []
        for g in range(2):
            bg = xc[rows, GW + g * 128:GW + (g + 1) * 128].astype(BF16)
            cg = xc[rows, GW + 256 + g * 128:GW + 256 + (g + 1) * 128].astype(BF16)
            cb = _dot_nt(cg, bg)
            gcols = slice(g * 256, (g + 1) * 256)
            yint = _dot(cg, stb[:, gcols])
            for pr in range(2):
                pidx = 2 * g + pr
                pcols = slice(pidx * 128, (pidx + 1) * 128)
                xp = xs[:, pcols]
                acc = yint[:, pr * 128:(pr + 1) * 128] * ea[:, pcols]
                for hh in range(2):
                    ln = lane0 + 2 * pidx + hh
                    sg = acs[:, ln:ln + 1] - acs_t[ln:ln + 1, :]
                    decay = jnp.where(mask, jnp.exp(jnp.where(mask, sg, 0.0)), 0.0)
                    scores = (cb * decay * dt_t[ln:ln + 1, :]).astype(BF16)
                    xh = jnp.where(upper_half == (hh == 1), xp, 0.0).astype(BF16)
                    acc = acc + _dot(scores, xh)
                ys.append(acc)
            st_ref[:, gcols] = ea[last:last + 1, gcols] * st[:, gcols] + _dot_tn(bg, xw[:, gcols])
        y = jnp.concatenate(ys, axis=1)
        if final:
            y = y0_ref[rows, :] + y + xs * skip_ref[...]
            y = y * _silu(z_ref[rows, :])
            y_ref[rows, :] = _rms(y, ng_ref[...]).astype(y_ref.dtype)
        else:
            y_ref[rows, :] = y

    @pl.when(j == pl.num_programs(1) - 1)
    def _():
        sfin_ref[...] = st_ref[...]


def _ssd(p, s0, consts, nb, seg, direction, final, y0=None, fin=None):
    m = p.shape[0]
    tb = 256
    nblk = m // nb // tb
    if direction == 0:
        blk = lambda b, j: b * nblk + j
    else:
        blk = lambda b, j: b * nblk + (nblk - 1 - j)
    full = lambda shape: pl.BlockSpec(shape, lambda b, j: (0,) * len(shape))
    in_specs = [pl.BlockSpec((tb, 1024), lambda b, j: (blk(b, j), 3)),
                pl.BlockSpec((tb, LANE), lambda b, j: (blk(b, j), COL_SMALL // LANE)),
                full((3, 1024)), full((1, 1024)), full((1, LANE)), full((1, LANE)), full((LANE, GW)),
                pl.BlockSpec((None, SSD_Q, GW), lambda b, j: (b, 0, 0))]
    args = [p, p, *consts, s0]
    if final:
        in_specs += [pl.BlockSpec((tb, GW), lambda b, j: (blk(b, j), 0)),
                     pl.BlockSpec((tb, GW), lambda b, j: (blk(b, j), 5)),
                     full((1, GW)), full((1, GW))]
        args += [y0, p, *fin]
    return pl.pallas_call(
        functools.partial(_ssd_kernel, direction=direction, final=final, seg=seg),
        out_shape=(jax.ShapeDtypeStruct((m, GW), BF16 if final else F32),
                   jax.ShapeDtypeStruct((nb, SSD_Q, GW), F32)),
        grid=(nb, nblk),
        in_specs=in_specs,
        out_specs=(pl.BlockSpec((tb, GW), lambda b, j: (blk(b, j), 0)),
                   pl.BlockSpec((None, SSD_Q, GW), lambda b, j: (b, 0, 0))),
        scratch_shapes=[pltpu.VMEM((SSD_Q, GW), F32)],
        name=f"ssd_d{direction}",
    )(*args)


def _gla_kernel(*refs, direction, final):
    if final:
        (q_ref, k_ref, v_ref, sm_ref, wg_ref, bg_ref, s0_ref, o0_ref, g_ref, ng_ref,
         o_ref, sfin_ref, st_ref) = refs
    else:
        (q_ref, k_ref, v_ref, sm_ref, wg_ref, bg_ref, s0_ref, o_ref, sfin_ref, st_ref) = refs
    j = pl.program_id(1)
    tb = q_ref.shape[0]
    c_len = GLA_C

    @pl.when(j == 0)
    def _():
        st_ref[...] = s0_ref[...]

    gk = -_softplus(-(_dot(sm_ref[...].astype(BF16), wg_ref[...]) + bg_ref[...])) * (1.0 / 16.0)
    ii = lax.broadcasted_iota(jnp.int32, (tb, tb), 0)
    jj = lax.broadcasted_iota(jnp.int32, (tb, tb), 1)
    same = (ii >> 6) == (jj >> 6)
    btri = (same & ((jj <= ii) if direction == 0 else (jj >= ii))).astype(BF16)
    gcum = _dot01_left(btri, gk)

    i64 = lax.broadcasted_iota(jnp.int32, (c_len, c_len), 0)
    j64 = lax.broadcasted_iota(jnp.int32, (c_len, c_len), 1)
    causal = (j64 <= i64) if direction == 0 else (j64 >= i64)
    last = c_len - 1 if direction == 0 else 0
    ref = c_len // 2 if direction == 0 else c_len - 1 - c_len // 2
    upper_half = lax.broadcasted_iota(jnp.int32, (c_len, 128), 1) >= GLA_DK

    nchunk = tb // c_len
    order = range(nchunk) if direction == 0 else range(nchunk - 1, -1, -1)
    outs = [None] * nchunk
    for c in order:
        rows = slice(c * c_len, (c + 1) * c_len)
        gc = gcum[rows, :]
        gl = gc[last:last + 1, :]
        gr = gc[ref:ref + 1, :]
        qc = q_ref[rows, :] * (GLA_DK ** -0.5)
        kc = k_ref[rows, :]
        vc = v_ref[rows, :].astype(BF16)
        kdec = kc * jnp.exp(gl - gc)
        qg = qc * jnp.exp(gc)
        qa = qc * jnp.exp(gc - gr)
        kb = (kc * jnp.exp(gr - gc)).astype(BF16)
        st = st_ref[...]
        stb = st.astype(BF16)
        heads = []
        for pr in range(2):
            pcols = slice(pr * 128, (pr + 1) * 128)
            contrib = jnp.zeros((GLA_DV, 128), F32)
            for hh in range(2):
                h = 2 * pr + hh
                mh = upper_half == (hh == 1)
                zero = jnp.zeros((c_len, 128), F32)
                att = _dot_nt(jnp.where(mh, qa[:, pcols], zero).astype(BF16), kb[:, pcols])
                att = jnp.where(causal, att, 0.0).astype(BF16)
                vh = vc[:, h * GLA_DV:(h + 1) * GLA_DV]
                yh = _dot(att, vh) + _dot_nt(jnp.where(mh, qg[:, pcols], zero).astype(BF16), stb[:, pcols])
                heads.append(yh)
                contrib = contrib + _dot_tn(vh, jnp.where(mh, kdec[:, pcols], zero).astype(BF16))
            st_ref[:, pcols] = jnp.exp(gl[:, pcols]) * st[:, pcols] + contrib
        outs[c] = jnp.concatenate(heads, axis=1)
    o = jnp.concatenate(outs, axis=0)
    if final:
        o = o0_ref[...] + o
        normed = [_rms(o[:, h * GLA_DV:(h + 1) * GLA_DV], ng_ref[...]) for h in range(GLA_HEADS)]
        o_ref[...] = (jnp.concatenate(normed, axis=1) * _silu(g_ref[...])).astype(o_ref.dtype)
    else:
        o_ref[...] = o

    @pl.when(j == pl.num_programs(1) - 1)
    def _():
        sfin_ref[...] = st_ref[...]


def _gla(p, s0, consts, nb, direction, final, o0=None, ng=None):
    m = p.shape[0]
    tb = 256
    nblk = m // nb // tb
    if direction == 0:
        blk = lambda b, j: b * nblk + j
    else:
        blk = lambda b, j: b * nblk + (nblk - 1 - j)
    full = lambda shape: pl.BlockSpec(shape, lambda b, j: (0,) * len(shape))
    in_specs = [pl.BlockSpec((tb, 256), lambda b, j: (blk(b, j), 16)),
                pl.BlockSpec((tb, 256), lambda b, j: (blk(b, j), 17)),
                pl.BlockSpec((tb, GW), lambda b, j: (blk(b, j), 9)),
                pl.BlockSpec((tb, LANE), lambda b, j: (blk(b, j), COL_SMALL // LANE)),
                full((LANE, 256)), full((1, 256)),
                pl.BlockSpec((None, GLA_DV, 256), lambda b, j: (b, 0, 0))]
    args = [p, p, p, p, *consts, s0]
    if final:
        in_specs += [pl.BlockSpec((tb, GW), lambda b, j: (blk(b, j), 0)),
                     pl.BlockSpec((tb, GW), lambda b, j: (blk(b, j), 10)),
                     full((1, GLA_DV))]
        args += [o0, p, ng]
    return pl.pallas_call(
        functools.partial(_gla_kernel, direction=direction, final=final),
        out_shape=(jax.ShapeDtypeStruct((m, GW), BF16 if final else F32),
                   jax.ShapeDtypeStruct((nb, GLA_DV, 256), F32)),
        grid=(nb, nblk),
        in_specs=in_specs,
        out_specs=(pl.BlockSpec((tb, GW), lambda b, j: (blk(b, j), 0)),
                   pl.BlockSpec((None, GLA_DV, 256), lambda b, j: (b, 0, 0))),
        scratch_shapes=[pltpu.VMEM((GLA_DV, 256), F32)],
        name=f"gla_d{direction}",
    )(*args)


def _outproj_kernel(*refs, route):
    if route:
        (h_ref, ya_ref, ymb_ref, ygla_ref, w_ref, gate_ref, g_ref, sh_ref, sc_ref, wr_ref,
         hn_ref, xn_ref, ri_ref, rw_ref, cnt_ref, carry_ref) = refs
    else:
        (h_ref, ya_ref, ymb_ref, ygla_ref, w_ref, gate_ref, g_ref, sh_ref, sc_ref,
         hn_ref, xn_ref) = refs
    acc = (_dot(ya_ref[...], w_ref[0:2 * GW, :]) + _dot(ymb_ref[...], w_ref[2 * GW:3 * GW, :])
           + _dot(ygla_ref[...], w_ref[3 * GW:4 * GW, :]))
    hn = h_ref[...] + gate_ref[...] * acc
    hn_ref[...] = hn
    xn = _rms(hn, g_ref[...]) * (1.0 + sc_ref[...]) + sh_ref[...]
    xn_ref[...] = xn.astype(xn_ref.dtype)
    if not route:
        return

    tm = hn.shape[0]

    @pl.when(pl.program_id(0) == 0)
    def _():
        carry_ref[...] = jnp.zeros_like(carry_ref)

    xh = xn.astype(BF16)
    xl = (xn - xh.astype(F32)).astype(BF16)
    wr = wr_ref[...]
    wh = wr.astype(BF16)
    wl = (wr - wh.astype(F32)).astype(BF16)
    logits = _dot(xh, wh) + _dot(xh, wl) + _dot(xl, wh)
    lane = lax.broadcasted_iota(jnp.int32, (tm, LANE), 1)
    lane_f = lane.astype(F32)
    neg = jnp.float32(-3.0e38)
    lm = jnp.where(lane < N_EXPERTS, logits, neg)
    m1 = jnp.max(lm, axis=-1, keepdims=True)
    i1 = jnp.min(jnp.where(lm == m1, lane_f, float(LANE)), axis=-1, keepdims=True)
    lm2 = jnp.where(lane_f == i1, neg, lm)
    m2 = jnp.max(lm2, axis=-1, keepdims=True)
    i2 = jnp.min(jnp.where(lm2 == m2, lane_f, float(LANE)), axis=-1, keepdims=True)
    e = jnp.exp(m2 - m1)
    w1 = 1.0 / (1.0 + e)
    w2 = e / (1.0 + e)
    sel1 = lane_f == i1
    sel2 = lane_f == i2
    onehot = jnp.where(sel1 | sel2, 1.0, 0.0)
    ti = lax.broadcasted_iota(jnp.int32, (tm, tm), 0)
    tj = lax.broadcasted_iota(jnp.int32, (tm, tm), 1)
    incl = _dot((tj <= ti).astype(BF16), onehot.astype(BF16)) + carry_ref[...]
    excl = incl - onehot
    r1 = jnp.sum(jnp.where(sel1, excl, 0.0), axis=-1, keepdims=True)
    r2 = jnp.sum(jnp.where(sel2, excl, 0.0), axis=-1, keepdims=True)
    info = jnp.where(lane == 0, i1, jnp.where(lane == 1, i2, jnp.where(lane == 2, r1, jnp.where(lane == 3, r2, 0.0))))
    ri_ref[...] = info.astype(jnp.int32)
    rw_ref[...] = jnp.where(lane == 0, w1, jnp.where(lane == 1, w2, 0.0))
    carry_ref[...] = incl[tm - 1:tm, :]
    cnt_ref[...] = incl[tm - 1:tm, :]


def _outproj(h, ya, ymb, ygla, w, modr, row_fn, g, w_router=None):
    m = h.shape[0]
    tm = 512
    route = w_router is not None
    mod = lambda k: pl.BlockSpec((None, None, 1, D), lambda i: (row_fn(i), k, 0, 0))
    in_specs = [pl.BlockSpec((tm, D), lambda i: (i, 0)),
                pl.BlockSpec((tm, 2 * GW), lambda i: (i, 0)),
                pl.BlockSpec((tm, GW), lambda i: (i, 0)),
                pl.BlockSpec((tm, GW), lambda i: (i, 0)),
                pl.BlockSpec((D, D), lambda i: (0, 0)),
                mod(2), pl.BlockSpec((1, D), lambda i: (0, 0)), mod(3), mod(4)]
    args = [h, ya, ymb, ygla, w, modr, g, modr, modr]
    out_shape = [jax.ShapeDtypeStruct((m, D), F32), jax.ShapeDtypeStruct((m, D), F32 if route else BF16)]
    out_specs = [pl.BlockSpec((tm, D), lambda i: (i, 0)), pl.BlockSpec((tm, D), lambda i: (i, 0))]
    scratch = []
    if route:
        in_specs.append(pl.BlockSpec((D, LANE), lambda i: (0, 0)))
        args.append(w_router)
        out_shape += [jax.ShapeDtypeStruct((m, LANE), jnp.int32), jax.ShapeDtypeStruct((m, LANE), F32),
                      jax.ShapeDtypeStruct((1, LANE), F32)]
        out_specs += [pl.BlockSpec((tm, LANE), lambda i: (i, 0)), pl.BlockSpec((tm, LANE), lambda i: (i, 0)),
                      pl.BlockSpec((1, LANE), lambda i: (0, 0))]
        scratch = [pltpu.VMEM((1, LANE), F32)]
    return pl.pallas_call(
        functools.partial(_outproj_kernel, route=route),
        out_shape=tuple(out_shape),
        grid=(m // tm,),
        in_specs=in_specs,
        out_specs=tuple(out_specs),
        scratch_shapes=scratch,
        name="outproj_route" if route else "outproj",
    )(*args)


def _ffn_kernel(x_ref, h_ref, gate_ref, wg_ref, wu_ref, wd_ref, o_ref):
    @pl.when(pl.program_id(1) == 0)
    def _():
        o_ref[...] = h_ref[...]

    x = x_ref[...]
    hidden = (_silu(_dot(x, wg_ref[...])) * _dot(x, wu_ref[...])).astype(BF16)
    o_ref[...] += gate_ref[...] * _dot(hidden, wd_ref[...])


def _ffn(xn, h, modr, row_fn, wg, wu, wd):
    m = h.shape[0]
    tm = 512
    tf = 512
    nf = wg.shape[1] // tf
    return pl.pallas_call(
        _ffn_kernel,
        out_shape=jax.ShapeDtypeStruct((m, D), F32),
        grid=(m // tm, nf),
        in_specs=[pl.BlockSpec((tm, D), lambda i, f: (i, 0)),
                  pl.BlockSpec((tm, D), lambda i, f: (i, 0)),
                  pl.BlockSpec((None, None, 1, D), lambda i, f: (row_fn(i), 5, 0, 0)),
                  pl.BlockSpec((D, tf), lambda i, f: (0, f)),
                  pl.BlockSpec((D, tf), lambda i, f: (0, f)),
                  pl.BlockSpec((tf, D), lambda i, f: (f, 0))],
        out_specs=pl.BlockSpec((tm, D), lambda i, f: (i, 0)),
        name="ffn",
    )(xn, h, modr, wg, wu, wd)


def _row_copy(src_hbm, row, dst, r, sem):
    return pltpu.make_async_copy(src_hbm.at[pl.ds(row, 1)], dst.at[pl.ds(r, 1)], sem)


def _dispatch_kernel(src_ref, nblk_ref, x_hbm, o_ref, buf_ref, sem):
    i = pl.program_id(0)
    r_blk = buf_ref.shape[0]

    @pl.when(i < nblk_ref[0])
    def _():
        def issue(r, carry):
            _row_copy(x_hbm, src_ref[i * r_blk + r], buf_ref, r, sem).start()
            return carry

        lax.fori_loop(0, r_blk, issue, 0)

        def wait(r, carry):
            _row_copy(x_hbm, 0, buf_ref, r, sem).wait()
            return carry

        lax.fori_loop(0, r_blk, wait, 0)
        o_ref[...] = buf_ref[...].astype(BF16)

    @pl.when(i >= nblk_ref[0])
    def _():
        o_ref[...] = jnp.zeros_like(o_ref)


def _dispatch(xn, src, nblk):
    rows = src.shape[0]
    r = GATHER_R
    return pl.pallas_call(
        _dispatch_kernel,
        out_shape=jax.ShapeDtypeStruct((rows, D), BF16),
        grid_spec=pltpu.PrefetchScalarGridSpec(
            num_scalar_prefetch=2,
            grid=(rows // r,),
            in_specs=[pl.BlockSpec(memory_space=pl.ANY)],
            out_specs=pl.BlockSpec((r, D), lambda i, src, nb: (i, 0)),
            scratch_shapes=[pltpu.VMEM((r, D), F32), pltpu.SemaphoreType.DMA(())]),
        name="moe_dispatch",
    )(src, nblk, xn)


def _expert_kernel(te_ref, nv_ref, nu_ref, x_ref, wg_ref, wu_ref, wd_ref, o_ref, wgb_ref, wub_ref, wdb_ref):
    i = pl.program_id(0)
    f = pl.program_id(1)

    @pl.when(i < nu_ref[0])
    def _():
        @pl.when(f == 0)
        def _():
            o_ref[...] = jnp.zeros_like(o_ref)

        wgb_ref[...] = wg_ref[...].astype(BF16)
        wub_ref[...] = wu_ref[...].astype(BF16)
        wdb_ref[...] = wd_ref[...].astype(BF16)
        for sb in range(MOE_TM // MOE_SB):
            @pl.when(sb * MOE_SB < nv_ref[i])
            def _():
                rows = slice(sb * MOE_SB, (sb + 1) * MOE_SB)
                x = x_ref[rows, :]
                hidden = (_silu(_dot(x, wgb_ref[...])) * _dot(x, wub_ref[...])).astype(BF16)
                o_ref[rows, :] += _dot(hidden, wdb_ref[...])

    @pl.when((i >= nu_ref[0]) & (f == 0))
    def _():
        o_ref[...] = jnp.zeros_like(o_ref)


def _experts(xs, te, nv, nu, wg, wu, wd):
    rows = xs.shape[0]
    dff = wg.shape[2]
    nf = dff // MOE_TF

    def row_map(i, f, te, nv, nu):
        return (jnp.minimum(i, nu[0] - 1), 0)

    def f_eff(i, f, nu):
        return jnp.where(i < nu[0], f, nf - 1)

    return pl.pallas_call(
        _expert_kernel,
        out_shape=jax.ShapeDtypeStruct((rows, D), F32),
        grid_spec=pltpu.PrefetchScalarGridSpec(
            num_scalar_prefetch=3,
            grid=(rows // MOE_TM, nf),
            in_specs=[pl.BlockSpec((MOE_TM, D), row_map),
                      pl.BlockSpec((None, D, MOE_TF), lambda i, f, te, nv, nu: (te[i], 0, f_eff(i, f, nu))),
                      pl.BlockSpec((None, D, MOE_TF), lambda i, f, te, nv, nu: (te[i], 0, f_eff(i, f, nu))),
                      pl.BlockSpec((None, MOE_TF, D), lambda i, f, te, nv, nu: (te[i], f_eff(i, f, nu), 0))],
            out_specs=pl.BlockSpec((MOE_TM, D), lambda i, f, te, nv, nu: (i, 0)),
            scratch_shapes=[pltpu.VMEM((D, MOE_TF), BF16), pltpu.VMEM((D, MOE_TF), BF16),
                            pltpu.VMEM((MOE_TF, D), BF16)]),
        name="moe_experts",
    )(te, nv, nu, xs, wg, wu, wd)


def _combine_kernel(pos_ref, h_ref, rw_ref, gate_ref, gfin_ref, y_hbm, o_ref, buf_ref, sem):
    i = pl.program_id(0)
    r_blk = h_ref.shape[0]

    def issue(r, carry):
        t = i * r_blk + r
        _row_copy(y_hbm, pos_ref[2 * t], buf_ref.at[0], r, sem).start()
        _row_copy(y_hbm, pos_ref[2 * t + 1], buf_ref.at[1], r, sem).start()
        return carry

    lax.fori_loop(0, r_blk, issue, 0)

    def wait(r, carry):
        _row_copy(y_hbm, 0, buf_ref.at[0], r, sem).wait()
        _row_copy(y_hbm, 0, buf_ref.at[1], r, sem).wait()
        return carry

    lax.fori_loop(0, r_blk, wait, 0)
    y = rw_ref[:, 0:1] * buf_ref[0] + rw_ref[:, 1:2] * buf_ref[1]
    hn = h_ref[...] + gate_ref[...] * y
    o_ref[...] = _rms(hn, gfin_ref[...])


def _combine(pos, h, rw, modr, row_fn, gfin, ys):
    m = h.shape[0]
    r = GATHER_R
    return pl.pallas_call(
        _combine_kernel,
        out_shape=jax.ShapeDtypeStruct((m, D), F32),
        grid_spec=pltpu.PrefetchScalarGridSpec(
            num_scalar_prefetch=1,
            grid=(m // r,),
            in_specs=[pl.BlockSpec((r, D), lambda i, pos: (i, 0)),
                      pl.BlockSpec((r, LANE), lambda i, pos: (i, 0)),
                      pl.BlockSpec((None, None, 1, D), lambda i, pos: (row_fn(i), 5, 0, 0)),
                      pl.BlockSpec((1, D), lambda i, pos: (0, 0)),
                      pl.BlockSpec(memory_space=pl.ANY)],
            out_specs=pl.BlockSpec((r, D), lambda i, pos: (i, 0)),
            scratch_shapes=[pltpu.VMEM((2, r, D), F32), pltpu.SemaphoreType.DMA(())]),
        name="moe_combine",
    )(pos, h, rw, modr, gfin, ys)


def _moe(xn, h, ri, rw, cnt, modr, row_fn, gfin, wg, wu, wd):
    m = h.shape[0]
    tm = MOE_TM
    nt = MOE_NT
    counts = cnt[0, :N_EXPERTS].astype(jnp.int32)
    nt_e = (counts + tm - 1) // tm
    t_end = jnp.cumsum(nt_e)
    t_start = t_end - nt_e
    n_used = t_end[-1]
    tid = jnp.arange(nt, dtype=jnp.int32)
    te_raw = jnp.sum((tid[:, None] >= t_end[None, :]).astype(jnp.int32), axis=1)
    te_last = jnp.sum((n_used - 1 >= t_end).astype(jnp.int32))
    te = jnp.minimum(te_raw, te_last).astype(jnp.int32)
    nv = jnp.where(tid < n_used, jnp.clip(counts[te] - (tid - t_start[te]) * tm, 0, tm), 0).astype(jnp.int32)
    pos = (t_start[ri[:, 0:2]] * tm + ri[:, 2:4]).astype(jnp.int32)
    tok = jnp.repeat(jnp.arange(m, dtype=jnp.int32), 2)
    src = jnp.zeros((nt * tm,), jnp.int32).at[pos.reshape(-1)].set(tok)
    nu = n_used.reshape(1).astype(jnp.int32)
    xs = _dispatch(xn, src, nu * (tm // GATHER_R))
    ys = _experts(xs, te, nv, nu, wg, wu, wd)
    return _combine(pos.reshape(-1), h, rw, modr, row_fn, gfin, ys)


def _relayout_w_in(w):
    pad = jnp.zeros((w.shape[0], P_PAD - 5680), w.dtype)
    return jnp.concatenate([w[:, :4096], w[:, 4112:5648], w[:, 4096:4112], w[:, 5648:5680], pad], axis=1).astype(BF16)


def _head_rows(v, direction):
    return jnp.zeros((1, LANE), F32).at[0, 8 * direction:8 * direction + SSD_HEADS].set(v.astype(F32))


def kernel(x, c, ctx, c_ctx, w_ada, b_ada, g_mix, g_ffn, w_in, w_out, sc_conv_w, cm_w_s, cm_b_s, mb_conv_w,
           mb_conv_b, mb_a_log, mb_dt_bias, mb_d, mb_norm_g, gla_w_gate, gla_b_gate, gla_norm_g, ffn_w_gate,
           ffn_w_up, ffn_w_down, moe_router, moe_w_gate, moe_w_up, moe_w_down, g_final):
    nb, seq, _ = x.shape
    ctx_len = ctx.shape[1]
    depth = w_ada.shape[0]
    m_lat = nb * seq
    h = x.reshape(m_lat, D)
    hc = ctx.reshape(nb * ctx_len, D)

    cond = jnp.concatenate([c, c_ctx[None, :]], axis=0)
    cb = jnp.broadcast_to(cond[:, :, None], (nb + 1, D, LANE))
    mods = _ada(cb, w_ada, b_ada).reshape(depth, 8, N_MOD, 1, D)

    tm_in = 1024
    lat_row_in = lambda i: i // (seq // tm_in)
    lat_row_512 = lambda i: i // (seq // 512)
    lat_row_g = lambda i: i // (seq // GATHER_R)
    ctx_row = lambda i: nb

    expand = [jnp.zeros((LANE, GW), F32).at[8 * d + jnp.arange(GW) // SSD_HD, jnp.arange(GW)].set(1.0).astype(BF16)
              for d in range(2)]
    out = None
    for i in range(depth):
        last = i == depth - 1
        modr = mods[i]
        w_in_r = _relayout_w_in(w_in[i])
        g_mix_i = g_mix[i].reshape(1, D)
        g_ffn_i = g_ffn[i].reshape(1, D)
        p = _inproj(h, g_mix_i, modr, lat_row_in, w_in_r, tm_in)
        pc = _inproj(hc, g_mix_i, modr, ctx_row, w_in_r, nb * ctx_len)

        ws = cm_w_s[i].astype(BF16)
        bs = jnp.repeat(cm_b_s[i].T, 128, axis=1)
        ya = _scm(p, sc_conv_w[i], ws, bs, GRID_W)
        if not last:
            yac = _scm(pc, sc_conv_w[i], ws, bs, ctx_len)

        conv_b = mb_conv_b[i].reshape(1, -1)
        s_zero = jnp.zeros((nb, SSD_Q, GW), F32)
        skip = jnp.repeat(mb_d[i, 0] + mb_d[i, 1], SSD_HD).reshape(1, GW)
        fin = (skip, mb_norm_g[i].reshape(1, GW))
        ssd_c = [(mb_conv_w[i], conv_b, _head_rows(mb_a_log[i, d], d), _head_rows(mb_dt_bias[i, d], d), expand[d])
                 for d in range(2)]
        y0c, s0c = _ssd(pc, s_zero, ssd_c[0], nb, ctx_len, 0, False)
        y0, _ = _ssd(p, s0c, ssd_c[0], nb, GRID_W, 0, False)
        if last:
            _, s1c = _ssd(pc, s_zero, ssd_c[1], nb, ctx_len, 1, False)
        else:
            ymbc, s1c = _ssd(pc, s_zero, ssd_c[1], nb, ctx_len, 1, True, y0c, fin)
        ymb, _ = _ssd(p, s1c, ssd_c[1], nb, GRID_W, 1, True, y0, fin)

        g_zero = jnp.zeros((nb, GLA_DV, 256), F32)
        gla_c = [(jnp.zeros((LANE, 256), F32).at[16 + 16 * d:32 + 16 * d].set(gla_w_gate[i, d]).astype(BF16),
                  gla_b_gate[i, d].reshape(1, 256)) for d in range(2)]
        ng = gla_norm_g[i].reshape(1, GLA_DV)
        o0c, t0c = _gla(pc, g_zero, gla_c[0], nb, 0, False)
        o0, _ = _gla(p, t0c, gla_c[0], nb, 0, False)
        if last:
            _, t1c = _gla(pc, g_zero, gla_c[1], nb, 1, False)
        else:
            yglac, t1c = _gla(pc, g_zero, gla_c[1], nb, 1, True, o0c, ng)
        ygla = _gla(p, t1c, gla_c[1], nb, 1, True, o0, ng)[0]

        w_out_b = w_out[i].astype(BF16)
        j = i // 2
        if i % 2 == 0:
            wg, wu, wd = ffn_w_gate[j].astype(BF16), ffn_w_up[j].astype(BF16), ffn_w_down[j].astype(BF16)
            hn, xn = _outproj(h, ya, ymb, ygla, w_out_b, modr, lat_row_512, g_ffn_i)
            h = _ffn(xn, hn, modr, lat_row_512, wg, wu, wd)
            if not last:
                hcn, xcn = _outproj(hc, yac, ymbc, yglac, w_out_b, modr, ctx_row, g_ffn_i)
                hc = _ffn(xcn, hcn, modr, ctx_row, wg, wu, wd)
            if last:
                out = h
        else:
            w_router = jnp.zeros((D, LANE), F32).at[:, :N_EXPERTS].set(moe_router[j])
            hn, xn, ri, rw, cnt = _outproj(h, ya, ymb, ygla, w_out_b, modr, lat_row_512, g_ffn_i, w_router)
            if last:
                out = _moe(xn, hn, ri, rw, cnt, modr, lat_row_g, g_final.reshape(1, D),
                           moe_w_gate[j], moe_w_up[j], moe_w_down[j])
            else:
                raise NotImplementedError("routed FFN is only implemented as the last layer's channel mixer")
    return out.reshape(nb, seq, D)
```

```python
import functools

import jax
import jax.numpy as jnp
import numpy as np
from jax import lax
from jax.experimental import pallas as pl
from jax.experimental.pallas import tpu as pltpu

F32 = jnp.float32
BF16 = jnp.bfloat16
EPS = 1e-6

D = 2048
GW = D // 4
GRID_W = 64
N_MOD = 6
LANE = 128
P_PAD = 5760
COL_SMALL = 5632

SSD_Q = 128
SSD_HEADS = 8
SSD_HD = 64
GLA_C = 64
GLA_HEADS = 4
GLA_DK = 64
GLA_DV = 128
GLA_RANK = 16
N_EXPERTS = 8

MOE_TM = 1024
MOE_SB = 256
MOE_TF = 512
MOE_NT = 2 * 8192 // MOE_TM + N_EXPERTS
GATHER_R = 256


def _silu(x):
    return x * jax.nn.sigmoid(x)


def _softplus(x):
    return jnp.maximum(x, 0.0) + jnp.log1p(jnp.exp(-jnp.abs(x)))


def _split3(x):
    hi = x.astype(BF16)
    r = x - hi.astype(F32)
    mid = r.astype(BF16)
    lo = (r - mid.astype(F32)).astype(BF16)
    return hi, mid, lo


def _dot(a, b):
    return jnp.dot(a, b, preferred_element_type=F32)


def _dot_nt(a, b):
    return lax.dot_general(a, b, (((1,), (1,)), ((), ())), preferred_element_type=F32)


def _dot_tn(a, b):
    return lax.dot_general(a, b, (((0,), (0,)), ((), ())), preferred_element_type=F32)


def _dot01_left(m01, x):
    hi, mid, lo = _split3(x)
    return _dot(m01, hi) + _dot(m01, mid) + _dot(m01, lo)


def _dot01_right(x, m01):
    hi, mid, lo = _split3(x)
    return _dot(hi, m01) + _dot(mid, m01) + _dot(lo, m01)


def _rms(x, g):
    return x * lax.rsqrt(jnp.mean(x * x, axis=-1, keepdims=True) + EPS) * g


def _conv3(x, w, seg):
    n = x.shape[0]
    t = lax.broadcasted_iota(jnp.int32, (n, 1), 0) & (seg - 1)
    prev = jnp.where(t == 0, 0.0, pltpu.roll(x, 1, 0))
    nxt = jnp.where(t == seg - 1, 0.0, pltpu.roll(x, n - 1, 0))
    return prev * w[0:1] + x * w[1:2] + nxt * w[2:3]


def _ada_kernel(cb_ref, w_ref, b_ref, o_ref, s_ref):
    @pl.when((pl.program_id(0) == 0) & (pl.program_id(1) == 0))
    def _():
        s_ref[...] = _silu(cb_ref[...])

    tn = w_ref.shape[1]
    o_ref[...] = jnp.zeros_like(o_ref)
    for j in range(tn // LANE):
        wj = w_ref[:, j * LANE:(j + 1) * LANE]
        for m in range(3):
            o_ref[m:m + 1, j * LANE:(j + 1) * LANE] = (
                jnp.sum(wj * s_ref[m], axis=0, keepdims=True) + b_ref[:, j * LANE:(j + 1) * LANE])


def _ada(cb, w_ada, b_ada):
    depth = w_ada.shape[0]
    n = w_ada.shape[2]
    tn = 1024
    return pl.pallas_call(
        _ada_kernel,
        out_shape=jax.ShapeDtypeStruct((depth, 8, n), F32),
        grid=(depth, n // tn),
        in_specs=[pl.BlockSpec((3, D, LANE), lambda l, j: (0, 0, 0)),
                  pl.BlockSpec((None, D, tn), lambda l, j: (l, 0, j)),
                  pl.BlockSpec((None, 1, tn), lambda l, j: (l, 0, j))],
        out_specs=pl.BlockSpec((None, 8, tn), lambda l, j: (l, 0, j)),
        scratch_shapes=[pltpu.VMEM((3, D, LANE), F32)],
        name="ada",
    )(cb, w_ada, b_ada.reshape(depth, 1, n))


def _inproj_kernel(h_ref, g_ref, sh_ref, sc_ref, w_ref, o_ref, xn_ref):
    @pl.when(pl.program_id(1) == 0)
    def _():
        xn = _rms(h_ref[...], g_ref[...])
        xn_ref[...] = (xn * (1.0 + sc_ref[...]) + sh_ref[...]).astype(BF16)

    o_ref[...] = _dot(xn_ref[...], w_ref[...])


def _mod_spec(row_fn, k):
    return pl.BlockSpec((None, None, 1, D), lambda i, j: (row_fn(i), k, 0, 0))


def _inproj(h, g, modr, row_fn, w, tm):
    m = h.shape[0]
    tn = 1152
    return pl.pallas_call(
        _inproj_kernel,
        out_shape=jax.ShapeDtypeStruct((m, P_PAD), F32),
        grid=(m // tm, P_PAD // tn),
        in_specs=[pl.BlockSpec((tm, D), lambda i, j: (i, 0)),
                  pl.BlockSpec((1, D), lambda i, j: (0, 0)),
                  _mod_spec(row_fn, 0), _mod_spec(row_fn, 1),
                  pl.BlockSpec((D, tn), lambda i, j: (0, j))],
        out_specs=pl.BlockSpec((tm, tn), lambda i, j: (i, j)),
        scratch_shapes=[pltpu.VMEM((tm, D), BF16)],
        name="inproj",
    )(h, g, modr, modr, w)


def _gelu_tanh(x):
    c = 0.7978845608028654
    return x * (0.5 * (1.0 + jnp.tanh(c * (x + 0.044715 * (x * x * x)))))


def _scm_kernel(sc_ref, u_ref, v_ref, cw_ref, ws_ref, bs_ref, o_ref, *, seg):
    t = sc_ref.shape[0]
    bgate = sc_ref[:, 0:GW]
    y_sc = bgate * _conv3(sc_ref[:, GW:2 * GW] * sc_ref[:, 2 * GW:3 * GW], cw_ref[...], seg)
    o_ref[:, 0:GW] = y_sc.astype(BF16)

    u = _gelu_tanh(u_ref[...])
    v = _gelu_tanh(v_ref[...]).astype(BF16)
    for c in range(t // 128):
        rows = slice(c * 128, (c + 1) * 128)
        for g in range(4):
            cols = slice(g * 128, (g + 1) * 128)
            s = _dot(ws_ref[g], v[rows, cols]) + bs_ref[:, cols]
            o_ref[rows, GW + g * 128:GW + (g + 1) * 128] = (u[rows, cols] * s).astype(BF16)


def _scm(p, cw, ws, bs, seg):
    m = p.shape[0]
    t = 512
    return pl.pallas_call(
        functools.partial(_scm_kernel, seg=seg),
        out_shape=jax.ShapeDtypeStruct((m, 2 * GW), BF16),
        grid=(m // t,),
        in_specs=[pl.BlockSpec((t, 3 * GW), lambda i: (i, 0)),
                  pl.BlockSpec((t, GW), lambda i: (i, 3)),
                  pl.BlockSpec((t, GW), lambda i: (i, 4)),
                  pl.BlockSpec((3, GW), lambda i: (0, 0)),
                  pl.BlockSpec((4, 128, 128), lambda i: (0, 0, 0)),
                  pl.BlockSpec((128, GW), lambda i: (0, 0))],
        out_specs=pl.BlockSpec((t, 2 * GW), lambda i: (i, 0)),
        name="scm",
    )(p, p, p, cw, ws, bs)


def _ssd_kernel(*refs, direction, final, seg):
    if final:
        (xbc_ref, sm_ref, cw_ref, cb_ref, alog_ref, dtb_ref, e_ref, s0_ref,
         y0_ref, z_ref, skip_ref, ng_ref, y_ref, sfin_ref, st_ref) = refs
    else:
        (xbc_ref, sm_ref, cw_ref, cb_ref, alog_ref, dtb_ref, e_ref, s0_ref,
         y_ref, sfin_ref, st_ref) = refs
    j = pl.program_id(1)
    tb = xbc_ref.shape[0]
    q = SSD_Q

    @pl.when(j == 0)
    def _():
        st_ref[...] = s0_ref[...]

    xc = _silu(_conv3(xbc_ref[...], cw_ref[...], seg) + cb_ref[...])

    a_neg = -jnp.exp(alog_ref[...])
    ii = lax.broadcasted_iota(jnp.int32, (q, q), 0)
    jj = lax.broadcasted_iota(jnp.int32, (q, q), 1)
    mask = (jj <= ii) if direction == 0 else (jj >= ii)
    tri = mask.astype(BF16)
    last = q - 1 if direction == 0 else 0
    lane0 = 8 * direction
    upper_half = lax.broadcasted_iota(jnp.int32, (q, 128), 1) >= SSD_HD
    expand = e_ref[...]

    nchunk = tb // q
    order = range(nchunk) if direction == 0 else range(nchunk - 1, -1, -1)
    for c in order:
        rows = slice(c * q, (c + 1) * q)
        dt = _softplus(sm_ref[rows, :] + dtb_ref[...])
        acs = _dot01_left(tri, dt * a_neg)
        tot = acs[last:last + 1, :]
        acs_t = acs.T
        dt_t = dt.T
        wx = _dot01_right(jnp.exp(tot - acs) * dt, expand)
        ea = _dot01_right(jnp.exp(acs), expand)
        xs = xc[rows, 0:GW]
        xw = (wx * xs).astype(BF16)
        st = st_ref[...]
        stb = st.astype(BF16)
        ys = []
        for g in range(2):
            bg = xc[rows, GW + g * 128:GW + (g + 1) * 128].astype(BF16)
            cg = xc[rows, GW + 256 + g * 128:GW + 256 + (g + 1) * 128].astype(BF16)
            cb = _dot_nt(cg, bg)
            gcols = slice(g * 256, (g + 1) * 256)
            yint = _dot(cg, stb[:, gcols])
            for pr in range(2):
                pidx = 2 * g + pr
                pcols = slice(pidx * 128, (pidx + 1) * 128)
                xp = xs[:, pcols]
                acc = yint[:, pr * 128:(pr + 1) * 128] * ea[:, pcols]
                for hh in range(2):
                    ln = lane0 + 2 * pidx + hh
                    sg = acs[:, ln:ln + 1] - acs_t[ln:ln + 1, :]
                    decay = jnp.where(mask, jnp.exp(jnp.where(mask, sg, 0.0)), 0.0)
                    scores = (cb * decay * dt_t[ln:ln + 1, :]).astype(BF16)
                    xh = jnp.where(upper_half == (hh == 1), xp, 0.0).astype(BF16)
                    acc = acc + _dot(scores, xh)
                ys.append(acc)
            st_ref[:, gcols] = ea[last:last + 1, gcols] * st[:, gcols] + _dot_tn(bg, xw[:, gcols])
        y = jnp.concatenate(ys, axis=1)
        if final:
            y = y0_ref[rows, :] + y + xs * skip_ref[...]
            y = y * _silu(z_ref[rows, :])
            y_ref[rows, :] = _rms(y, ng_ref[...]).astype(y_ref.dtype)
        else:
            y_ref[rows, :] = y

    @pl.when(j == pl.num_programs(1) - 1)
    def _():
        sfin_ref[...] = st_ref[...]


def _ssd(p, s0, consts, nb, seg, direction, final, y0=None, fin=None):
    m = p.shape[0]
    tb = 256
    nblk = m // nb // tb
    if direction == 0:
        blk = lambda b, j: b * nblk + j
    else:
        blk = lambda b, j: b * nblk + (nblk - 1 - j)
    full = lambda shape: pl.BlockSpec(shape, lambda b, j: (0,) * len(shape))
    in_specs = [pl.BlockSpec((tb, 1024), lambda b, j: (blk(b, j), 3)),
                pl.BlockSpec((tb, LANE), lambda b, j: (blk(b, j), COL_SMALL // LANE)),
                full((3, 1024)), full((1, 1024)), full((1, LANE)), full((1, LANE)), full((LANE, GW)),
                pl.BlockSpec((None, SSD_Q, GW), lambda b, j: (b, 0, 0))]
    args = [p, p, *consts, s0]
    if final:
        in_specs += [pl.BlockSpec((tb, GW), lambda b, j: (blk(b, j), 0)),
                     pl.BlockSpec((tb, GW), lambda b, j: (blk(b, j), 5)),
                     full((1, GW)), full((1, GW))]
        args += [y0, p, *fin]
    return pl.pallas_call(
        functools.partial(_ssd_kernel, direction=direction, final=final, seg=seg),
        out_shape=(jax.ShapeDtypeStruct((m, GW), BF16 if final else F32),
                   jax.ShapeDtypeStruct((nb, SSD_Q, GW), F32)),
        grid=(nb, nblk),
        in_specs=in_specs,
        out_specs=(pl.BlockSpec((tb, GW), lambda b, j: (blk(b, j), 0)),
                   pl.BlockSpec((None, SSD_Q, GW), lambda b, j: (b, 0, 0))),
        scratch_shapes=[pltpu.VMEM((SSD_Q, GW), F32)],
        name=f"ssd_d{direction}",
    )(*args)


def _gla_kernel(*refs, direction, final):
    if final:
        (q_ref, k_ref, v_ref, sm_ref, wg_ref, bg_ref, s0_ref, o0_ref, g_ref, ng_ref,
         o_ref, sfin_ref, st_ref) = refs
    else:
        (q_ref, k_ref, v_ref, sm_ref, wg_ref, bg_ref, s0_ref, o_ref, sfin_ref, st_ref) = refs
    j = pl.program_id(1)
    tb = q_ref.shape[0]
    c_len = GLA_C

    @pl.when(j == 0)
    def _():
        st_ref[...] = s0_ref[...]

    gk = -_softplus(-(_dot(sm_ref[...].astype(BF16), wg_ref[...]) + bg_ref[...])) * (1.0 / 16.0)
    ii = lax.broadcasted_iota(jnp.int32, (tb, tb), 0)
    jj = lax.broadcasted_iota(jnp.int32, (tb, tb), 1)
    same = (ii >> 6) == (jj >> 6)
    btri = (same & ((jj <= ii) if direction == 0 else (jj >= ii))).astype(BF16)
    gcum = _dot01_left(btri, gk)

    i64 = lax.broadcasted_iota(jnp.int32, (c_len, c_len), 0)
    j64 = lax.broadcasted_iota(jnp.int32, (c_len, c_len), 1)
    causal = (j64 <= i64) if direction == 0 else (j64 >= i64)
    last = c_len - 1 if direction == 0 else 0
    ref = c_len // 2 if direction == 0 else c_len - 1 - c_len // 2
    upper_half = lax.broadcasted_iota(jnp.int32, (c_len, 128), 1) >= GLA_DK

    nchunk = tb // c_len
    order = range(nchunk) if direction == 0 else range(nchunk - 1, -1, -1)
    outs = [None] * nchunk
    for c in order:
        rows = slice(c * c_len, (c + 1) * c_len)
        gc = gcum[rows, :]
        gl = gc[last:last + 1, :]
        gr = gc[ref:ref + 1, :]
        qc = q_ref[rows, :] * (GLA_DK ** -0.5)
        kc = k_ref[rows, :]
        vc = v_ref[rows, :].astype(BF16)
        kdec = kc * jnp.exp(gl - gc)
        qg = qc * jnp.exp(gc)
        qa = qc * jnp.exp(gc - gr)
        kb = (kc * jnp.exp(gr - gc)).astype(BF16)
        st = st_ref[...]
        stb = st.astype(BF16)
        heads = []
        for pr in range(2):
            pcols = slice(pr * 128, (pr + 1) * 128)
            contrib = jnp.zeros((GLA_DV, 128), F32)
            for hh in range(2):
                h = 2 * pr + hh
                mh = upper_half == (hh == 1)
                zero = jnp.zeros((c_len, 128), F32)
                att = _dot_nt(jnp.where(mh, qa[:, pcols], zero).astype(BF16), kb[:, pcols])
                att = jnp.where(causal, att, 0.0).astype(BF16)
                vh = vc[:, h * GLA_DV:(h + 1) * GLA_DV]
                yh = _dot(att, vh) + _dot_nt(jnp.where(mh, qg[:, pcols], zero).astype(BF16), stb[:, pcols])
                heads.append(yh)
                contrib = contrib + _dot_tn(vh, jnp.where(mh, kdec[:, pcols], zero).astype(BF16))
            st_ref[:, pcols] = jnp.exp(gl[:, pcols]) * st[:, pcols] + contrib
        outs[c] = jnp.concatenate(heads, axis=1)
    o = jnp.concatenate(outs, axis=0)
    if final:
        o = o0_ref[...] + o
        normed = [_rms(o[:, h * GLA_DV:(h + 1) * GLA_DV], ng_ref[...]) for h in range(GLA_HEADS)]
        o_ref[...] = (jnp.concatenate(normed, axis=1) * _silu(g_ref[...])).astype(o_ref.dtype)
    else:
        o_ref[...] = o

    @pl.when(j == pl.num_programs(1) - 1)
    def _():
        sfin_ref[...] = st_ref[...]


def _gla(p, s0, consts, nb, direction, final, o0=None, ng=None):
    m = p.shape[0]
    tb = 256
    nblk = m // nb // tb
    if direction == 0:
        blk = lambda b, j: b * nblk + j
    else:
        blk = lambda b, j: b * nblk + (nblk - 1 - j)
    full = lambda shape: pl.BlockSpec(shape, lambda b, j: (0,) * len(shape))
    in_specs = [pl.BlockSpec((tb, 256), lambda b, j: (blk(b, j), 16)),
                pl.BlockSpec((tb, 256), lambda b, j: (blk(b, j), 17)),
                pl.BlockSpec((tb, GW), lambda b, j: (blk(b, j), 9)),
                pl.BlockSpec((tb, LANE), lambda b, j: (blk(b, j), COL_SMALL // LANE)),
                full((LANE, 256)), full((1, 256)),
                pl.BlockSpec((None, GLA_DV, 256), lambda b, j: (b, 0, 0))]
    args = [p, p, p, p, *consts, s0]
    if final:
        in_specs += [pl.BlockSpec((tb, GW), lambda b, j: (blk(b, j), 0)),
                     pl.BlockSpec((tb, GW), lambda b, j: (blk(b, j), 10)),
                     full((1, GLA_DV))]
        args += [o0, p, ng]
    return pl.pallas_call(
        functools.partial(_gla_kernel, direction=direction, final=final),
        out_shape=(jax.ShapeDtypeStruct((m, GW), BF16 if final else F32),
                   jax.ShapeDtypeStruct((nb, GLA_DV, 256), F32)),
        grid=(nb, nblk),
        in_specs=in_specs,
        out_specs=(pl.BlockSpec((tb, GW), lambda b, j: (blk(b, j), 0)),
                   pl.BlockSpec((None, GLA_DV, 256), lambda b, j: (b, 0, 0))),
        scratch_shapes=[pltpu.VMEM((GLA_DV, 256), F32)],
        name=f"gla_d{direction}",
    )(*args)


def _outproj_kernel(*refs, route):
    if route:
        (h_ref, ya_ref, ymb_ref, ygla_ref, w_ref, gate_ref, g_ref, sh_ref, sc_ref, wr_ref,
         hn_ref, xn_ref, ri_ref, rw_ref, cnt_ref, carry_ref) = refs
    else:
        (h_ref, ya_ref, ymb_ref, ygla_ref, w_ref, gate_ref, g_ref, sh_ref, sc_ref,
         hn_ref, xn_ref) = refs
    acc = (_dot(ya_ref[...], w_ref[0:2 * GW, :]) + _dot(ymb_ref[...], w_ref[2 * GW:3 * GW, :])
           + _dot(ygla_ref[...], w_ref[3 * GW:4 * GW, :]))
    hn = h_ref[...] + gate_ref[...] * acc
    hn_ref[...] = hn
    xn = _rms(hn, g_ref[...]) * (1.0 + sc_ref[...]) + sh_ref[...]
    xn_ref[...] = xn.astype(xn_ref.dtype)
    if not route:
        return

    tm = hn.shape[0]

    @pl.when(pl.program_id(0) == 0)
    def _():
        carry_ref[...] = jnp.zeros_like(carry_ref)

    xh = xn.astype(BF16)
    xl = (xn - xh.astype(F32)).astype(BF16)
    wr = wr_ref[...]
    wh = wr.astype(BF16)
    wl = (wr - wh.astype(F32)).astype(BF16)
    logits = _dot(xh, wh) + _dot(xh, wl) + _dot(xl, wh)
    lane = lax.broadcasted_iota(jnp.int32, (tm, LANE), 1)
    lane_f = lane.astype(F32)
    neg = jnp.float32(-3.0e38)
    lm = jnp.where(lane < N_EXPERTS, logits, neg)
    m1 = jnp.max(lm, axis=-1, keepdims=True)
    i1 = jnp.min(jnp.where(lm == m1, lane_f, float(LANE)), axis=-1, keepdims=True)
    lm2 = jnp.where(lane_f == i1, neg, lm)
    m2 = jnp.max(lm2, axis=-1, keepdims=True)
    i2 = jnp.min(jnp.where(lm2 == m2, lane_f, float(LANE)), axis=-1, keepdims=True)
    e = jnp.exp(m2 - m1)
    w1 = 1.0 / (1.0 + e)
    w2 = e / (1.0 + e)
    sel1 = lane_f == i1
    sel2 = lane_f == i2
    onehot = jnp.where(sel1 | sel2, 1.0, 0.0)
    ti = lax.broadcasted_iota(jnp.int32, (tm, tm), 0)
    tj = lax.broadcasted_iota(jnp.int32, (tm, tm), 1)
    incl = _dot((tj <= ti).astype(BF16), onehot.astype(BF16)) + carry_ref[...]
    excl = incl - onehot
    r1 = jnp.sum(jnp.where(sel1, excl, 0.0), axis=-1, keepdims=True)
    r2 = jnp.sum(jnp.where(sel2, excl, 0.0), axis=-1, keepdims=True)
    info = jnp.where(lane == 0, i1, jnp.where(lane == 1, i2, jnp.where(lane == 2, r1, jnp.where(lane == 3, r2, 0.0))))
    ri_ref[...] = info.astype(jnp.int32)
    rw_ref[...] = jnp.where(lane == 0, w1, jnp.where(lane == 1, w2, 0.0))
    carry_ref[...] = incl[tm - 1:tm, :]
    cnt_ref[...] = incl[tm - 1:tm, :]


def _outproj(h, ya, ymb, ygla, w, modr, row_fn, g, w_router=None):
    m = h.shape[0]
    tm = 512
    route = w_router is not None
    mod = lambda k: pl.BlockSpec((None, None, 1, D), lambda i: (row_fn(i), k, 0, 0))
    in_specs = [pl.BlockSpec((tm, D), lambda i: (i, 0)),
                pl.BlockSpec((tm, 2 * GW), lambda i: (i, 0)),
                pl.BlockSpec((tm, GW), lambda i: (i, 0)),
                pl.BlockSpec((tm, GW), lambda i: (i, 0)),
                pl.BlockSpec((D, D), lambda i: (0, 0)),
                mod(2), pl.BlockSpec((1, D), lambda i: (0, 0)), mod(3), mod(4)]
    args = [h, ya, ymb, ygla, w, modr, g, modr, modr]
    out_shape = [jax.ShapeDtypeStruct((m, D), F32), jax.ShapeDtypeStruct((m, D), F32 if route else BF16)]
    out_specs = [pl.BlockSpec((tm, D), lambda i: (i, 0)), pl.BlockSpec((tm, D), lambda i: (i, 0))]
    scratch = []
    if route:
        in_specs.append(pl.BlockSpec((D, LANE), lambda i: (0, 0)))
        args.append(w_router)
        out_shape += [jax.ShapeDtypeStruct((m, LANE), jnp.int32), jax.ShapeDtypeStruct((m, LANE), F32),
                      jax.ShapeDtypeStruct((1, LANE), F32)]
        out_specs += [pl.BlockSpec((tm, LANE), lambda i: (i, 0)), pl.BlockSpec((tm, LANE), lambda i: (i, 0)),
                      pl.BlockSpec((1, LANE), lambda i: (0, 0))]
        scratch = [pltpu.VMEM((1, LANE), F32)]
    return pl.pallas_call(
        functools.partial(_outproj_kernel, route=route),
        out_shape=tuple(out_shape),
        grid=(m // tm,),
        in_specs=in_specs,
        out_specs=tuple(out_specs),
        scratch_shapes=scratch,
        name="outproj_route" if route else "outproj",
    )(*args)


def _swiglu_rows(x_ref, wg_ref, wu_ref, wd_ref, o_ref, wgb_ref, wub_ref, wdb_ref, nsb, gate):
    def sub_block(sb):
        start = sb * MOE_SB
        return pl.ds(start if isinstance(sb, int) else pl.multiple_of(start, MOE_SB), MOE_SB)

    def up(sb):
        x = x_ref[sub_block(sb), :]
        return (_silu(_dot(x, wgb_ref[...])) * _dot(x, wub_ref[...])).astype(BF16)

    def down(sb, hidden):
        rows = sub_block(sb)
        y = _dot(hidden, wdb_ref[...])
        o_ref[rows, :] += y if gate is None else gate * y

    wgb_ref[...] = wg_ref[...].astype(BF16)
    wub_ref[...] = wu_ref[...].astype(BF16)
    hidden = up(0)
    wdb_ref[...] = wd_ref[...].astype(BF16)
    if isinstance(nsb, int):
        for sb in range(1, nsb):
            nxt = up(sb)
            down(sb - 1, hidden)
            hidden = nxt
    else:
        def body(sb, hid):
            nxt = up(sb)
            down(sb - 1, hid)
            return nxt

        hidden = lax.fori_loop(1, nsb, body, hidden)
    down(nsb - 1, hidden)


def _ffn_kernel(x_ref, h_hbm, gate_ref, wg_ref, wu_ref, wd_ref, o_ref, wgb_ref, wub_ref, wdb_ref):
    tm = o_ref.shape[0]

    @pl.when(pl.program_id(1) == 0)
    def _():
        pltpu.sync_copy(h_hbm.at[pl.ds(pl.program_id(0) * tm, tm)], o_ref)

    _swiglu_rows(x_ref, wg_ref, wu_ref, wd_ref, o_ref, wgb_ref, wub_ref, wdb_ref, tm // MOE_SB, gate_ref[...])


def _ffn(xn, h, modr, row_fn, wg, wu, wd, tm):
    m = h.shape[0]
    tf = MOE_TF
    nf = wg.shape[1] // tf
    return pl.pallas_call(
        _ffn_kernel,
        out_shape=jax.ShapeDtypeStruct((m, D), F32),
        grid=(m // tm, nf),
        in_specs=[pl.BlockSpec((tm, D), lambda i, f: (i, 0)),
                  pl.BlockSpec(memory_space=pl.ANY),
                  pl.BlockSpec((None, None, 1, D), lambda i, f: (row_fn(i), 5, 0, 0)),
                  pl.BlockSpec((D, tf), lambda i, f: (0, f)),
                  pl.BlockSpec((D, tf), lambda i, f: (0, f)),
                  pl.BlockSpec((tf, D), lambda i, f: (f, 0))],
        out_specs=pl.BlockSpec((tm, D), lambda i, f: (i, 0)),
        scratch_shapes=[pltpu.VMEM((D, tf), BF16), pltpu.VMEM((D, tf), BF16), pltpu.VMEM((tf, D), BF16)],
        name="ffn",
    )(xn, h, modr, wg, wu, wd)


def _row_copy(src_hbm, row, dst, r, sem):
    return pltpu.make_async_copy(src_hbm.at[pl.ds(row, 1)], dst.at[pl.ds(r, 1)], sem)


def _dispatch_kernel(src_ref, nblk_ref, x_hbm, o_ref, buf_ref, sem):
    i = pl.program_id(0)
    r_blk = buf_ref.shape[0]

    @pl.when(i < nblk_ref[0])
    def _():
        def issue(r, carry):
            _row_copy(x_hbm, src_ref[i * r_blk + r], buf_ref, r, sem).start()
            return carry

        lax.fori_loop(0, r_blk, issue, 0)

        def wait(r, carry):
            _row_copy(x_hbm, 0, buf_ref, r, sem).wait()
            return carry

        lax.fori_loop(0, r_blk, wait, 0)
        o_ref[...] = buf_ref[...].astype(BF16)

    @pl.when(i >= nblk_ref[0])
    def _():
        o_ref[...] = jnp.zeros_like(o_ref)


def _dispatch(xn, src, nblk):
    rows = src.shape[0]
    r = GATHER_R
    return pl.pallas_call(
        _dispatch_kernel,
        out_shape=jax.ShapeDtypeStruct((rows, D), BF16),
        grid_spec=pltpu.PrefetchScalarGridSpec(
            num_scalar_prefetch=2,
            grid=(rows // r,),
            in_specs=[pl.BlockSpec(memory_space=pl.ANY)],
            out_specs=pl.BlockSpec((r, D), lambda i, src, nb: (i, 0)),
            scratch_shapes=[pltpu.VMEM((r, D), F32), pltpu.SemaphoreType.DMA(())]),
        name="moe_dispatch",
    )(src, nblk, xn)


def _expert_kernel(te_ref, nv_ref, nu_ref, x_ref, wg_ref, wu_ref, wd_ref, o_ref, wgb_ref, wub_ref, wdb_ref):
    i = pl.program_id(0)
    f = pl.program_id(1)

    @pl.when(i < nu_ref[0])
    def _():
        @pl.when(f == 0)
        def _():
            o_ref[...] = jnp.zeros_like(o_ref)

        nsb = (nv_ref[i] + MOE_SB - 1) // MOE_SB
        refs = (x_ref, wg_ref, wu_ref, wd_ref, o_ref, wgb_ref, wub_ref, wdb_ref)

        @pl.when(nsb == MOE_TM // MOE_SB)
        def _():
            _swiglu_rows(*refs, MOE_TM // MOE_SB, None)

        @pl.when(nsb < MOE_TM // MOE_SB)
        def _():
            _swiglu_rows(*refs, nsb, None)

    @pl.when((i >= nu_ref[0]) & (f == 0))
    def _():
        o_ref[...] = jnp.zeros_like(o_ref)


def _experts(xs, te, nv, nu, wg, wu, wd):
    rows = xs.shape[0]
    dff = wg.shape[2]
    nf = dff // MOE_TF

    def row_map(i, f, te, nv, nu):
        return (jnp.minimum(i, nu[0] - 1), 0)

    def f_eff(i, f, nu):
        return jnp.where(i < nu[0], f, nf - 1)

    return pl.pallas_call(
        _expert_kernel,
        out_shape=jax.ShapeDtypeStruct((rows, D), F32),
        grid_spec=pltpu.PrefetchScalarGridSpec(
            num_scalar_prefetch=3,
            grid=(rows // MOE_TM, nf),
            in_specs=[pl.BlockSpec((MOE_TM, D), row_map),
                      pl.BlockSpec((None, D, MOE_TF), lambda i, f, te, nv, nu: (te[i], 0, f_eff(i, f, nu))),
                      pl.BlockSpec((None, D, MOE_TF), lambda i, f, te, nv, nu: (te[i], 0, f_eff(i, f, nu))),
                      pl.BlockSpec((None, MOE_TF, D), lambda i, f, te, nv, nu: (te[i], f_eff(i, f, nu), 0))],
            out_specs=pl.BlockSpec((MOE_TM, D), lambda i, f, te, nv, nu: (i, 0)),
            scratch_shapes=[pltpu.VMEM((D, MOE_TF), BF16), pltpu.VMEM((D, MOE_TF), BF16),
                            pltpu.VMEM((MOE_TF, D), BF16)]),
        name="moe_experts",
    )(te, nv, nu, xs, wg, wu, wd)


def _combine_kernel(pos_ref, h_ref, rw_ref, gate_ref, gfin_ref, y_hbm, o_ref, buf_ref, sem):
    i = pl.program_id(0)
    r_blk = h_ref.shape[0]

    def issue(r, carry):
        t = i * r_blk + r
        _row_copy(y_hbm, pos_ref[2 * t], buf_ref.at[0], r, sem).start()
        _row_copy(y_hbm, pos_ref[2 * t + 1], buf_ref.at[1], r, sem).start()
        return carry

    lax.fori_loop(0, r_blk, issue, 0)

    def wait(r, carry):
        _row_copy(y_hbm, 0, buf_ref.at[0], r, sem).wait()
        _row_copy(y_hbm, 0, buf_ref.at[1], r, sem).wait()
        return carry

    lax.fori_loop(0, r_blk, wait, 0)
    y = rw_ref[:, 0:1] * buf_ref[0] + rw_ref[:, 1:2] * buf_ref[1]
    hn = h_ref[...] + gate_ref[...] * y
    o_ref[...] = _rms(hn, gfin_ref[...])


def _combine(pos, h, rw, modr, row_fn, gfin, ys):
    m = h.shape[0]
    r = GATHER_R
    return pl.pallas_call(
        _combine_kernel,
        out_shape=jax.ShapeDtypeStruct((m, D), F32),
        grid_spec=pltpu.PrefetchScalarGridSpec(
            num_scalar_prefetch=1,
            grid=(m // r,),
            in_specs=[pl.BlockSpec((r, D), lambda i, pos: (i, 0)),
                      pl.BlockSpec((r, LANE), lambda i, pos: (i, 0)),
                      pl.BlockSpec((None, None, 1, D), lambda i, pos: (row_fn(i), 5, 0, 0)),
                      pl.BlockSpec((1, D), lambda i, pos: (0, 0)),
                      pl.BlockSpec(memory_space=pl.ANY)],
            out_specs=pl.BlockSpec((r, D), lambda i, pos: (i, 0)),
            scratch_shapes=[pltpu.VMEM((2, r, D), F32), pltpu.SemaphoreType.DMA(())]),
        name="moe_combine",
    )(pos, h, rw, modr, gfin, ys)


def _moe(xn, h, ri, rw, cnt, modr, row_fn, gfin, wg, wu, wd):
    m = h.shape[0]
    tm = MOE_TM
    nt = MOE_NT
    counts = cnt[0, :N_EXPERTS].astype(jnp.int32)
    nt_e = (counts + tm - 1) // tm
    t_end = jnp.cumsum(nt_e)
    t_start = t_end - nt_e
    n_used = t_end[-1]
    tid = jnp.arange(nt, dtype=jnp.int32)
    te_raw = jnp.sum((tid[:, None] >= t_end[None, :]).astype(jnp.int32), axis=1)
    te_last = jnp.sum((n_used - 1 >= t_end).astype(jnp.int32))
    te = jnp.minimum(te_raw, te_last).astype(jnp.int32)
    nv = jnp.where(tid < n_used, jnp.clip(counts[te] - (tid - t_start[te]) * tm, 0, tm), 0).astype(jnp.int32)
    pos = (t_start[ri[:, 0:2]] * tm + ri[:, 2:4]).astype(jnp.int32)
    tok = jnp.repeat(jnp.arange(m, dtype=jnp.int32), 2)
    src = jnp.zeros((nt * tm,), jnp.int32).at[pos.reshape(-1)].set(tok)
    nu = n_used.reshape(1).astype(jnp.int32)
    xs = _dispatch(xn, src, nu * (tm // GATHER_R))
    ys = _experts(xs, te, nv, nu, wg, wu, wd)
    return _combine(pos.reshape(-1), h, rw, modr, row_fn, gfin, ys)


def _relayout_w_in(w):
    pad = jnp.zeros((w.shape[0], P_PAD - 5680), w.dtype)
    return jnp.concatenate([w[:, :4096], w[:, 4112:5648], w[:, 4096:4112], w[:, 5648:5680], pad], axis=1).astype(BF16)


def _head_rows(v, direction):
    return jnp.zeros((1, LANE), F32).at[0, 8 * direction:8 * direction + SSD_HEADS].set(v.astype(F32))


def kernel(x, c, ctx, c_ctx, w_ada, b_ada, g_mix, g_ffn, w_in, w_out, sc_conv_w, cm_w_s, cm_b_s, mb_conv_w,
           mb_conv_b, mb_a_log, mb_dt_bias, mb_d, mb_norm_g, gla_w_gate, gla_b_gate, gla_norm_g, ffn_w_gate,
           ffn_w_up, ffn_w_down, moe_router, moe_w_gate, moe_w_up, moe_w_down, g_final):
    nb, seq, _ = x.shape
    ctx_len = ctx.shape[1]
    depth = w_ada.shape[0]
    m_lat = nb * seq
    h = x.reshape(m_lat, D)
    hc = ctx.reshape(nb * ctx_len, D)

    cond = jnp.concatenate([c, c_ctx[None, :]], axis=0)
    cb = jnp.broadcast_to(cond[:, :, None], (nb + 1, D, LANE))
    mods = _ada(cb, w_ada, b_ada).reshape(depth, 8, N_MOD, 1, D)

    tm_in = 1024
    lat_row_in = lambda i: i // (seq // tm_in)
    lat_row_512 = lambda i: i // (seq // 512)
    lat_row_g = lambda i: i // (seq // GATHER_R)
    ctx_row = lambda i: nb

    head_of_col = np.arange(GW) // SSD_HD
    expand = [jnp.asarray(np.arange(LANE)[:, None] == 8 * d + head_of_col[None, :], dtype=BF16) for d in range(2)]
    out = None
    for i in range(depth):
        last = i == depth - 1
        modr = mods[i]
        w_in_r = _relayout_w_in(w_in[i])
        g_mix_i = g_mix[i].reshape(1, D)
        g_ffn_i = g_ffn[i].reshape(1, D)
        p = _inproj(h, g_mix_i, modr, lat_row_in, w_in_r, tm_in)
        pc = _inproj(hc, g_mix_i, modr, ctx_row, w_in_r, nb * ctx_len)

        ws = cm_w_s[i].astype(BF16)
        bs = jnp.repeat(cm_b_s[i].T, 128, axis=1)
        ya = _scm(p, sc_conv_w[i], ws, bs, GRID_W)
        if not last:
            yac = _scm(pc, sc_conv_w[i], ws, bs, ctx_len)

        conv_b = mb_conv_b[i].reshape(1, -1)
        s_zero = jnp.zeros((nb, SSD_Q, GW), F32)
        skip = jnp.repeat(mb_d[i, 0] + mb_d[i, 1], SSD_HD).reshape(1, GW)
        fin = (skip, mb_norm_g[i].reshape(1, GW))
        ssd_c = [(mb_conv_w[i], conv_b, _head_rows(mb_a_log[i, d], d), _head_rows(mb_dt_bias[i, d], d), expand[d])
                 for d in range(2)]
        y0c, s0c = _ssd(pc, s_zero, ssd_c[0], nb, ctx_len, 0, False)
        y0, _ = _ssd(p, s0c, ssd_c[0], nb, GRID_W, 0, False)
        if last:
            _, s1c = _ssd(pc, s_zero, ssd_c[1], nb, ctx_len, 1, False)
        else:
            ymbc, s1c = _ssd(pc, s_zero, ssd_c[1], nb, ctx_len, 1, True, y0c, fin)
        ymb, _ = _ssd(p, s1c, ssd_c[1], nb, GRID_W, 1, True, y0, fin)

        g_zero = jnp.zeros((nb, GLA_DV, 256), F32)
        gla_c = [(jnp.zeros((LANE, 256), F32).at[16 + 16 * d:32 + 16 * d].set(gla_w_gate[i, d]).astype(BF16),
                  gla_b_gate[i, d].reshape(1, 256)) for d in range(2)]
        ng = gla_norm_g[i].reshape(1, GLA_DV)
        o0c, t0c = _gla(pc, g_zero, gla_c[0], nb, 0, False)
        o0, _ = _gla(p, t0c, gla_c[0], nb, 0, False)
        if last:
            _, t1c = _gla(pc, g_zero, gla_c[1], nb, 1, False)
        else:
            yglac, t1c = _gla(pc, g_zero, gla_c[1], nb, 1, True, o0c, ng)
        ygla = _gla(p, t1c, gla_c[1], nb, 1, True, o0, ng)[0]

        w_out_b = w_out[i].astype(BF16)
        j = i // 2
        if i % 2 == 0:
            wg, wu, wd = ffn_w_gate[j], ffn_w_up[j], ffn_w_down[j]
            hn, xn = _outproj(h, ya, ymb, ygla, w_out_b, modr, lat_row_512, g_ffn_i)
            h = _ffn(xn, hn, modr, lat_row_in, wg, wu, wd, tm_in)
            if not last:
                hcn, xcn = _outproj(hc, yac, ymbc, yglac, w_out_b, modr, ctx_row, g_ffn_i)
                hc = _ffn(xcn, hcn, modr, ctx_row, wg, wu, wd, nb * ctx_len)
            if last:
                out = h
        else:
            w_router = jnp.pad(moe_router[j], ((0, 0), (0, LANE - N_EXPERTS)))
            hn, xn, ri, rw, cnt = _outproj(h, ya, ymb, ygla, w_out_b, modr, lat_row_512, g_ffn_i, w_router)
            if last:
                out = _moe(xn, hn, ri, rw, cnt, modr, lat_row_g, g_final.reshape(1, D),
                           moe_w_gate[j], moe_w_up[j], moe_w_down[j])
            else:
                raise NotImplementedError("routed FFN is only implemented as the last layer's channel mixer")
    return out.reshape(nb, seq, D)
```

```python
import functools

import jax
import jax.numpy as jnp
import numpy as np
from jax import lax
from jax.experimental import pallas as pl
from jax.experimental.pallas import tpu as pltpu

F32 = jnp.float32
BF16 = jnp.bfloat16
EPS = 1e-6

D = 2048
GW = D // 4
GRID_W = 64
N_MOD = 6
LANE = 128
P_PAD = 5760
COL_SMALL = 5632

SSD_Q = 128
SSD_HEADS = 8
SSD_HD = 64
GLA_C = 64
GLA_HEADS = 4
GLA_DK = 64
GLA_DV = 128
GLA_RANK = 16
N_EXPERTS = 8

MOE_TM = 1024
MOE_SB = 256
MOE_TF = 512
MOE_NT = 2 * 8192 // MOE_TM + N_EXPERTS
MOE_ISSUE = 76
MOE_STAGE = 14 * MOE_ISSUE
MOE_VMEM_LIMIT = 63 * 1024 * 1024
GATHER_R = 256


def _silu(x):
    return x * jax.nn.sigmoid(x)


def _softplus(x):
    return jnp.maximum(x, 0.0) + jnp.log1p(jnp.exp(-jnp.abs(x)))


def _split3(x):
    hi = x.astype(BF16)
    r = x - hi.astype(F32)
    mid = r.astype(BF16)
    lo = (r - mid.astype(F32)).astype(BF16)
    return hi, mid, lo


def _dot(a, b):
    return jnp.dot(a, b, preferred_element_type=F32)


def _dot_nt(a, b):
    return lax.dot_general(a, b, (((1,), (1,)), ((), ())), preferred_element_type=F32)


def _dot_tn(a, b):
    return lax.dot_general(a, b, (((0,), (0,)), ((), ())), preferred_element_type=F32)


def _dot01_left(m01, x):
    hi, mid, lo = _split3(x)
    return _dot(m01, hi) + _dot(m01, mid) + _dot(m01, lo)


def _dot01_right(x, m01):
    hi, mid, lo = _split3(x)
    return _dot(hi, m01) + _dot(mid, m01) + _dot(lo, m01)


def _rms(x, g):
    return x * lax.rsqrt(jnp.mean(x * x, axis=-1, keepdims=True) + EPS) * g


def _conv3(x, w, seg):
    n = x.shape[0]
    t = lax.broadcasted_iota(jnp.int32, (n, 1), 0) & (seg - 1)
    prev = jnp.where(t == 0, 0.0, pltpu.roll(x, 1, 0))
    nxt = jnp.where(t == seg - 1, 0.0, pltpu.roll(x, n - 1, 0))
    return prev * w[0:1] + x * w[1:2] + nxt * w[2:3]


def _ada_kernel(cb_ref, w_ref, b_ref, o_ref, s_ref):
    @pl.when((pl.program_id(0) == 0) & (pl.program_id(1) == 0))
    def _():
        s_ref[...] = _silu(cb_ref[...])

    tn = w_ref.shape[1]
    o_ref[...] = jnp.zeros_like(o_ref)
    for j in range(tn // LANE):
        wj = w_ref[:, j * LANE:(j + 1) * LANE]
        for m in range(3):
            o_ref[m:m + 1, j * LANE:(j + 1) * LANE] = (
                jnp.sum(wj * s_ref[m], axis=0, keepdims=True) + b_ref[:, j * LANE:(j + 1) * LANE])


def _ada(cb, w_ada, b_ada):
    depth = w_ada.shape[0]
    n = w_ada.shape[2]
    tn = 1024
    return pl.pallas_call(
        _ada_kernel,
        out_shape=jax.ShapeDtypeStruct((depth, 8, n), F32),
        grid=(depth, n // tn),
        in_specs=[pl.BlockSpec((3, D, LANE), lambda l, j: (0, 0, 0)),
                  pl.BlockSpec((None, D, tn), lambda l, j: (l, 0, j)),
                  pl.BlockSpec((None, 1, tn), lambda l, j: (l, 0, j))],
        out_specs=pl.BlockSpec((None, 8, tn), lambda l, j: (l, 0, j)),
        scratch_shapes=[pltpu.VMEM((3, D, LANE), F32)],
        name="ada",
    )(cb, w_ada, b_ada.reshape(depth, 1, n))


def _inproj_kernel(h_ref, g_ref, sh_ref, sc_ref, w_ref, o_ref, xn_ref):
    @pl.when(pl.program_id(1) == 0)
    def _():
        xn = _rms(h_ref[...], g_ref[...])
        xn_ref[...] = (xn * (1.0 + sc_ref[...]) + sh_ref[...]).astype(BF16)

    o_ref[...] = _dot(xn_ref[...], w_ref[...])


def _mod_spec(row_fn, k):
    return pl.BlockSpec((None, None, 1, D), lambda i, j: (row_fn(i), k, 0, 0))


def _inproj(h, g, modr, row_fn, w, tm):
    m = h.shape[0]
    tn = 1152
    return pl.pallas_call(
        _inproj_kernel,
        out_shape=jax.ShapeDtypeStruct((m, P_PAD), F32),
        grid=(m // tm, P_PAD // tn),
        in_specs=[pl.BlockSpec((tm, D), lambda i, j: (i, 0)),
                  pl.BlockSpec((1, D), lambda i, j: (0, 0)),
                  _mod_spec(row_fn, 0), _mod_spec(row_fn, 1),
                  pl.BlockSpec((D, tn), lambda i, j: (0, j))],
        out_specs=pl.BlockSpec((tm, tn), lambda i, j: (i, j)),
        scratch_shapes=[pltpu.VMEM((tm, D), BF16)],
        name="inproj",
    )(h, g, modr, modr, w)


def _gelu_tanh(x):
    c = 0.7978845608028654
    return x * (0.5 * (1.0 + jnp.tanh(c * (x + 0.044715 * (x * x * x)))))


def _scm_kernel(sc_ref, u_ref, v_ref, cw_ref, ws_ref, bs_ref, o_ref, *, seg):
    t = sc_ref.shape[0]
    bgate = sc_ref[:, 0:GW]
    y_sc = bgate * _conv3(sc_ref[:, GW:2 * GW] * sc_ref[:, 2 * GW:3 * GW], cw_ref[...], seg)
    o_ref[:, 0:GW] = y_sc.astype(BF16)

    u = _gelu_tanh(u_ref[...])
    v = _gelu_tanh(v_ref[...]).astype(BF16)
    for c in range(t // 128):
        rows = slice(c * 128, (c + 1) * 128)
        for g in range(4):
            cols = slice(g * 128, (g + 1) * 128)
            s = _dot(ws_ref[g], v[rows, cols]) + bs_ref[:, cols]
            o_ref[rows, GW + g * 128:GW + (g + 1) * 128] = (u[rows, cols] * s).astype(BF16)


def _scm(p, cw, ws, bs, seg):
    m = p.shape[0]
    t = 512
    return pl.pallas_call(
        functools.partial(_scm_kernel, seg=seg),
        out_shape=jax.ShapeDtypeStruct((m, 2 * GW), BF16),
        grid=(m // t,),
        in_specs=[pl.BlockSpec((t, 3 * GW), lambda i: (i, 0)),
                  pl.BlockSpec((t, GW), lambda i: (i, 3)),
                  pl.BlockSpec((t, GW), lambda i: (i, 4)),
                  pl.BlockSpec((3, GW), lambda i: (0, 0)),
                  pl.BlockSpec((4, 128, 128), lambda i: (0, 0, 0)),
                  pl.BlockSpec((128, GW), lambda i: (0, 0))],
        out_specs=pl.BlockSpec((t, 2 * GW), lambda i: (i, 0)),
        name="scm",
    )(p, p, p, cw, ws, bs)


def _ssd_kernel(*refs, direction, final, seg):
    if final:
        (xbc_ref, sm_ref, cw_ref, cb_ref, alog_ref, dtb_ref, e_ref, s0_ref,
         y0_ref, z_ref, skip_ref, ng_ref, y_ref, sfin_ref, st_ref) = refs
    else:
        (xbc_ref, sm_ref, cw_ref, cb_ref, alog_ref, dtb_ref, e_ref, s0_ref,
         y_ref, sfin_ref, st_ref) = refs
    j = pl.program_id(0)
    nb, tb = xbc_ref.shape[0], xbc_ref.shape[1]
    q = SSD_Q

    @pl.when(j == 0)
    def _():
        st_ref[...] = s0_ref[...]

    a_neg = -jnp.exp(alog_ref[...])
    ii = lax.broadcasted_iota(jnp.int32, (q, q), 0)
    jj = lax.broadcasted_iota(jnp.int32, (q, q), 1)
    mask = (jj <= ii) if direction == 0 else (jj >= ii)
    tri = mask.astype(BF16)
    last = q - 1 if direction == 0 else 0
    lane0 = 8 * direction
    upper_half = lax.broadcasted_iota(jnp.int32, (q, 128), 1) >= SSD_HD
    expand = e_ref[...]

    xcs = [_silu(_conv3(xbc_ref[b], cw_ref[...], seg) + cb_ref[...]) for b in range(nb)]
    states = [st_ref[b] for b in range(nb)]

    def chunk(b, c, st):
        rows = slice(c * q, (c + 1) * q)
        xc = xcs[b]
        dt = _softplus(sm_ref[b, rows, :] + dtb_ref[...])
        acs = _dot01_left(tri, dt * a_neg)
        tot = acs[last:last + 1, :]
        acs_t = acs.T
        dt_t = dt.T
        wx = _dot01_right(jnp.exp(tot - acs) * dt, expand)
        ea = _dot01_right(jnp.exp(acs), expand)
        xs = xc[rows, 0:GW]
        xw = (wx * xs).astype(BF16)
        stb = st.astype(BF16)
        ys = []
        new_st = []
        for g in range(2):
            bg = xc[rows, GW + g * 128:GW + (g + 1) * 128].astype(BF16)
            cg = xc[rows, GW + 256 + g * 128:GW + 256 + (g + 1) * 128].astype(BF16)
            cb = _dot_nt(cg, bg)
            gcols = slice(g * 256, (g + 1) * 256)
            yint = _dot(cg, stb[:, gcols])
            for pr in range(2):
                pidx = 2 * g + pr
                pcols = slice(pidx * 128, (pidx + 1) * 128)
                xp = xs[:, pcols]
                acc = yint[:, pr * 128:(pr + 1) * 128] * ea[:, pcols]
                for hh in range(2):
                    ln = lane0 + 2 * pidx + hh
                    sg = acs[:, ln:ln + 1] - acs_t[ln:ln + 1, :]
                    decay = jnp.where(mask, jnp.exp(jnp.where(mask, sg, 0.0)), 0.0)
                    scores = (cb * decay * dt_t[ln:ln + 1, :]).astype(BF16)
                    xh = jnp.where(upper_half == (hh == 1), xp, 0.0).astype(BF16)
                    acc = acc + _dot(scores, xh)
                ys.append(acc)
            new_st.append(ea[last:last + 1, gcols] * st[:, gcols] + _dot_tn(bg, xw[:, gcols]))
        y = jnp.concatenate(ys, axis=1)
        if final:
            y = y0_ref[b, rows, :] + y + xs * skip_ref[...]
            y = y * _silu(z_ref[b, rows, :])
            y_ref[b, rows, :] = _rms(y, ng_ref[...]).astype(y_ref.dtype)
        else:
            y_ref[b, rows, :] = y
        return jnp.concatenate(new_st, axis=1)

    nchunk = tb // q
    order = range(nchunk) if direction == 0 else range(nchunk - 1, -1, -1)
    for c in order:
        for b in range(nb):
            states[b] = chunk(b, c, states[b])
    for b in range(nb):
        st_ref[b] = states[b]

    @pl.when(j == pl.num_programs(0) - 1)
    def _():
        sfin_ref[...] = st_ref[...]


def _scan_block_map(nblk, direction):
    return (lambda j: j) if direction == 0 else (lambda j: nblk - 1 - j)


def _ssd(p, s0, consts, nb, seg, direction, final, y0=None, fin=None):
    m = p.shape[0]
    seq = m // nb
    tb = 256
    nblk = seq // tb
    blk = _scan_block_map(nblk, direction)
    p3 = p.reshape(nb, seq, P_PAD)
    full = lambda shape: pl.BlockSpec(shape, lambda j: (0,) * len(shape))
    rows = lambda width, col: pl.BlockSpec((nb, tb, width), lambda j: (0, blk(j), col))
    in_specs = [rows(1024, 3), rows(LANE, COL_SMALL // LANE),
                full((3, 1024)), full((1, 1024)), full((1, LANE)), full((1, LANE)), full((LANE, GW)),
                full((nb, SSD_Q, GW))]
    args = [p3, p3, *consts, s0]
    if final:
        in_specs += [rows(GW, 0), rows(GW, 5), full((1, GW)), full((1, GW))]
        args += [y0.reshape(nb, seq, GW), p3, *fin]
    y, s_fin = pl.pallas_call(
        functools.partial(_ssd_kernel, direction=direction, final=final, seg=seg),
        out_shape=(jax.ShapeDtypeStruct((nb, seq, GW), BF16 if final else F32),
                   jax.ShapeDtypeStruct((nb, SSD_Q, GW), F32)),
        grid=(nblk,),
        in_specs=in_specs,
        out_specs=(rows(GW, 0), full((nb, SSD_Q, GW))),
        scratch_shapes=[pltpu.VMEM((nb, SSD_Q, GW), F32)],
        name=f"ssd_d{direction}",
    )(*args)
    return y.reshape(m, GW), s_fin


def _gla_kernel(*refs, direction, final):
    if final:
        (q_ref, k_ref, v_ref, sm_ref, wg_ref, bg_ref, s0_ref, o0_ref, g_ref, ng_ref,
         o_ref, sfin_ref, st_ref) = refs
    else:
        (q_ref, k_ref, v_ref, sm_ref, wg_ref, bg_ref, s0_ref, o_ref, sfin_ref, st_ref) = refs
    j = pl.program_id(0)
    nb, tb = q_ref.shape[0], q_ref.shape[1]
    c_len = GLA_C

    @pl.when(j == 0)
    def _():
        st_ref[...] = s0_ref[...]

    ii = lax.broadcasted_iota(jnp.int32, (tb, tb), 0)
    jj = lax.broadcasted_iota(jnp.int32, (tb, tb), 1)
    same = (ii >> 6) == (jj >> 6)
    btri = (same & ((jj <= ii) if direction == 0 else (jj >= ii))).astype(BF16)
    gcums = []
    for b in range(nb):
        gk = -_softplus(-(_dot(sm_ref[b].astype(BF16), wg_ref[...]) + bg_ref[...])) * (1.0 / 16.0)
        gcums.append(_dot01_left(btri, gk))

    i64 = lax.broadcasted_iota(jnp.int32, (c_len, c_len), 0)
    j64 = lax.broadcasted_iota(jnp.int32, (c_len, c_len), 1)
    causal = (j64 <= i64) if direction == 0 else (j64 >= i64)
    last = c_len - 1 if direction == 0 else 0
    ref = c_len // 2 if direction == 0 else c_len - 1 - c_len // 2
    upper_half = lax.broadcasted_iota(jnp.int32, (c_len, 128), 1) >= GLA_DK

    states = [st_ref[b] for b in range(nb)]

    def chunk(b, c, st):
        rows = slice(c * c_len, (c + 1) * c_len)
        gc = gcums[b][rows, :]
        gl = gc[last:last + 1, :]
        gr = gc[ref:ref + 1, :]
        qc = q_ref[b, rows, :] * (GLA_DK ** -0.5)
        kc = k_ref[b, rows, :]
        vc = v_ref[b, rows, :].astype(BF16)
        kdec = kc * jnp.exp(gl - gc)
        qg = qc * jnp.exp(gc)
        qa = qc * jnp.exp(gc - gr)
        kb = (kc * jnp.exp(gr - gc)).astype(BF16)
        stb = st.astype(BF16)
        heads = []
        new_st = []
        for pr in range(2):
            pcols = slice(pr * 128, (pr + 1) * 128)
            contrib = jnp.zeros((GLA_DV, 128), F32)
            for hh in range(2):
                h = 2 * pr + hh
                mh = upper_half == (hh == 1)
                zero = jnp.zeros((c_len, 128), F32)
                att = _dot_nt(jnp.where(mh, qa[:, pcols], zero).astype(BF16), kb[:, pcols])
                att = jnp.where(causal, att, 0.0).astype(BF16)
                vh = vc[:, h * GLA_DV:(h + 1) * GLA_DV]
                yh = _dot(att, vh) + _dot_nt(jnp.where(mh, qg[:, pcols], zero).astype(BF16), stb[:, pcols])
                heads.append(yh)
                contrib = contrib + _dot_tn(vh, jnp.where(mh, kdec[:, pcols], zero).astype(BF16))
            new_st.append(jnp.exp(gl[:, pcols]) * st[:, pcols] + contrib)
        o = jnp.concatenate(heads, axis=1)
        if final:
            o = o0_ref[b, rows, :] + o
            normed = [_rms(o[:, h * GLA_DV:(h + 1) * GLA_DV], ng_ref[...]) for h in range(GLA_HEADS)]
            o_ref[b, rows, :] = (jnp.concatenate(normed, axis=1) * _silu(g_ref[b, rows, :])).astype(o_ref.dtype)
        else:
            o_ref[b, rows, :] = o
        return jnp.concatenate(new_st, axis=1)

    nchunk = tb // c_len
    order = range(nchunk) if direction == 0 else range(nchunk - 1, -1, -1)
    for c in order:
        for b in range(nb):
            states[b] = chunk(b, c, states[b])
    for b in range(nb):
        st_ref[b] = states[b]

    @pl.when(j == pl.num_programs(0) - 1)
    def _():
        sfin_ref[...] = st_ref[...]


def _gla(p, s0, consts, nb, direction, final, o0=None, ng=None):
    m = p.shape[0]
    seq = m // nb
    tb = 256
    nblk = seq // tb
    blk = _scan_block_map(nblk, direction)
    p3 = p.reshape(nb, seq, P_PAD)
    full = lambda shape: pl.BlockSpec(shape, lambda j: (0,) * len(shape))
    rows = lambda width, col: pl.BlockSpec((nb, tb, width), lambda j: (0, blk(j), col))
    in_specs = [rows(256, 16), rows(256, 17), rows(GW, 9), rows(LANE, COL_SMALL // LANE),
                full((LANE, 256)), full((1, 256)), full((nb, GLA_DV, 256))]
    args = [p3, p3, p3, p3, *consts, s0]
    if final:
        in_specs += [rows(GW, 0), rows(GW, 10), full((1, GLA_DV))]
        args += [o0.reshape(nb, seq, GW), p3, ng]
    o, s_fin = pl.pallas_call(
        functools.partial(_gla_kernel, direction=direction, final=final),
        out_shape=(jax.ShapeDtypeStruct((nb, seq, GW), BF16 if final else F32),
                   jax.ShapeDtypeStruct((nb, GLA_DV, 256), F32)),
        grid=(nblk,),
        in_specs=in_specs,
        out_specs=(rows(GW, 0), full((nb, GLA_DV, 256))),
        scratch_shapes=[pltpu.VMEM((nb, GLA_DV, 256), F32)],
        name=f"gla_d{direction}",
    )(*args)
    return o.reshape(m, GW), s_fin


def _outproj_kernel(*refs, route):
    if route:
        (h_ref, ya_ref, ymb_ref, ygla_ref, w_ref, gate_ref, g_ref, sh_ref, sc_ref, wr_ref,
         hn_ref, xn_ref, ri_ref, rw_ref, cnt_ref, carry_ref, xprev_ref) = refs
    else:
        (h_ref, ya_ref, ymb_ref, ygla_ref, w_ref, gate_ref, g_ref, sh_ref, sc_ref,
         hn_ref, xn_ref) = refs

    def project():
        acc = (_dot(ya_ref[...], w_ref[0:2 * GW, :]) + _dot(ymb_ref[...], w_ref[2 * GW:3 * GW, :])
               + _dot(ygla_ref[...], w_ref[3 * GW:4 * GW, :]))
        hn = h_ref[...] + gate_ref[...] * acc
        hn_ref[...] = hn
        xn = _rms(hn, g_ref[...]) * (1.0 + sc_ref[...]) + sh_ref[...]
        xn_ref[...] = xn.astype(xn_ref.dtype)
        return xn

    if not route:
        project()
        return

    i = pl.program_id(0)
    n_tiles = pl.num_programs(0) - 1

    def route_prev():
        _route_tile(xprev_ref[...], wr_ref, ri_ref, rw_ref, cnt_ref, carry_ref)

    @pl.when(i == 0)
    def _():
        carry_ref[...] = jnp.zeros_like(carry_ref)
        xprev_ref[...] = project()

    @pl.when((i > 0) & (i < n_tiles))
    def _():
        route_prev()
        xprev_ref[...] = project()

    @pl.when(i == n_tiles)
    def _():
        route_prev()


def _route_tile(xn, wr_ref, ri_ref, rw_ref, cnt_ref, carry_ref):
    tm = xn.shape[0]
    xh = xn.astype(BF16)
    xl = (xn - xh.astype(F32)).astype(BF16)
    wr = wr_ref[...]
    wh = wr.astype(BF16)
    wl = (wr - wh.astype(F32)).astype(BF16)
    hh_hl = _dot(xh, jnp.concatenate([wh, wl], axis=1))
    logits = hh_hl[:, 0:LANE] + hh_hl[:, LANE:2 * LANE] + _dot(xl, wh)
    lane = lax.broadcasted_iota(jnp.int32, (tm, LANE), 1)
    lane_f = lane.astype(F32)
    neg = jnp.float32(-3.0e38)
    lm = jnp.where(lane < N_EXPERTS, logits, neg)
    m1 = jnp.max(lm, axis=-1, keepdims=True)
    i1 = jnp.min(jnp.where(lm == m1, lane_f, float(LANE)), axis=-1, keepdims=True)
    lm2 = jnp.where(lane_f == i1, neg, lm)
    m2 = jnp.max(lm2, axis=-1, keepdims=True)
    i2 = jnp.min(jnp.where(lm2 == m2, lane_f, float(LANE)), axis=-1, keepdims=True)
    e = jnp.exp(m2 - m1)
    w1 = 1.0 / (1.0 + e)
    w2 = e / (1.0 + e)
    sel1 = lane_f == i1
    sel2 = lane_f == i2
    onehot = jnp.where(sel1 | sel2, 1.0, 0.0)
    ti = lax.broadcasted_iota(jnp.int32, (tm, tm), 0)
    tj = lax.broadcasted_iota(jnp.int32, (tm, tm), 1)
    incl = _dot((tj <= ti).astype(BF16), onehot.astype(BF16)) + carry_ref[...]
    excl = incl - onehot
    r1 = jnp.sum(jnp.where(sel1, excl, 0.0), axis=-1, keepdims=True)
    r2 = jnp.sum(jnp.where(sel2, excl, 0.0), axis=-1, keepdims=True)
    info = jnp.where(lane == 0, i1, jnp.where(lane == 1, i2, jnp.where(lane == 2, r1, jnp.where(lane == 3, r2, 0.0))))
    ri_ref[...] = info.astype(jnp.int32)
    rw_ref[...] = jnp.where(lane == 0, w1, jnp.where(lane == 1, w2, 0.0))
    carry_ref[...] = incl[tm - 1:tm, :]
    cnt_ref[...] = incl[tm - 1:tm, :]


def _outproj(h, ya, ymb, ygla, w, modr, row_fn, g, w_router=None):
    m = h.shape[0]
    tm = 512
    route = w_router is not None
    n_tiles = m // tm
    cur = (lambda i: jnp.minimum(i, n_tiles - 1)) if route else (lambda i: i)
    prev = lambda i: jnp.maximum(i - 1, 0)
    mod = lambda k: pl.BlockSpec((None, None, 1, D), lambda i: (row_fn(cur(i)), k, 0, 0))
    in_specs = [pl.BlockSpec((tm, D), lambda i: (cur(i), 0)),
                pl.BlockSpec((tm, 2 * GW), lambda i: (cur(i), 0)),
                pl.BlockSpec((tm, GW), lambda i: (cur(i), 0)),
                pl.BlockSpec((tm, GW), lambda i: (cur(i), 0)),
                pl.BlockSpec((D, D), lambda i: (0, 0)),
                mod(2), pl.BlockSpec((1, D), lambda i: (0, 0)), mod(3), mod(4)]
    args = [h, ya, ymb, ygla, w, modr, g, modr, modr]
    out_shape = [jax.ShapeDtypeStruct((m, D), F32), jax.ShapeDtypeStruct((m, D), F32 if route else BF16)]
    out_specs = [pl.BlockSpec((tm, D), lambda i: (cur(i), 0)), pl.BlockSpec((tm, D), lambda i: (cur(i), 0))]
    scratch = []
    if route:
        in_specs.append(pl.BlockSpec((D, LANE), lambda i: (0, 0)))
        args.append(w_router)
        out_shape += [jax.ShapeDtypeStruct((m, LANE), jnp.int32), jax.ShapeDtypeStruct((m, LANE), F32),
                      jax.ShapeDtypeStruct((1, LANE), F32)]
        out_specs += [pl.BlockSpec((tm, LANE), lambda i: (prev(i), 0)), pl.BlockSpec((tm, LANE), lambda i: (prev(i), 0)),
                      pl.BlockSpec((1, LANE), lambda i: (0, 0))]
        scratch = [pltpu.VMEM((1, LANE), F32), pltpu.VMEM((tm, D), F32)]
    return pl.pallas_call(
        functools.partial(_outproj_kernel, route=route),
        out_shape=tuple(out_shape),
        grid=(n_tiles + 1 if route else n_tiles,),
        in_specs=in_specs,
        out_specs=tuple(out_specs),
        scratch_shapes=scratch,
        name="outproj_route" if route else "outproj",
    )(*args)


def _swiglu_rows(x_ref, wg_ref, wu_ref, wd_ref, o_ref, wgb_ref, wub_ref, wdb_ref, nsb, gate, prologue=None):
    def sub_block(sb):
        start = sb * MOE_SB
        return pl.ds(start if isinstance(sb, int) else pl.multiple_of(start, MOE_SB), MOE_SB)

    def up(sb):
        x = x_ref[sub_block(sb), :]
        return (_silu(_dot(x, wgb_ref[...])) * _dot(x, wub_ref[...])).astype(BF16)

    def down(sb, hidden):
        rows = sub_block(sb)
        y = _dot(hidden, wdb_ref[...])
        o_ref[rows, :] += y if gate is None else gate * y

    if prologue is not None:
        prologue()
    wgb_ref[...] = wg_ref[...].astype(BF16)
    wub_ref[...] = wu_ref[...].astype(BF16)
    hidden = up(0)
    wdb_ref[...] = wd_ref[...].astype(BF16)
    if isinstance(nsb, int):
        for sb in range(1, nsb):
            nxt = up(sb)
            down(sb - 1, hidden)
            hidden = nxt
    else:
        def body(sb, hid):
            nxt = up(sb)
            down(sb - 1, hid)
            return nxt

        hidden = lax.fori_loop(1, nsb, body, hidden)
    down(nsb - 1, hidden)


def _ffn_kernel(x_ref, h_hbm, gate_ref, wg_ref, wu_ref, wd_ref, o_ref, wgb_ref, wub_ref, wdb_ref):
    tm = o_ref.shape[0]

    @pl.when(pl.program_id(1) == 0)
    def _():
        pltpu.sync_copy(h_hbm.at[pl.ds(pl.program_id(0) * tm, tm)], o_ref)

    _swiglu_rows(x_ref, wg_ref, wu_ref, wd_ref, o_ref, wgb_ref, wub_ref, wdb_ref, tm // MOE_SB, gate_ref[...])


def _ffn(xn, h, modr, row_fn, wg, wu, wd, tm):
    m = h.shape[0]
    tf = MOE_TF
    nf = wg.shape[1] // tf
    return pl.pallas_call(
        _ffn_kernel,
        out_shape=jax.ShapeDtypeStruct((m, D), F32),
        grid=(m // tm, nf),
        in_specs=[pl.BlockSpec((tm, D), lambda i, f: (i, 0)),
                  pl.BlockSpec(memory_space=pl.ANY),
                  pl.BlockSpec((None, None, 1, D), lambda i, f: (row_fn(i), 5, 0, 0)),
                  pl.BlockSpec((D, tf), lambda i, f: (0, f)),
                  pl.BlockSpec((D, tf), lambda i, f: (0, f)),
                  pl.BlockSpec((tf, D), lambda i, f: (f, 0))],
        out_specs=pl.BlockSpec((tm, D), lambda i, f: (i, 0)),
        scratch_shapes=[pltpu.VMEM((D, tf), BF16), pltpu.VMEM((D, tf), BF16), pltpu.VMEM((tf, D), BF16)],
        name="ffn",
    )(xn, h, modr, wg, wu, wd)


def _row_copy(src_hbm, row, dst, r, sem):
    return pltpu.make_async_copy(src_hbm.at[pl.ds(row, 1)], dst.at[pl.ds(r, 1)], sem)


def _expert_kernel(te_ref, nv_ref, nu_ref, src_ref, xn_hbm, wg_ref, wu_ref, wd_ref, o_ref,
                   xb_ref, stage_ref, wgb_ref, wub_ref, wdb_ref, sem):
    i = pl.program_id(0)
    f = pl.program_id(1)
    n_tiles = pl.num_programs(0)
    nf = pl.num_programs(1)
    used = i < nu_ref[0]

    def wait_stage():
        def wait(r, carry):
            _row_copy(xn_hbm, 0, stage_ref, r, sem).wait()
            return carry

        lax.fori_loop(0, MOE_STAGE, wait, 0, unroll=8)

    @pl.when(f == 0)
    def _():
        @pl.when(i == 0)
        def _():
            def issue(r, carry):
                _row_copy(xn_hbm, src_ref[r], stage_ref, r, sem).start()
                return carry

            lax.fori_loop(0, MOE_STAGE, issue, 0, unroll=8)

        @pl.when(i <= nu_ref[0])
        def _():
            wait_stage()
            xb_ref[...] = stage_ref[0:MOE_TM, :].astype(BF16)

    def issue_next_rows():
        base = (i + 1) * MOE_TM + f * MOE_ISSUE
        for k in range(MOE_ISSUE):
            _row_copy(xn_hbm, src_ref[base + k], stage_ref, f * MOE_ISSUE + k, sem).start()

    @pl.when(used)
    def _():
        @pl.when(f == 0)
        def _():
            o_ref[...] = jnp.zeros_like(o_ref)

        nsb = (nv_ref[i] + MOE_SB - 1) // MOE_SB
        refs = (xb_ref, wg_ref, wu_ref, wd_ref, o_ref, wgb_ref, wub_ref, wdb_ref)

        @pl.when(nsb == MOE_TM // MOE_SB)
        def _():
            _swiglu_rows(*refs, MOE_TM // MOE_SB, None, issue_next_rows)

        @pl.when(nsb < MOE_TM // MOE_SB)
        def _():
            _swiglu_rows(*refs, nsb, None, issue_next_rows)

    @pl.when(jnp.logical_not(used) & (f == 0))
    def _():
        o_ref[...] = jnp.zeros_like(o_ref)

    @pl.when(used & (i == n_tiles - 1) & (f == nf - 1))
    def _():
        wait_stage()


def _experts(xn, src, te, nv, nu, wg, wu, wd):
    dff = wg.shape[2]
    nf = dff // MOE_TF
    assert nf * MOE_ISSUE == MOE_STAGE and MOE_STAGE >= MOE_TM

    def f_eff(i, f, nu):
        return jnp.where(i < nu[0], f, nf - 1)

    return pl.pallas_call(
        _expert_kernel,
        out_shape=jax.ShapeDtypeStruct((MOE_NT * MOE_TM, D), F32),
        grid_spec=pltpu.PrefetchScalarGridSpec(
            num_scalar_prefetch=4,
            grid=(MOE_NT, nf),
            in_specs=[pl.BlockSpec(memory_space=pl.ANY),
                      pl.BlockSpec((None, D, MOE_TF), lambda i, f, te, nv, nu, src: (te[i], 0, f_eff(i, f, nu))),
                      pl.BlockSpec((None, D, MOE_TF), lambda i, f, te, nv, nu, src: (te[i], 0, f_eff(i, f, nu))),
                      pl.BlockSpec((None, MOE_TF, D), lambda i, f, te, nv, nu, src: (te[i], f_eff(i, f, nu), 0))],
            out_specs=pl.BlockSpec((MOE_TM, D), lambda i, f, te, nv, nu, src: (i, 0)),
            scratch_shapes=[pltpu.VMEM((MOE_TM, D), BF16), pltpu.VMEM((MOE_STAGE, D), F32),
                            pltpu.VMEM((D, MOE_TF), BF16), pltpu.VMEM((D, MOE_TF), BF16),
                            pltpu.VMEM((MOE_TF, D), BF16), pltpu.SemaphoreType.DMA(())]),
        compiler_params=pltpu.CompilerParams(vmem_limit_bytes=MOE_VMEM_LIMIT),
        name="moe_experts",
    )(te, nv, nu, src, xn, wg, wu, wd)


def _combine_kernel(pos_ref, h_ref, rw_ref, gate_ref, gfin_ref, y_hbm, o_ref, buf_ref, sem):
    i = pl.program_id(0)
    r_blk = h_ref.shape[0]
    slot = i % 2

    def issue_tile(tile, s):
        def issue(r, carry):
            t = tile * r_blk + r
            _row_copy(y_hbm, pos_ref[2 * t], buf_ref.at[s, 0], r, sem.at[s]).start()
            _row_copy(y_hbm, pos_ref[2 * t + 1], buf_ref.at[s, 1], r, sem.at[s]).start()
            return carry

        lax.fori_loop(0, r_blk, issue, 0, unroll=8)

    @pl.when(i == 0)
    def _():
        issue_tile(0, 0)

    @pl.when(i + 1 < pl.num_programs(0))
    def _():
        issue_tile(i + 1, 1 - slot)

    def wait(r, carry):
        _row_copy(y_hbm, 0, buf_ref.at[slot, 0], r, sem.at[slot]).wait()
        _row_copy(y_hbm, 0, buf_ref.at[slot, 1], r, sem.at[slot]).wait()
        return carry

    lax.fori_loop(0, r_blk, wait, 0, unroll=8)
    y = rw_ref[:, 0:1] * buf_ref[slot, 0] + rw_ref[:, 1:2] * buf_ref[slot, 1]
    hn = h_ref[...] + gate_ref[...] * y
    o_ref[...] = _rms(hn, gfin_ref[...])


def _combine(pos, h, rw, modr, row_fn, gfin, ys):
    m = h.shape[0]
    r = GATHER_R
    return pl.pallas_call(
        _combine_kernel,
        out_shape=jax.ShapeDtypeStruct((m, D), F32),
        grid_spec=pltpu.PrefetchScalarGridSpec(
            num_scalar_prefetch=1,
            grid=(m // r,),
            in_specs=[pl.BlockSpec((r, D), lambda i, pos: (i, 0)),
                      pl.BlockSpec((r, LANE), lambda i, pos: (i, 0)),
                      pl.BlockSpec((None, None, 1, D), lambda i, pos: (row_fn(i), 5, 0, 0)),
                      pl.BlockSpec((1, D), lambda i, pos: (0, 0)),
                      pl.BlockSpec(memory_space=pl.ANY)],
            out_specs=pl.BlockSpec((r, D), lambda i, pos: (i, 0)),
            scratch_shapes=[pltpu.VMEM((2, 2, r, D), F32), pltpu.SemaphoreType.DMA((2,))]),
        name="moe_combine",
    )(pos, h, rw, modr, gfin, ys)


def _moe(xn, h, ri, rw, cnt, modr, row_fn, gfin, wg, wu, wd):
    m = h.shape[0]
    tm = MOE_TM
    nt = MOE_NT
    counts = cnt[0, :N_EXPERTS].astype(jnp.int32)
    nt_e = (counts + tm - 1) // tm
    t_end = jnp.cumsum(nt_e)
    t_start = t_end - nt_e
    n_used = t_end[-1]
    tid = jnp.arange(nt, dtype=jnp.int32)
    te_raw = jnp.sum((tid[:, None] >= t_end[None, :]).astype(jnp.int32), axis=1)
    te_last = jnp.sum((n_used - 1 >= t_end).astype(jnp.int32))
    te = jnp.minimum(te_raw, te_last).astype(jnp.int32)
    nv = jnp.where(tid < n_used, jnp.clip(counts[te] - (tid - t_start[te]) * tm, 0, tm), 0).astype(jnp.int32)
    pos = (t_start[ri[:, 0:2]] * tm + ri[:, 2:4]).astype(jnp.int32)
    tok = jnp.repeat(jnp.arange(m, dtype=jnp.int32), 2)
    src = jnp.zeros(((nt + 2) * tm,), jnp.int32).at[pos.reshape(-1)].set(tok)
    nu = n_used.reshape(1).astype(jnp.int32)
    ys = _experts(xn, src, te, nv, nu, wg, wu, wd)
    return _combine(pos.reshape(-1), h, rw, modr, row_fn, gfin, ys)


def _relayout_w_in(w):
    pad = jnp.zeros((w.shape[0], P_PAD - 5680), w.dtype)
    return jnp.concatenate([w[:, :4096], w[:, 4112:5648], w[:, 4096:4112], w[:, 5648:5680], pad], axis=1).astype(BF16)


def _head_rows(v, direction):
    return jnp.zeros((1, LANE), F32).at[0, 8 * direction:8 * direction + SSD_HEADS].set(v.astype(F32))


def kernel(x, c, ctx, c_ctx, w_ada, b_ada, g_mix, g_ffn, w_in, w_out, sc_conv_w, cm_w_s, cm_b_s, mb_conv_w,
           mb_conv_b, mb_a_log, mb_dt_bias, mb_d, mb_norm_g, gla_w_gate, gla_b_gate, gla_norm_g, ffn_w_gate,
           ffn_w_up, ffn_w_down, moe_router, moe_w_gate, moe_w_up, moe_w_down, g_final):
    nb, seq, _ = x.shape
    ctx_len = ctx.shape[1]
    depth = w_ada.shape[0]
    m_lat = nb * seq
    h = x.reshape(m_lat, D)
    hc = ctx.reshape(nb * ctx_len, D)

    cond = jnp.concatenate([c, c_ctx[None, :]], axis=0)
    cb = jnp.broadcast_to(cond[:, :, None], (nb + 1, D, LANE))
    mods = _ada(cb, w_ada, b_ada).reshape(depth, 8, N_MOD, 1, D)

    tm_in = 1024
    lat_row_in = lambda i: i // (seq // tm_in)
    lat_row_512 = lambda i: i // (seq // 512)
    lat_row_g = lambda i: i // (seq // GATHER_R)
    ctx_row = lambda i: nb

    head_of_col = np.arange(GW) // SSD_HD
    expand = [jnp.asarray(np.arange(LANE)[:, None] == 8 * d + head_of_col[None, :], dtype=BF16) for d in range(2)]
    out = None
    for i in range(depth):
        last = i == depth - 1
        modr = mods[i]
        w_in_r = _relayout_w_in(w_in[i])
        g_mix_i = g_mix[i].reshape(1, D)
        g_ffn_i = g_ffn[i].reshape(1, D)
        p = _inproj(h, g_mix_i, modr, lat_row_in, w_in_r, tm_in)
        pc = _inproj(hc, g_mix_i, modr, ctx_row, w_in_r, nb * ctx_len)

        ws = cm_w_s[i].astype(BF16)
        bs = jnp.repeat(cm_b_s[i].T, 128, axis=1)
        ya = _scm(p, sc_conv_w[i], ws, bs, GRID_W)
        if not last:
            yac = _scm(pc, sc_conv_w[i], ws, bs, ctx_len)

        conv_b = mb_conv_b[i].reshape(1, -1)
        s_zero = jnp.zeros((nb, SSD_Q, GW), F32)
        skip = jnp.repeat(mb_d[i, 0] + mb_d[i, 1], SSD_HD).reshape(1, GW)
        fin = (skip, mb_norm_g[i].reshape(1, GW))
        ssd_c = [(mb_conv_w[i], conv_b, _head_rows(mb_a_log[i, d], d), _head_rows(mb_dt_bias[i, d], d), expand[d])
                 for d in range(2)]
        y0c, s0c = _ssd(pc, s_zero, ssd_c[0], nb, ctx_len, 0, False)
        y0, _ = _ssd(p, s0c, ssd_c[0], nb, GRID_W, 0, False)
        if last:
            _, s1c = _ssd(pc, s_zero, ssd_c[1], nb, ctx_len, 1, False)
        else:
            ymbc, s1c = _ssd(pc, s_zero, ssd_c[1], nb, ctx_len, 1, True, y0c, fin)
        ymb, _ = _ssd(p, s1c, ssd_c[1], nb, GRID_W, 1, True, y0, fin)

        g_zero = jnp.zeros((nb, GLA_DV, 256), F32)
        gla_c = [(jnp.zeros((LANE, 256), F32).at[16 + 16 * d:32 + 16 * d].set(gla_w_gate[i, d]).astype(BF16),
                  gla_b_gate[i, d].reshape(1, 256)) for d in range(2)]
        ng = gla_norm_g[i].reshape(1, GLA_DV)
        o0c, t0c = _gla(pc, g_zero, gla_c[0], nb, 0, False)
        o0, _ = _gla(p, t0c, gla_c[0], nb, 0, False)
        if last:
            _, t1c = _gla(pc, g_zero, gla_c[1], nb, 1, False)
        else:
            yglac, t1c = _gla(pc, g_zero, gla_c[1], nb, 1, True, o0c, ng)
        ygla = _gla(p, t1c, gla_c[1], nb, 1, True, o0, ng)[0]

        w_out_b = w_out[i].astype(BF16)
        j = i // 2
        if i % 2 == 0:
            wg, wu, wd = ffn_w_gate[j], ffn_w_up[j], ffn_w_down[j]
            hn, xn = _outproj(h, ya, ymb, ygla, w_out_b, modr, lat_row_512, g_ffn_i)
            h = _ffn(xn, hn, modr, lat_row_in, wg, wu, wd, tm_in)
            if not last:
                hcn, xcn = _outproj(hc, yac, ymbc, yglac, w_out_b, modr, ctx_row, g_ffn_i)
                hc = _ffn(xcn, hcn, modr, ctx_row, wg, wu, wd, nb * ctx_len)
            if last:
                out = h
        else:
            w_router = jnp.pad(moe_router[j], ((0, 0), (0, LANE - N_EXPERTS)))
            hn, xn, ri, rw, cnt = _outproj(h, ya, ymb, ygla, w_out_b, modr, lat_row_512, g_ffn_i, w_router)
            if last:
                out = _moe(xn, hn, ri, rw, cnt, modr, lat_row_g, g_final.reshape(1, D),
                           moe_w_gate[j], moe_w_up[j], moe_w_down[j])
            else:
                raise NotImplementedError("routed FFN is only implemented as the last layer's channel mixer")
    return out.reshape(nb, seq, D)
```

```python
import functools

import jax
import jax.numpy as jnp
import numpy as np
from jax import lax
from jax.experimental import pallas as pl
from jax.experimental.pallas import tpu as pltpu

F32 = jnp.float32
BF16 = jnp.bfloat16
EPS = 1e-6

D = 2048
GW = D // 4
GRID_W = 64
N_MOD = 6
LANE = 128
BF16_SUBLANES = 16
P_MAIN = 5632

SSD_Q = 128
SSD_HEADS = 8
SSD_HD = 64
GLA_C = 64
GLA_HEADS = 4
GLA_DK = 64
GLA_DV = 128
GLA_RANK = 16
N_EXPERTS = 8

FFN_SB = 256
MOE_SB = 272
MOE_TM = 4 * MOE_SB
MOE_TF = 512
MOE_NT = 2 * 8192 // MOE_TM + N_EXPERTS
MOE_ISSUE = 78
MOE_STAGE = 14 * MOE_ISSUE
MOE_UNROLL = 6
MOE_VMEM_LIMIT = 127 * 512 * 1024
GATHER_R = 256


def _silu(x):
    return x * jax.nn.sigmoid(x)


def _softplus(x):
    return jnp.maximum(x, 0.0) + jnp.log1p(jnp.exp(-jnp.abs(x)))


def _split3(x):
    hi = x.astype(BF16)
    r = x - hi.astype(F32)
    mid = r.astype(BF16)
    lo = (r - mid.astype(F32)).astype(BF16)
    return hi, mid, lo


def _dot(a, b):
    return jnp.dot(a, b, preferred_element_type=F32)


def _dot_nt(a, b):
    return lax.dot_general(a, b, (((1,), (1,)), ((), ())), preferred_element_type=F32)


def _dot_tn(a, b):
    return lax.dot_general(a, b, (((0,), (0,)), ((), ())), preferred_element_type=F32)


def _dot01_left(m01, x):
    hi, mid, lo = _split3(x)
    return _dot(m01, hi) + _dot(m01, mid) + _dot(m01, lo)


def _dot01_right(x, m01):
    hi, mid, lo = _split3(x)
    return _dot(hi, m01) + _dot(mid, m01) + _dot(lo, m01)


def _rms(x, g):
    return x * lax.rsqrt(jnp.mean(x * x, axis=-1, keepdims=True) + EPS) * g


def _conv3(x, w, seg):
    n = x.shape[0]
    t = lax.broadcasted_iota(jnp.int32, (n, 1), 0) & (seg - 1)
    prev = jnp.where(t == 0, 0.0, pltpu.roll(x, 1, 0))
    nxt = jnp.where(t == seg - 1, 0.0, pltpu.roll(x, n - 1, 0))
    return prev * w[0:1] + x * w[1:2] + nxt * w[2:3]


def _ada_kernel(cb_ref, w_ref, b_ref, o_ref, s_ref):
    @pl.when((pl.program_id(0) == 0) & (pl.program_id(1) == 0))
    def _():
        s_ref[...] = _silu(cb_ref[...])

    tn = w_ref.shape[1]
    o_ref[...] = jnp.zeros_like(o_ref)
    for j in range(tn // LANE):
        wj = w_ref[:, j * LANE:(j + 1) * LANE]
        for m in range(3):
            o_ref[m:m + 1, j * LANE:(j + 1) * LANE] = (
                jnp.sum(wj * s_ref[m], axis=0, keepdims=True) + b_ref[:, j * LANE:(j + 1) * LANE])


def _ada(cb, w_ada, b_ada):
    depth = w_ada.shape[0]
    n = w_ada.shape[2]
    tn = 1024
    return pl.pallas_call(
        _ada_kernel,
        out_shape=jax.ShapeDtypeStruct((depth, 8, n), F32),
        grid=(depth, n // tn),
        in_specs=[pl.BlockSpec((3, D, LANE), lambda l, j: (0, 0, 0)),
                  pl.BlockSpec((None, D, tn), lambda l, j: (l, 0, j)),
                  pl.BlockSpec((None, 1, tn), lambda l, j: (l, 0, j))],
        out_specs=pl.BlockSpec((None, 8, tn), lambda l, j: (l, 0, j)),
        scratch_shapes=[pltpu.VMEM((3, D, LANE), F32)],
        name="ada",
    )(cb, w_ada, b_ada.reshape(depth, 1, n))


def _inproj_kernel(h_ref, g_ref, sh_ref, sc_ref, w_ref, ws_ref, o_ref, os_ref, xn_ref):
    @pl.when(pl.program_id(1) == 0)
    def _():
        xn = _rms(h_ref[...], g_ref[...])
        xn_ref[...] = (xn * (1.0 + sc_ref[...]) + sh_ref[...]).astype(BF16)
        os_ref[...] = _dot(xn_ref[...], ws_ref[...])

    o_ref[...] = _dot(xn_ref[...], w_ref[...]).astype(BF16)


def _mod_spec(row_fn, k):
    return pl.BlockSpec((None, None, 1, D), lambda i, j: (row_fn(i), k, 0, 0))


def _inproj(h, g, modr, row_fn, w, w_small, tm):
    m = h.shape[0]
    tn = P_MAIN // 4
    return pl.pallas_call(
        _inproj_kernel,
        out_shape=(jax.ShapeDtypeStruct((m, P_MAIN), BF16), jax.ShapeDtypeStruct((m, LANE), F32)),
        grid=(m // tm, P_MAIN // tn),
        in_specs=[pl.BlockSpec((tm, D), lambda i, j: (i, 0)),
                  pl.BlockSpec((1, D), lambda i, j: (0, 0)),
                  _mod_spec(row_fn, 0), _mod_spec(row_fn, 1),
                  pl.BlockSpec((D, tn), lambda i, j: (0, j)),
                  pl.BlockSpec((D, LANE), lambda i, j: (0, 0))],
        out_specs=(pl.BlockSpec((tm, tn), lambda i, j: (i, j)),
                   pl.BlockSpec((tm, LANE), lambda i, j: (i, 0))),
        scratch_shapes=[pltpu.VMEM((tm, D), BF16)],
        name="inproj",
    )(h, g, modr, modr, w, w_small)


def _gelu_tanh(x):
    c = 0.7978845608028654
    return x * (0.5 * (1.0 + jnp.tanh(c * (x + 0.044715 * (x * x * x)))))


def _scm_kernel(sc_ref, u_ref, v_ref, cw_ref, ws_ref, bs_ref, o_ref, *, seg):
    t = sc_ref.shape[0]
    bgate = sc_ref[:, 0:GW].astype(F32)
    gated = sc_ref[:, GW:2 * GW].astype(F32) * sc_ref[:, 2 * GW:3 * GW].astype(F32)
    y_sc = bgate * _conv3(gated, cw_ref[...], seg)
    o_ref[:, 0:GW] = y_sc.astype(BF16)

    u = _gelu_tanh(u_ref[...].astype(F32))
    v = _gelu_tanh(v_ref[...].astype(F32)).astype(BF16)
    for c in range(t // 128):
        rows = slice(c * 128, (c + 1) * 128)
        for g in range(4):
            cols = slice(g * 128, (g + 1) * 128)
            s = _dot(ws_ref[g], v[rows, cols]) + bs_ref[:, cols]
            o_ref[rows, GW + g * 128:GW + (g + 1) * 128] = (u[rows, cols] * s).astype(BF16)


def _scm(p, cw, ws, bs, seg):
    m = p.shape[0]
    t = 512
    return pl.pallas_call(
        functools.partial(_scm_kernel, seg=seg),
        out_shape=jax.ShapeDtypeStruct((m, 2 * GW), BF16),
        grid=(m // t,),
        in_specs=[pl.BlockSpec((t, 3 * GW), lambda i: (i, 0)),
                  pl.BlockSpec((t, GW), lambda i: (i, 3)),
                  pl.BlockSpec((t, GW), lambda i: (i, 4)),
                  pl.BlockSpec((3, GW), lambda i: (0, 0)),
                  pl.BlockSpec((4, 128, 128), lambda i: (0, 0, 0)),
                  pl.BlockSpec((128, GW), lambda i: (0, 0))],
        out_specs=pl.BlockSpec((t, 2 * GW), lambda i: (i, 0)),
        name="scm",
    )(p, p, p, cw, ws, bs)


def _ssd_kernel(*refs, direction, final, seg):
    if final:
        (xbc_ref, sm_ref, cw_ref, cb_ref, alog_ref, dtb_ref, e_ref, s0_ref,
         y0_ref, z_ref, skip_ref, ng_ref, y_ref, sfin_ref, st_ref) = refs
    else:
        (xbc_ref, sm_ref, cw_ref, cb_ref, alog_ref, dtb_ref, e_ref, s0_ref,
         y_ref, sfin_ref, st_ref) = refs
    j = pl.program_id(0)
    nb, tb = xbc_ref.shape[0], xbc_ref.shape[1]
    q = SSD_Q

    @pl.when(j == 0)
    def _():
        st_ref[...] = s0_ref[...]

    a_neg = -jnp.exp(alog_ref[...])
    ii = lax.broadcasted_iota(jnp.int32, (q, q), 0)
    jj = lax.broadcasted_iota(jnp.int32, (q, q), 1)
    mask = (jj <= ii) if direction == 0 else (jj >= ii)
    tri = mask.astype(BF16)
    last = q - 1 if direction == 0 else 0
    lane0 = 8 * direction
    upper_half = lax.broadcasted_iota(jnp.int32, (q, 128), 1) >= SSD_HD
    expand = e_ref[...]

    xcs = [_silu(_conv3(xbc_ref[b].astype(F32), cw_ref[...], seg) + cb_ref[...])
           for b in range(nb)]
    states = [st_ref[b] for b in range(nb)]

    def chunk(b, c, st):
        rows = slice(c * q, (c + 1) * q)
        xc = xcs[b]
        dt = _softplus(sm_ref[b, rows, :] + dtb_ref[...])
        acs = _dot01_left(tri, dt * a_neg)
        tot = acs[last:last + 1, :]
        acs_t = acs.T
        dt_t = dt.T
        wx = _dot01_right(jnp.exp(tot - acs) * dt, expand)
        ea = _dot01_right(jnp.exp(acs), expand)
        xs = xc[rows, 0:GW]
        xw = (wx * xs).astype(BF16)
        stb = st.astype(BF16)
        ys = []
        new_st = []
        for g in range(2):
            bg = xc[rows, GW + g * 128:GW + (g + 1) * 128].astype(BF16)
            cg = xc[rows, GW + 256 + g * 128:GW + 256 + (g + 1) * 128].astype(BF16)
            cb = _dot_nt(cg, bg)
            gcols = slice(g * 256, (g + 1) * 256)
            yint = _dot(cg, stb[:, gcols])
            for pr in range(2):
                pidx = 2 * g + pr
                pcols = slice(pidx * 128, (pidx + 1) * 128)
                xp = xs[:, pcols]
                acc = yint[:, pr * 128:(pr + 1) * 128] * ea[:, pcols]
                for hh in range(2):
                    ln = lane0 + 2 * pidx + hh
                    sg = acs[:, ln:ln + 1] - acs_t[ln:ln + 1, :]
                    decay = jnp.where(mask, jnp.exp(jnp.where(mask, sg, 0.0)), 0.0)
                    scores = (cb * decay * dt_t[ln:ln + 1, :]).astype(BF16)
                    xh = jnp.where(upper_half == (hh == 1), xp, 0.0).astype(BF16)
                    acc = acc + _dot(scores, xh)
                ys.append(acc)
            new_st.append(ea[last:last + 1, gcols] * st[:, gcols] + _dot_tn(bg, xw[:, gcols]))
        y = jnp.concatenate(ys, axis=1)
        if final:
            y = y0_ref[b, rows, :] + y + xs * skip_ref[...]
            y = y * _silu(z_ref[b, rows, :].astype(F32))
            y_ref[b, rows, :] = _rms(y, ng_ref[...]).astype(y_ref.dtype)
        else:
            y_ref[b, rows, :] = y
        return jnp.concatenate(new_st, axis=1)

    nchunk = tb // q
    order = range(nchunk) if direction == 0 else range(nchunk - 1, -1, -1)
    for c in order:
        for b in range(nb):
            states[b] = chunk(b, c, states[b])
    for b in range(nb):
        st_ref[b] = states[b]

    @pl.when(j == pl.num_programs(0) - 1)
    def _():
        sfin_ref[...] = st_ref[...]


def _scan_block_map(nblk, direction):
    return (lambda j: j) if direction == 0 else (lambda j: nblk - 1 - j)


def _ssd(p, ps, s0, consts, nb, seg, direction, final, y0=None, fin=None):
    m = p.shape[0]
    seq = m // nb
    tb = 256
    nblk = seq // tb
    blk = _scan_block_map(nblk, direction)
    p3 = p.reshape(nb, seq, P_MAIN)
    full = lambda shape: pl.BlockSpec(shape, lambda j: (0,) * len(shape))
    rows = lambda width, col: pl.BlockSpec((nb, tb, width), lambda j: (0, blk(j), col))
    in_specs = [rows(1024, 3), rows(LANE, 0),
                full((3, 1024)), full((1, 1024)), full((1, LANE)), full((1, LANE)), full((LANE, GW)),
                full((nb, SSD_Q, GW))]
    args = [p3, ps.reshape(nb, seq, LANE), *consts, s0]
    if final:
        in_specs += [rows(GW, 0), rows(GW, 5), full((1, GW)), full((1, GW))]
        args += [y0.reshape(nb, seq, GW), p3, *fin]
    y, s_fin = pl.pallas_call(
        functools.partial(_ssd_kernel, direction=direction, final=final, seg=seg),
        out_shape=(jax.ShapeDtypeStruct((nb, seq, GW), BF16 if final else F32),
                   jax.ShapeDtypeStruct((nb, SSD_Q, GW), F32)),
        grid=(nblk,),
        in_specs=in_specs,
        out_specs=(rows(GW, 0), full((nb, SSD_Q, GW))),
        scratch_shapes=[pltpu.VMEM((nb, SSD_Q, GW), F32)],
        name=f"ssd_d{direction}",
    )(*args)
    return y.reshape(m, GW), s_fin


def _gla_kernel(*refs, direction, final):
    if final:
        (q_ref, k_ref, v_ref, sm_ref, wg_ref, bg_ref, s0_ref, o0_ref, g_ref, ng_ref,
         o_ref, sfin_ref, st_ref) = refs
    else:
        (q_ref, k_ref, v_ref, sm_ref, wg_ref, bg_ref, s0_ref, o_ref, sfin_ref, st_ref) = refs
    j = pl.program_id(0)
    nb, tb = q_ref.shape[0], q_ref.shape[1]
    c_len = GLA_C

    @pl.when(j == 0)
    def _():
        st_ref[...] = s0_ref[...]

    ii = lax.broadcasted_iota(jnp.int32, (tb, tb), 0)
    jj = lax.broadcasted_iota(jnp.int32, (tb, tb), 1)
    same = (ii >> 6) == (jj >> 6)
    btri = (same & ((jj <= ii) if direction == 0 else (jj >= ii))).astype(BF16)
    gcums = []
    for b in range(nb):
        gk = -_softplus(-(_dot(sm_ref[b].astype(BF16), wg_ref[...]) + bg_ref[...])) * (1.0 / 16.0)
        gcums.append(_dot01_left(btri, gk))

    i64 = lax.broadcasted_iota(jnp.int32, (c_len, c_len), 0)
    j64 = lax.broadcasted_iota(jnp.int32, (c_len, c_len), 1)
    causal = (j64 <= i64) if direction == 0 else (j64 >= i64)
    last = c_len - 1 if direction == 0 else 0
    ref = c_len // 2 if direction == 0 else c_len - 1 - c_len // 2
    upper_half = lax.broadcasted_iota(jnp.int32, (c_len, 128), 1) >= GLA_DK

    states = [st_ref[b] for b in range(nb)]

    def chunk(b, c, st):
        rows = slice(c * c_len, (c + 1) * c_len)
        gc = gcums[b][rows, :]
        gl = gc[last:last + 1, :]
        gr = gc[ref:ref + 1, :]
        qc = q_ref[b, rows, :].astype(F32) * (GLA_DK ** -0.5)
        kc = k_ref[b, rows, :].astype(F32)
        vc = v_ref[b, rows, :]
        kdec = kc * jnp.exp(gl - gc)
        qg = qc * jnp.exp(gc)
        qa = qc * jnp.exp(gc - gr)
        kb = (kc * jnp.exp(gr - gc)).astype(BF16)
        stb = st.astype(BF16)
        heads = []
        new_st = []
        for pr in range(2):
            pcols = slice(pr * 128, (pr + 1) * 128)
            contrib = jnp.zeros((GLA_DV, 128), F32)
            for hh in range(2):
                h = 2 * pr + hh
                mh = upper_half == (hh == 1)
                zero = jnp.zeros((c_len, 128), F32)
                att = _dot_nt(jnp.where(mh, qa[:, pcols], zero).astype(BF16), kb[:, pcols])
                att = jnp.where(causal, att, 0.0).astype(BF16)
                vh = vc[:, h * GLA_DV:(h + 1) * GLA_DV]
                yh = _dot(att, vh) + _dot_nt(jnp.where(mh, qg[:, pcols], zero).astype(BF16), stb[:, pcols])
                heads.append(yh)
                contrib = contrib + _dot_tn(vh, jnp.where(mh, kdec[:, pcols], zero).astype(BF16))
            new_st.append(jnp.exp(gl[:, pcols]) * st[:, pcols] + contrib)
        o = jnp.concatenate(heads, axis=1)
        if final:
            o = o0_ref[b, rows, :] + o
            normed = [_rms(o[:, h * GLA_DV:(h + 1) * GLA_DV], ng_ref[...]) for h in range(GLA_HEADS)]
            gate = _silu(g_ref[b, rows, :].astype(F32))
            o_ref[b, rows, :] = (jnp.concatenate(normed, axis=1) * gate).astype(o_ref.dtype)
        else:
            o_ref[b, rows, :] = o
        return jnp.concatenate(new_st, axis=1)

    nchunk = tb // c_len
    order = range(nchunk) if direction == 0 else range(nchunk - 1, -1, -1)
    for c in order:
        for b in range(nb):
            states[b] = chunk(b, c, states[b])
    for b in range(nb):
        st_ref[b] = states[b]

    @pl.when(j == pl.num_programs(0) - 1)
    def _():
        sfin_ref[...] = st_ref[...]


def _gla(p, ps, s0, consts, nb, direction, final, o0=None, ng=None):
    m = p.shape[0]
    seq = m // nb
    tb = 256
    nblk = seq // tb
    blk = _scan_block_map(nblk, direction)
    p3 = p.reshape(nb, seq, P_MAIN)
    full = lambda shape: pl.BlockSpec(shape, lambda j: (0,) * len(shape))
    rows = lambda width, col: pl.BlockSpec((nb, tb, width), lambda j: (0, blk(j), col))
    in_specs = [rows(256, 16), rows(256, 17), rows(GW, 9), rows(LANE, 0),
                full((LANE, 256)), full((1, 256)), full((nb, GLA_DV, 256))]
    args = [p3, p3, p3, ps.reshape(nb, seq, LANE), *consts, s0]
    if final:
        in_specs += [rows(GW, 0), rows(GW, 10), full((1, GLA_DV))]
        args += [o0.reshape(nb, seq, GW), p3, ng]
    o, s_fin = pl.pallas_call(
        functools.partial(_gla_kernel, direction=direction, final=final),
        out_shape=(jax.ShapeDtypeStruct((nb, seq, GW), BF16 if final else F32),
                   jax.ShapeDtypeStruct((nb, GLA_DV, 256), F32)),
        grid=(nblk,),
        in_specs=in_specs,
        out_specs=(rows(GW, 0), full((nb, GLA_DV, 256))),
        scratch_shapes=[pltpu.VMEM((nb, GLA_DV, 256), F32)],
        name=f"gla_d{direction}",
    )(*args)
    return o.reshape(m, GW), s_fin


def _outproj_kernel(*refs, route):
    if route:
        (h_ref, ya_ref, ymb_ref, ygla_ref, w_ref, gate_ref, g_ref, sh_ref, sc_ref, wr_ref,
         hn_ref, xn_ref, ri_ref, rw_ref, cnt_ref, carry_ref, xprev_ref) = refs
    else:
        (h_ref, ya_ref, ymb_ref, ygla_ref, w_ref, gate_ref, g_ref, sh_ref, sc_ref,
         hn_ref, xn_ref) = refs

    def project():
        acc = (_dot(ya_ref[...], w_ref[0:2 * GW, :]) + _dot(ymb_ref[...], w_ref[2 * GW:3 * GW, :])
               + _dot(ygla_ref[...], w_ref[3 * GW:4 * GW, :]))
        hn = h_ref[...] + gate_ref[...] * acc
        hn_ref[...] = hn
        xn = _rms(hn, g_ref[...]) * (1.0 + sc_ref[...]) + sh_ref[...]
        xn_ref[...] = xn.astype(xn_ref.dtype)
        return xn

    if not route:
        project()
        return

    i = pl.program_id(0)
    n_tiles = pl.num_programs(0) - 1

    def route_prev():
        _route_tile(xprev_ref[...], wr_ref, ri_ref, rw_ref, cnt_ref, carry_ref)

    @pl.when(i == 0)
    def _():
        carry_ref[...] = jnp.zeros_like(carry_ref)
        xprev_ref[...] = project()

    @pl.when((i > 0) & (i < n_tiles))
    def _():
        route_prev()
        xprev_ref[...] = project()

    @pl.when(i == n_tiles)
    def _():
        route_prev()


def _route_tile(xn, wr_ref, ri_ref, rw_ref, cnt_ref, carry_ref):
    tm = xn.shape[0]
    xh = xn.astype(BF16)
    xl = (xn - xh.astype(F32)).astype(BF16)
    wr = wr_ref[...]
    wh = wr.astype(BF16)
    wl = (wr - wh.astype(F32)).astype(BF16)
    hh_hl = _dot(xh, jnp.concatenate([wh, wl], axis=1))
    logits = hh_hl[:, 0:LANE] + hh_hl[:, LANE:2 * LANE] + _dot(xl, wh)
    lane = lax.broadcasted_iota(jnp.int32, (tm, LANE), 1)
    lane_f = lane.astype(F32)
    neg = jnp.float32(-3.0e38)
    lm = jnp.where(lane < N_EXPERTS, logits, neg)
    m1 = jnp.max(lm, axis=-1, keepdims=True)
    i1 = jnp.min(jnp.where(lm == m1, lane_f, float(LANE)), axis=-1, keepdims=True)
    lm2 = jnp.where(lane_f == i1, neg, lm)
    m2 = jnp.max(lm2, axis=-1, keepdims=True)
    i2 = jnp.min(jnp.where(lm2 == m2, lane_f, float(LANE)), axis=-1, keepdims=True)
    e = jnp.exp(m2 - m1)
    w1 = 1.0 / (1.0 + e)
    w2 = e / (1.0 + e)
    sel1 = lane_f == i1
    sel2 = lane_f == i2
    onehot = jnp.where(sel1 | sel2, 1.0, 0.0)
    ti = lax.broadcasted_iota(jnp.int32, (tm, tm), 0)
    tj = lax.broadcasted_iota(jnp.int32, (tm, tm), 1)
    incl = _dot((tj <= ti).astype(BF16), onehot.astype(BF16)) + carry_ref[...]
    excl = incl - onehot
    r1 = jnp.sum(jnp.where(sel1, excl, 0.0), axis=-1, keepdims=True)
    r2 = jnp.sum(jnp.where(sel2, excl, 0.0), axis=-1, keepdims=True)
    info = jnp.where(lane == 0, i1, jnp.where(lane == 1, i2, jnp.where(lane == 2, r1, jnp.where(lane == 3, r2, 0.0))))
    ri_ref[...] = info.astype(jnp.int32)
    rw_ref[...] = jnp.where(lane == 0, w1, jnp.where(lane == 1, w2, 0.0))
    carry_ref[...] = incl[tm - 1:tm, :]
    cnt_ref[...] = incl[tm - 1:tm, :]


def _outproj(h, ya, ymb, ygla, w, modr, row_fn, g, w_router=None):
    m = h.shape[0]
    tm = 512
    route = w_router is not None
    n_tiles = m // tm
    cur = (lambda i: jnp.minimum(i, n_tiles - 1)) if route else (lambda i: i)
    prev = lambda i: jnp.maximum(i - 1, 0)
    mod = lambda k: pl.BlockSpec((None, None, 1, D), lambda i: (row_fn(cur(i)), k, 0, 0))
    in_specs = [pl.BlockSpec((tm, D), lambda i: (cur(i), 0)),
                pl.BlockSpec((tm, 2 * GW), lambda i: (cur(i), 0)),
                pl.BlockSpec((tm, GW), lambda i: (cur(i), 0)),
                pl.BlockSpec((tm, GW), lambda i: (cur(i), 0)),
                pl.BlockSpec((D, D), lambda i: (0, 0)),
                mod(2), pl.BlockSpec((1, D), lambda i: (0, 0)), mod(3), mod(4)]
    args = [h, ya, ymb, ygla, w, modr, g, modr, modr]
    out_shape = [jax.ShapeDtypeStruct((m, D), F32), jax.ShapeDtypeStruct((m, D), F32 if route else BF16)]
    out_specs = [pl.BlockSpec((tm, D), lambda i: (cur(i), 0)), pl.BlockSpec((tm, D), lambda i: (cur(i), 0))]
    scratch = []
    if route:
        in_specs.append(pl.BlockSpec((D, LANE), lambda i: (0, 0)))
        args.append(w_router)
        out_shape += [jax.ShapeDtypeStruct((m, LANE), jnp.int32), jax.ShapeDtypeStruct((m, LANE), F32),
                      jax.ShapeDtypeStruct((1, LANE), F32)]
        out_specs += [pl.BlockSpec((tm, LANE), lambda i: (prev(i), 0)), pl.BlockSpec((tm, LANE), lambda i: (prev(i), 0)),
                      pl.BlockSpec((1, LANE), lambda i: (0, 0))]
        scratch = [pltpu.VMEM((1, LANE), F32), pltpu.VMEM((tm, D), F32)]
    return pl.pallas_call(
        functools.partial(_outproj_kernel, route=route),
        out_shape=tuple(out_shape),
        grid=(n_tiles + 1 if route else n_tiles,),
        in_specs=in_specs,
        out_specs=tuple(out_specs),
        scratch_shapes=scratch,
        name="outproj_route" if route else "outproj",
    )(*args)


def _swiglu_rows(x_ref, wg_ref, wu_ref, wd_ref, o_ref, wgb_ref, wub_ref, wdb_ref, sb_rows, nsb, gate,
                 prologue=None):
    def sub_block(sb):
        start = sb * sb_rows
        return pl.ds(start if isinstance(sb, int) else pl.multiple_of(start, BF16_SUBLANES), sb_rows)

    def up(sb):
        x = x_ref[sub_block(sb), :]
        return (_silu(_dot(x, wgb_ref[...])) * _dot(x, wub_ref[...])).astype(BF16)

    def down(sb, hidden):
        rows = sub_block(sb)
        y = _dot(hidden, wdb_ref[...])
        o_ref[rows, :] += y if gate is None else gate * y

    if prologue is not None:
        prologue()
    if wg_ref is not None:
        wgb_ref[...] = wg_ref[...].astype(BF16)
        wub_ref[...] = wu_ref[...].astype(BF16)
    hidden = up(0)
    if wd_ref is not None:
        wdb_ref[...] = wd_ref[...].astype(BF16)
    if isinstance(nsb, int):
        for sb in range(1, nsb):
            nxt = up(sb)
            down(sb - 1, hidden)
            hidden = nxt
    else:
        def body(sb, hid):
            nxt = up(sb)
            down(sb - 1, hid)
            return nxt

        hidden = lax.fori_loop(1, nsb, body, hidden)
    down(nsb - 1, hidden)


def _ffn_kernel(*refs, emit):
    if emit:
        x_ref, h_hbm, gate_ref, wg_ref, wu_ref, wd_ref, o_ref, wgb_ref, wub_ref, wdb_ref = refs
    else:
        x_ref, h_hbm, gate_ref, wgb_ref, wub_ref, wdb_ref, o_ref = refs
        wg_ref = wu_ref = wd_ref = None
    tm = o_ref.shape[0]

    @pl.when(pl.program_id(1) == 0)
    def _():
        pltpu.sync_copy(h_hbm.at[pl.ds(pl.program_id(0) * tm, tm)], o_ref)

    _swiglu_rows(x_ref, wg_ref, wu_ref, wd_ref, o_ref, wgb_ref, wub_ref, wdb_ref, FFN_SB, tm // FFN_SB,
                 gate_ref[...])


def _ffn(xn, h, modr, row_fn, wg, wu, wd, tm, emit):
    m = h.shape[0]
    tf = MOE_TF
    dff = wg.shape[1]
    assert not emit or m == tm
    w_specs = [pl.BlockSpec((D, tf), lambda i, f: (0, f)),
               pl.BlockSpec((D, tf), lambda i, f: (0, f)),
               pl.BlockSpec((tf, D), lambda i, f: (f, 0))]
    out_shape = [jax.ShapeDtypeStruct((m, D), F32)]
    out_specs = [pl.BlockSpec((tm, D), lambda i, f: (i, 0))]
    if emit:
        out_shape += [jax.ShapeDtypeStruct((D, dff), BF16), jax.ShapeDtypeStruct((D, dff), BF16),
                      jax.ShapeDtypeStruct((dff, D), BF16)]
        out_specs += w_specs
    return pl.pallas_call(
        functools.partial(_ffn_kernel, emit=emit),
        out_shape=tuple(out_shape),
        grid=(m // tm, dff // tf),
        in_specs=[pl.BlockSpec((tm, D), lambda i, f: (i, 0)),
                  pl.BlockSpec(memory_space=pl.ANY),
                  pl.BlockSpec((None, None, 1, D), lambda i, f: (row_fn(i), 5, 0, 0))] + w_specs,
        out_specs=tuple(out_specs),
        name="ffn_emit" if emit else "ffn",
    )(xn, h, modr, wg, wu, wd)


def _row_copy(src_hbm, row, dst, r, sem):
    return pltpu.make_async_copy(src_hbm.at[pl.ds(row, 1)], dst.at[pl.ds(r, 1)], sem)


def _expert_kernel(te_ref, nv_ref, nu_ref, src_ref, xn_hbm, wg_ref, wu_ref, wd_ref, o_ref,
                   xb_ref, stage_ref, wgb_ref, wub_ref, wdb_ref, sem):
    i = pl.program_id(0)
    f = pl.program_id(1)
    n_tiles = pl.num_programs(0)
    nf = pl.num_programs(1)
    used = i < nu_ref[0]

    def wait_stage():
        def wait(r, carry):
            _row_copy(xn_hbm, 0, stage_ref, r, sem).wait()
            return carry

        lax.fori_loop(0, MOE_STAGE, wait, 0, unroll=MOE_UNROLL)

    @pl.when(f == 0)
    def _():
        @pl.when(i == 0)
        def _():
            def issue(r, carry):
                _row_copy(xn_hbm, src_ref[r], stage_ref, r, sem).start()
                return carry

            lax.fori_loop(0, MOE_STAGE, issue, 0, unroll=MOE_UNROLL)

        @pl.when(i <= nu_ref[0])
        def _():
            wait_stage()
            xb_ref[...] = stage_ref[0:MOE_TM, :].astype(BF16)

    def issue_next_rows():
        base = (i + 1) * MOE_TM + f * MOE_ISSUE
        for k in range(MOE_ISSUE):
            _row_copy(xn_hbm, src_ref[base + k], stage_ref, f * MOE_ISSUE + k, sem).start()

    @pl.when(used)
    def _():
        @pl.when(f == 0)
        def _():
            o_ref[...] = jnp.zeros_like(o_ref)

        nsb = (nv_ref[i] + MOE_SB - 1) // MOE_SB
        refs = (xb_ref, wg_ref, wu_ref, wd_ref, o_ref, wgb_ref, wub_ref, wdb_ref)

        @pl.when(nsb == MOE_TM // MOE_SB)
        def _():
            _swiglu_rows(*refs, MOE_SB, MOE_TM // MOE_SB, None, issue_next_rows)

        @pl.when(nsb < MOE_TM // MOE_SB)
        def _():
            _swiglu_rows(*refs, MOE_SB, nsb, None, issue_next_rows)

    @pl.when(jnp.logical_not(used) & (f == 0))
    def _():
        o_ref[...] = jnp.zeros_like(o_ref)

    @pl.when(used & (i == n_tiles - 1) & (f == nf - 1))
    def _():
        wait_stage()


def _experts(xn, src, te, nv, nu, wg, wu, wd):
    dff = wg.shape[2]
    nf = dff // MOE_TF
    assert nf * MOE_ISSUE == MOE_STAGE and MOE_STAGE >= MOE_TM

    def f_eff(i, f, nu):
        return jnp.where(i < nu[0], f, nf - 1)

    return pl.pallas_call(
        _expert_kernel,
        out_shape=jax.ShapeDtypeStruct((MOE_NT * MOE_TM, D), F32),
        grid_spec=pltpu.PrefetchScalarGridSpec(
            num_scalar_prefetch=4,
            grid=(MOE_NT, nf),
            in_specs=[pl.BlockSpec(memory_space=pl.ANY),
                      pl.BlockSpec((None, D, MOE_TF), lambda i, f, te, nv, nu, src: (te[i], 0, f_eff(i, f, nu))),
                      pl.BlockSpec((None, D, MOE_TF), lambda i, f, te, nv, nu, src: (te[i], 0, f_eff(i, f, nu))),
                      pl.BlockSpec((None, MOE_TF, D), lambda i, f, te, nv, nu, src: (te[i], f_eff(i, f, nu), 0))],
            out_specs=pl.BlockSpec((MOE_TM, D), lambda i, f, te, nv, nu, src: (i, 0)),
            scratch_shapes=[pltpu.VMEM((MOE_TM, D), BF16), pltpu.VMEM((MOE_STAGE, D), F32),
                            pltpu.VMEM((D, MOE_TF), BF16), pltpu.VMEM((D, MOE_TF), BF16),
                            pltpu.VMEM((MOE_TF, D), BF16), pltpu.SemaphoreType.DMA(())]),
        compiler_params=pltpu.CompilerParams(vmem_limit_bytes=MOE_VMEM_LIMIT),
        name="moe_experts",
    )(te, nv, nu, src, xn, wg, wu, wd)


def _combine_kernel(pos_ref, h_ref, rw_ref, gate_ref, gfin_ref, y_hbm, o_ref, buf_ref, sem):
    i = pl.program_id(0)
    r_blk = h_ref.shape[0]
    slot = i % 2

    def issue_tile(tile, s):
        def issue(r, carry):
            t = tile * r_blk + r
            _row_copy(y_hbm, pos_ref[2 * t], buf_ref.at[s, 0], r, sem.at[s]).start()
            _row_copy(y_hbm, pos_ref[2 * t + 1], buf_ref.at[s, 1], r, sem.at[s]).start()
            return carry

        lax.fori_loop(0, r_blk, issue, 0, unroll=8)

    @pl.when(i == 0)
    def _():
        issue_tile(0, 0)

    @pl.when(i + 1 < pl.num_programs(0))
    def _():
        issue_tile(i + 1, 1 - slot)

    def wait(r, carry):
        _row_copy(y_hbm, 0, buf_ref.at[slot, 0], r, sem.at[slot]).wait()
        _row_copy(y_hbm, 0, buf_ref.at[slot, 1], r, sem.at[slot]).wait()
        return carry

    lax.fori_loop(0, r_blk, wait, 0, unroll=8)
    y = rw_ref[:, 0:1] * buf_ref[slot, 0] + rw_ref[:, 1:2] * buf_ref[slot, 1]
    hn = h_ref[...] + gate_ref[...] * y
    o_ref[...] = _rms(hn, gfin_ref[...])


def _combine(pos, h, rw, modr, row_fn, gfin, ys):
    m = h.shape[0]
    r = GATHER_R
    return pl.pallas_call(
        _combine_kernel,
        out_shape=jax.ShapeDtypeStruct((m, D), F32),
        grid_spec=pltpu.PrefetchScalarGridSpec(
            num_scalar_prefetch=1,
            grid=(m // r,),
            in_specs=[pl.BlockSpec((r, D), lambda i, pos: (i, 0)),
                      pl.BlockSpec((r, LANE), lambda i, pos: (i, 0)),
                      pl.BlockSpec((None, None, 1, D), lambda i, pos: (row_fn(i), 5, 0, 0)),
                      pl.BlockSpec((1, D), lambda i, pos: (0, 0)),
                      pl.BlockSpec(memory_space=pl.ANY)],
            out_specs=pl.BlockSpec((r, D), lambda i, pos: (i, 0)),
            scratch_shapes=[pltpu.VMEM((2, 2, r, D), F32), pltpu.SemaphoreType.DMA((2,))]),
        name="moe_combine",
    )(pos, h, rw, modr, gfin, ys)


def _moe(xn, h, ri, rw, cnt, modr, row_fn, gfin, wg, wu, wd):
    m = h.shape[0]
    tm = MOE_TM
    nt = MOE_NT
    counts = cnt[0, :N_EXPERTS].astype(jnp.int32)
    nt_e = (counts + tm - 1) // tm
    t_end = jnp.cumsum(nt_e)
    t_start = t_end - nt_e
    n_used = t_end[-1]
    tid = jnp.arange(nt, dtype=jnp.int32)
    te_raw = jnp.sum((tid[:, None] >= t_end[None, :]).astype(jnp.int32), axis=1)
    te_last = jnp.sum((n_used - 1 >= t_end).astype(jnp.int32))
    te = jnp.minimum(te_raw, te_last).astype(jnp.int32)
    nv = jnp.where(tid < n_used, jnp.clip(counts[te] - (tid - t_start[te]) * tm, 0, tm), 0).astype(jnp.int32)
    pos = (t_start[ri[:, 0:2]] * tm + ri[:, 2:4]).astype(jnp.int32)
    tok = jnp.repeat(jnp.arange(m, dtype=jnp.int32), 2)
    src = jnp.zeros(((nt + 2) * tm,), jnp.int32).at[pos.reshape(-1)].set(tok)
    nu = n_used.reshape(1).astype(jnp.int32)
    ys = _experts(xn, src, te, nv, nu, wg, wu, wd)
    return _combine(pos.reshape(-1), h, rw, modr, row_fn, gfin, ys)


def _relayout_w_in(w):
    main = jnp.concatenate([w[:, :4096], w[:, 4112:5648]], axis=1).astype(BF16)
    pad = jnp.zeros((w.shape[0], LANE - 48), w.dtype)
    small = jnp.concatenate([w[:, 4096:4112], w[:, 5648:5680], pad], axis=1).astype(BF16)
    return main, small


def _head_rows(v, direction):
    return jnp.zeros((1, LANE), F32).at[0, 8 * direction:8 * direction + SSD_HEADS].set(v.astype(F32))


def kernel(x, c, ctx, c_ctx, w_ada, b_ada, g_mix, g_ffn, w_in, w_out, sc_conv_w, cm_w_s, cm_b_s, mb_conv_w,
           mb_conv_b, mb_a_log, mb_dt_bias, mb_d, mb_norm_g, gla_w_gate, gla_b_gate, gla_norm_g, ffn_w_gate,
           ffn_w_up, ffn_w_down, moe_router, moe_w_gate, moe_w_up, moe_w_down, g_final):
    nb, seq, _ = x.shape
    ctx_len = ctx.shape[1]
    depth = w_ada.shape[0]
    assert depth == 2, "supported stack: a dense-FFN layer followed by a last, routed-FFN layer"
    m_lat = nb * seq
    h = x.reshape(m_lat, D)
    hc = ctx.reshape(nb * ctx_len, D)

    cond = jnp.concatenate([c, c_ctx[None, :]], axis=0)
    cb = jnp.broadcast_to(cond[:, :, None], (nb + 1, D, LANE))
    mods = _ada(cb, w_ada, b_ada).reshape(depth, 8, N_MOD, 1, D)

    tm_in = 1024
    lat_row_in = lambda i: i // (seq // tm_in)
    lat_row_512 = lambda i: i // (seq // 512)
    lat_row_g = lambda i: i // (seq // GATHER_R)
    ctx_row = lambda i: nb

    head_of_col = np.arange(GW) // SSD_HD
    expand = [jnp.asarray(np.arange(LANE)[:, None] == 8 * d + head_of_col[None, :], dtype=BF16) for d in range(2)]
    out = None
    for i in range(depth):
        last = i == depth - 1
        modr = mods[i]
        w_main, w_small = _relayout_w_in(w_in[i])
        g_mix_i = g_mix[i].reshape(1, D)
        g_ffn_i = g_ffn[i].reshape(1, D)
        p, ps = _inproj(h, g_mix_i, modr, lat_row_in, w_main, w_small, tm_in)
        pc, psc = _inproj(hc, g_mix_i, modr, ctx_row, w_main, w_small, nb * ctx_len)

        ws = cm_w_s[i].astype(BF16)
        bs = jnp.repeat(cm_b_s[i].T, 128, axis=1)
        ya = _scm(p, sc_conv_w[i], ws, bs, GRID_W)
        if not last:
            yac = _scm(pc, sc_conv_w[i], ws, bs, ctx_len)

        conv_b = mb_conv_b[i].reshape(1, -1)
        s_zero = jnp.zeros((nb, SSD_Q, GW), F32)
        skip = jnp.repeat(mb_d[i, 0] + mb_d[i, 1], SSD_HD).reshape(1, GW)
        fin = (skip, mb_norm_g[i].reshape(1, GW))
        ssd_c = [(mb_conv_w[i], conv_b, _head_rows(mb_a_log[i, d], d), _head_rows(mb_dt_bias[i, d], d), expand[d])
                 for d in range(2)]
        y0c, s0c = _ssd(pc, psc, s_zero, ssd_c[0], nb, ctx_len, 0, False)
        y0, _ = _ssd(p, ps, s0c, ssd_c[0], nb, GRID_W, 0, False)
        if last:
            _, s1c = _ssd(pc, psc, s_zero, ssd_c[1], nb, ctx_len, 1, False)
        else:
            ymbc, s1c = _ssd(pc, psc, s_zero, ssd_c[1], nb, ctx_len, 1, True, y0c, fin)
        ymb, _ = _ssd(p, ps, s1c, ssd_c[1], nb, GRID_W, 1, True, y0, fin)

        g_zero = jnp.zeros((nb, GLA_DV, 256), F32)
        gla_c = [(jnp.zeros((LANE, 256), F32).at[16 + 16 * d:32 + 16 * d].set(gla_w_gate[i, d]).astype(BF16),
                  gla_b_gate[i, d].reshape(1, 256)) for d in range(2)]
        ng = gla_norm_g[i].reshape(1, GLA_DV)
        o0c, t0c = _gla(pc, psc, g_zero, gla_c[0], nb, 0, False)
        o0, _ = _gla(p, ps, t0c, gla_c[0], nb, 0, False)
        if last:
            _, t1c = _gla(pc, psc, g_zero, gla_c[1], nb, 1, False)
        else:
            yglac, t1c = _gla(pc, psc, g_zero, gla_c[1], nb, 1, True, o0c, ng)
        ygla = _gla(p, ps, t1c, gla_c[1], nb, 1, True, o0, ng)[0]

        w_out_b = w_out[i].astype(BF16)
        j = i // 2
        if i % 2 == 0:
            hcn, xcn = _outproj(hc, yac, ymbc, yglac, w_out_b, modr, ctx_row, g_ffn_i)
            hc, wg, wu, wd = _ffn(xcn, hcn, modr, ctx_row, ffn_w_gate[j], ffn_w_up[j], ffn_w_down[j],
                                  nb * ctx_len, True)
            hn, xn = _outproj(h, ya, ymb, ygla, w_out_b, modr, lat_row_512, g_ffn_i)
            h = _ffn(xn, hn, modr, lat_row_in, wg, wu, wd, tm_in, False)[0]
        else:
            w_router = jnp.pad(moe_router[j], ((0, 0), (0, LANE - N_EXPERTS)))
            hn, xn, ri, rw, cnt = _outproj(h, ya, ymb, ygla, w_out_b, modr, lat_row_512, g_ffn_i, w_router)
            if last:
                out = _moe(xn, hn, ri, rw, cnt, modr, lat_row_g, g_final.reshape(1, D),
                           moe_w_gate[j], moe_w_up[j], moe_w_down[j])
            else:
                raise NotImplementedError("routed FFN is only implemented as the last layer's channel mixer")
    return out.reshape(nb, seq, D)
```

```python
import functools

import jax
import jax.numpy as jnp
import numpy as np
from jax import lax
from jax.experimental import pallas as pl
from jax.experimental.pallas import tpu as pltpu

F32 = jnp.float32
BF16 = jnp.bfloat16
EPS = 1e-6

D = 2048
GW = D // 4
GRID_W = 64
N_MOD = 6
LANE = 128
BF16_SUBLANES = 16
P_MAIN = 5632

SSD_Q = 128
SSD_HEADS = 8
SSD_HD = 64
GLA_C = 64
GLA_HEADS = 4
GLA_DK = 64
GLA_DV = 128
GLA_RANK = 16
N_EXPERTS = 8

FFN_SB = 256
MOE_SB = 272
MOE_TM = 4 * MOE_SB
MOE_TF = 512
MOE_NT = 2 * 8192 // MOE_TM + N_EXPERTS
MOE_ISSUE = 78
MOE_STAGE = 14 * MOE_ISSUE
MOE_UNROLL = 6
MOE_VMEM_LIMIT = 127 * 512 * 1024
GATHER_R = 256


def _silu(x):
    return x * jax.nn.sigmoid(x)


def _softplus(x):
    return jnp.maximum(x, 0.0) + jnp.log1p(jnp.exp(-jnp.abs(x)))


def _split3(x):
    hi = x.astype(BF16)
    r = x - hi.astype(F32)
    mid = r.astype(BF16)
    lo = (r - mid.astype(F32)).astype(BF16)
    return hi, mid, lo


def _dot(a, b):
    return jnp.dot(a, b, preferred_element_type=F32)


def _dot_nt(a, b):
    return lax.dot_general(a, b, (((1,), (1,)), ((), ())), preferred_element_type=F32)


def _dot_tn(a, b):
    return lax.dot_general(a, b, (((0,), (0,)), ((), ())), preferred_element_type=F32)


def _dot01_left(m01, x):
    hi, mid, lo = _split3(x)
    return _dot(m01, hi) + _dot(m01, mid) + _dot(m01, lo)


def _dot01_right(x, m01):
    hi, mid, lo = _split3(x)
    return _dot(hi, m01) + _dot(mid, m01) + _dot(lo, m01)


def _rms(x, g):
    return x * lax.rsqrt(jnp.mean(x * x, axis=-1, keepdims=True) + EPS) * g


def _conv3(x, w, seg):
    n = x.shape[0]
    t = lax.broadcasted_iota(jnp.int32, (n, 1), 0) & (seg - 1)
    prev = jnp.where(t == 0, 0.0, pltpu.roll(x, 1, 0))
    nxt = jnp.where(t == seg - 1, 0.0, pltpu.roll(x, n - 1, 0))
    return prev * w[0:1] + x * w[1:2] + nxt * w[2:3]


def _ada_kernel(cb_ref, w_ref, b_ref, o_ref, s_ref):
    @pl.when((pl.program_id(0) == 0) & (pl.program_id(1) == 0))
    def _():
        s_ref[...] = _silu(cb_ref[...])

    tn = w_ref.shape[1]
    o_ref[...] = jnp.zeros_like(o_ref)
    for j in range(tn // LANE):
        wj = w_ref[:, j * LANE:(j + 1) * LANE]
        for m in range(3):
            o_ref[m:m + 1, j * LANE:(j + 1) * LANE] = (
                jnp.sum(wj * s_ref[m], axis=0, keepdims=True) + b_ref[:, j * LANE:(j + 1) * LANE])


def _ada(cb, w_ada, b_ada):
    depth = w_ada.shape[0]
    n = w_ada.shape[2]
    tn = 1024
    return pl.pallas_call(
        _ada_kernel,
        out_shape=jax.ShapeDtypeStruct((depth, 8, n), F32),
        grid=(depth, n // tn),
        in_specs=[pl.BlockSpec((3, D, LANE), lambda l, j: (0, 0, 0)),
                  pl.BlockSpec((None, D, tn), lambda l, j: (l, 0, j)),
                  pl.BlockSpec((None, 1, tn), lambda l, j: (l, 0, j))],
        out_specs=pl.BlockSpec((None, 8, tn), lambda l, j: (l, 0, j)),
        scratch_shapes=[pltpu.VMEM((3, D, LANE), F32)],
        name="ada",
    )(cb, w_ada, b_ada.reshape(depth, 1, n))


def _inproj_kernel(h_ref, g_ref, sh_ref, sc_ref, w_ref, ws_ref, o_ref, os_ref, xn_ref):
    @pl.when(pl.program_id(1) == 0)
    def _():
        xn = _rms(h_ref[...], g_ref[...])
        xn_ref[...] = (xn * (1.0 + sc_ref[...]) + sh_ref[...]).astype(BF16)
        os_ref[...] = _dot(xn_ref[...], ws_ref[...])

    o_ref[...] = _dot(xn_ref[...], w_ref[...]).astype(BF16)


def _mod_spec(row_fn, k):
    return pl.BlockSpec((None, None, 1, D), lambda i, j: (row_fn(i), k, 0, 0))


def _inproj(h, g, modr, row_fn, w, w_small, layer, tm):
    m = h.shape[0]
    tn = P_MAIN // 4
    return pl.pallas_call(
        _inproj_kernel,
        out_shape=(jax.ShapeDtypeStruct((m, P_MAIN), BF16), jax.ShapeDtypeStruct((m, LANE), F32)),
        grid=(m // tm, P_MAIN // tn),
        in_specs=[pl.BlockSpec((tm, D), lambda i, j: (i, 0)),
                  pl.BlockSpec((1, D), lambda i, j: (0, 0)),
                  _mod_spec(row_fn, 0), _mod_spec(row_fn, 1),
                  pl.BlockSpec((None, D, tn), lambda i, j: (layer, 0, j)),
                  pl.BlockSpec((None, D, LANE), lambda i, j: (layer, 0, 0))],
        out_specs=(pl.BlockSpec((tm, tn), lambda i, j: (i, j)),
                   pl.BlockSpec((tm, LANE), lambda i, j: (i, 0))),
        scratch_shapes=[pltpu.VMEM((tm, D), BF16)],
        name="inproj",
    )(h, g, modr, modr, w, w_small)


def _gelu_tanh(x):
    c = 0.7978845608028654
    return x * (0.5 * (1.0 + jnp.tanh(c * (x + 0.044715 * (x * x * x)))))


def _scm_kernel(sc_ref, u_ref, v_ref, cw_ref, ws_ref, bs_ref, o_ref, *, seg):
    t = sc_ref.shape[0]
    bgate = sc_ref[:, 0:GW].astype(F32)
    gated = sc_ref[:, GW:2 * GW].astype(F32) * sc_ref[:, 2 * GW:3 * GW].astype(F32)
    y_sc = bgate * _conv3(gated, cw_ref[...], seg)
    o_ref[:, 0:GW] = y_sc.astype(BF16)

    u = _gelu_tanh(u_ref[...].astype(F32))
    v = _gelu_tanh(v_ref[...].astype(F32)).astype(BF16)
    for c in range(t // 128):
        rows = slice(c * 128, (c + 1) * 128)
        for g in range(4):
            cols = slice(g * 128, (g + 1) * 128)
            s = _dot(ws_ref[g], v[rows, cols]) + bs_ref[:, cols]
            o_ref[rows, GW + g * 128:GW + (g + 1) * 128] = (u[rows, cols] * s).astype(BF16)


def _scm(p, cw, ws, bs, seg):
    m = p.shape[0]
    t = 512
    return pl.pallas_call(
        functools.partial(_scm_kernel, seg=seg),
        out_shape=jax.ShapeDtypeStruct((m, 2 * GW), BF16),
        grid=(m // t,),
        in_specs=[pl.BlockSpec((t, 3 * GW), lambda i: (i, 0)),
                  pl.BlockSpec((t, GW), lambda i: (i, 3)),
                  pl.BlockSpec((t, GW), lambda i: (i, 4)),
                  pl.BlockSpec((3, GW), lambda i: (0, 0)),
                  pl.BlockSpec((4, 128, 128), lambda i: (0, 0, 0)),
                  pl.BlockSpec((128, GW), lambda i: (0, 0))],
        out_specs=pl.BlockSpec((t, 2 * GW), lambda i: (i, 0)),
        name="scm",
    )(p, p, p, cw, ws, bs)


def _ssd_kernel(*refs, direction, final, seg):
    if final:
        (xbc_ref, sm_ref, cw_ref, cb_ref, alog_ref, dtb_ref, e_ref, s0_ref,
         y0_ref, z_ref, skip_ref, ng_ref, y_ref, sfin_ref, st_ref) = refs
    else:
        (xbc_ref, sm_ref, cw_ref, cb_ref, alog_ref, dtb_ref, e_ref, s0_ref,
         y_ref, sfin_ref, st_ref) = refs
    j = pl.program_id(0)
    nb, tb = xbc_ref.shape[0], xbc_ref.shape[1]
    q = SSD_Q

    @pl.when(j == 0)
    def _():
        st_ref[...] = s0_ref[...]

    a_neg = -jnp.exp(alog_ref[...])
    ii = lax.broadcasted_iota(jnp.int32, (q, q), 0)
    jj = lax.broadcasted_iota(jnp.int32, (q, q), 1)
    mask = (jj <= ii) if direction == 0 else (jj >= ii)
    tri = mask.astype(BF16)
    last = q - 1 if direction == 0 else 0
    lane0 = 8 * direction
    upper_half = lax.broadcasted_iota(jnp.int32, (q, 128), 1) >= SSD_HD
    expand = e_ref[...]

    xcs = [_silu(_conv3(xbc_ref[b].astype(F32), cw_ref[...], seg) + cb_ref[...])
           for b in range(nb)]
    states = [st_ref[b] for b in range(nb)]

    def chunk(b, c, st):
        rows = slice(c * q, (c + 1) * q)
        xc = xcs[b]
        dt = _softplus(sm_ref[b, rows, :] + dtb_ref[...])
        acs = _dot01_left(tri, dt * a_neg)
        tot = acs[last:last + 1, :]
        acs_t = acs.T
        dt_t = dt.T
        wx = _dot01_right(jnp.exp(tot - acs) * dt, expand)
        ea = _dot01_right(jnp.exp(acs), expand)
        xs = xc[rows, 0:GW]
        xw = (wx * xs).astype(BF16)
        stb = st.astype(BF16)
        ys = []
        new_st = []
        for g in range(2):
            bg = xc[rows, GW + g * 128:GW + (g + 1) * 128].astype(BF16)
            cg = xc[rows, GW + 256 + g * 128:GW + 256 + (g + 1) * 128].astype(BF16)
            cb = _dot_nt(cg, bg)
            gcols = slice(g * 256, (g + 1) * 256)
            yint = _dot(cg, stb[:, gcols])
            for pr in range(2):
                pidx = 2 * g + pr
                pcols = slice(pidx * 128, (pidx + 1) * 128)
                xp = xs[:, pcols]
                acc = yint[:, pr * 128:(pr + 1) * 128] * ea[:, pcols]
                for hh in range(2):
                    ln = lane0 + 2 * pidx + hh
                    sg = acs[:, ln:ln + 1] - acs_t[ln:ln + 1, :]
                    decay = jnp.where(mask, jnp.exp(jnp.where(mask, sg, 0.0)), 0.0)
                    scores = (cb * decay * dt_t[ln:ln + 1, :]).astype(BF16)
                    xh = jnp.where(upper_half == (hh == 1), xp, 0.0).astype(BF16)
                    acc = acc + _dot(scores, xh)
                ys.append(acc)
            new_st.append(ea[last:last + 1, gcols] * st[:, gcols] + _dot_tn(bg, xw[:, gcols]))
        y = jnp.concatenate(ys, axis=1)
        if final:
            y = y0_ref[b, rows, :] + y + xs * skip_ref[...]
            y = y * _silu(z_ref[b, rows, :].astype(F32))
            y_ref[b, rows, :] = _rms(y, ng_ref[...]).astype(y_ref.dtype)
        else:
            y_ref[b, rows, :] = y
        return jnp.concatenate(new_st, axis=1)

    nchunk = tb // q
    order = range(nchunk) if direction == 0 else range(nchunk - 1, -1, -1)
    for c in order:
        for b in range(nb):
            states[b] = chunk(b, c, states[b])
    for b in range(nb):
        st_ref[b] = states[b]

    @pl.when(j == pl.num_programs(0) - 1)
    def _():
        sfin_ref[...] = st_ref[...]


def _scan_block_map(nblk, direction):
    return (lambda j: j) if direction == 0 else (lambda j: nblk - 1 - j)


def _ssd(p, ps, s0, consts, nb, seg, direction, final, y0=None, fin=None):
    m = p.shape[0]
    seq = m // nb
    tb = 256
    nblk = seq // tb
    blk = _scan_block_map(nblk, direction)
    p3 = p.reshape(nb, seq, P_MAIN)
    full = lambda shape: pl.BlockSpec(shape, lambda j: (0,) * len(shape))
    rows = lambda width, col: pl.BlockSpec((nb, tb, width), lambda j: (0, blk(j), col))
    in_specs = [rows(1024, 3), rows(LANE, 0),
                full((3, 1024)), full((1, 1024)), full((1, LANE)), full((1, LANE)), full((LANE, GW)),
                full((nb, SSD_Q, GW))]
    args = [p3, ps.reshape(nb, seq, LANE), *consts, s0]
    if final:
        in_specs += [rows(GW, 0), rows(GW, 5), full((1, GW)), full((1, GW))]
        args += [y0.reshape(nb, seq, GW), p3, *fin]
    y, s_fin = pl.pallas_call(
        functools.partial(_ssd_kernel, direction=direction, final=final, seg=seg),
        out_shape=(jax.ShapeDtypeStruct((nb, seq, GW), BF16 if final else F32),
                   jax.ShapeDtypeStruct((nb, SSD_Q, GW), F32)),
        grid=(nblk,),
        in_specs=in_specs,
        out_specs=(rows(GW, 0), full((nb, SSD_Q, GW))),
        scratch_shapes=[pltpu.VMEM((nb, SSD_Q, GW), F32)],
        name=f"ssd_d{direction}",
    )(*args)
    return y.reshape(m, GW), s_fin


def _gla_kernel(*refs, direction, final):
    if final:
        (q_ref, k_ref, v_ref, sm_ref, wg_ref, bg_ref, s0_ref, o0_ref, g_ref, ng_ref,
         o_ref, sfin_ref, st_ref) = refs
    else:
        (q_ref, k_ref, v_ref, sm_ref, wg_ref, bg_ref, s0_ref, o_ref, sfin_ref, st_ref) = refs
    j = pl.program_id(0)
    nb, tb = q_ref.shape[0], q_ref.shape[1]
    c_len = GLA_C

    @pl.when(j == 0)
    def _():
        st_ref[...] = s0_ref[...]

    ii = lax.broadcasted_iota(jnp.int32, (tb, tb), 0)
    jj = lax.broadcasted_iota(jnp.int32, (tb, tb), 1)
    same = (ii >> 6) == (jj >> 6)
    btri = (same & ((jj <= ii) if direction == 0 else (jj >= ii))).astype(BF16)
    gcums = []
    for b in range(nb):
        gk = -_softplus(-(_dot(sm_ref[b].astype(BF16), wg_ref[...]) + bg_ref[...])) * (1.0 / 16.0)
        gcums.append(_dot01_left(btri, gk))

    hs = GLA_HEADS * c_len
    ri = lax.broadcasted_iota(jnp.int32, (hs, hs), 0)
    ci = lax.broadcasted_iota(jnp.int32, (hs, hs), 1)
    own_lanes = (ri >> 6) == (ci >> 6)
    ti, tj = ri & (c_len - 1), ci & (c_len - 1)
    own_causal = own_lanes & ((tj <= ti) if direction == 0 else (tj >= ti))
    last = c_len - 1 if direction == 0 else 0
    ref = c_len // 2 if direction == 0 else c_len - 1 - c_len // 2

    def stack_heads(x):
        return jnp.where(own_lanes, jnp.concatenate([x] * GLA_HEADS, axis=0), 0.0).astype(BF16)

    states = [st_ref[b] for b in range(nb)]

    def chunk(b, c, st):
        rows = slice(c * c_len, (c + 1) * c_len)
        gc = gcums[b][rows, :]
        gl = gc[last:last + 1, :]
        gr = gc[ref:ref + 1, :]
        qc = q_ref[b, rows, :].astype(F32) * (GLA_DK ** -0.5)
        kc = k_ref[b, rows, :].astype(F32)
        vc = v_ref[b, rows, :]
        v_rows = jnp.concatenate([vc[:, h * GLA_DV:(h + 1) * GLA_DV] for h in range(GLA_HEADS)], axis=0)
        kb = (kc * jnp.exp(gr - gc)).astype(BF16)
        att = _dot_nt(stack_heads(qc * jnp.exp(gc - gr)), jnp.concatenate([kb] * GLA_HEADS, axis=0))
        att = jnp.where(own_causal, att, 0.0).astype(BF16)
        y = _dot(att, v_rows) + _dot_nt(stack_heads(qc * jnp.exp(gc)), st.astype(BF16))
        contrib = _dot_tn(v_rows, stack_heads(kc * jnp.exp(gl - gc)))
        new_st = jnp.exp(gl) * st + contrib
        o = jnp.concatenate([y[h * c_len:(h + 1) * c_len, :] for h in range(GLA_HEADS)], axis=1)
        if final:
            o = o0_ref[b, rows, :] + o
            normed = [_rms(o[:, h * GLA_DV:(h + 1) * GLA_DV], ng_ref[...]) for h in range(GLA_HEADS)]
            gate = _silu(g_ref[b, rows, :].astype(F32))
            o_ref[b, rows, :] = (jnp.concatenate(normed, axis=1) * gate).astype(o_ref.dtype)
        else:
            o_ref[b, rows, :] = o
        return new_st

    nchunk = tb // c_len
    order = range(nchunk) if direction == 0 else range(nchunk - 1, -1, -1)
    for c in order:
        for b in range(nb):
            states[b] = chunk(b, c, states[b])
    for b in range(nb):
        st_ref[b] = states[b]

    @pl.when(j == pl.num_programs(0) - 1)
    def _():
        sfin_ref[...] = st_ref[...]


def _gla(p, ps, s0, consts, nb, direction, final, o0=None, ng=None):
    m = p.shape[0]
    seq = m // nb
    tb = 256
    nblk = seq // tb
    blk = _scan_block_map(nblk, direction)
    p3 = p.reshape(nb, seq, P_MAIN)
    full = lambda shape: pl.BlockSpec(shape, lambda j: (0,) * len(shape))
    rows = lambda width, col: pl.BlockSpec((nb, tb, width), lambda j: (0, blk(j), col))
    in_specs = [rows(256, 16), rows(256, 17), rows(GW, 9), rows(LANE, 0),
                full((LANE, 256)), full((1, 256)), full((nb, GLA_DV, 256))]
    args = [p3, p3, p3, ps.reshape(nb, seq, LANE), *consts, s0]
    if final:
        in_specs += [rows(GW, 0), rows(GW, 10), full((1, GLA_DV))]
        args += [o0.reshape(nb, seq, GW), p3, ng]
    o, s_fin = pl.pallas_call(
        functools.partial(_gla_kernel, direction=direction, final=final),
        out_shape=(jax.ShapeDtypeStruct((nb, seq, GW), BF16 if final else F32),
                   jax.ShapeDtypeStruct((nb, GLA_DV, 256), F32)),
        grid=(nblk,),
        in_specs=in_specs,
        out_specs=(rows(GW, 0), full((nb, GLA_DV, 256))),
        scratch_shapes=[pltpu.VMEM((nb, GLA_DV, 256), F32)],
        name=f"gla_d{direction}",
    )(*args)
    return o.reshape(m, GW), s_fin


def _outproj_kernel(*refs, route):
    if route:
        (h_ref, ya_ref, ymb_ref, ygla_ref, w_ref, gate_ref, g_ref, sh_ref, sc_ref, wr_ref,
         hn_ref, xn_ref, ri_ref, rw_ref, cnt_ref, carry_ref, xprev_ref) = refs
    else:
        (h_ref, ya_ref, ymb_ref, ygla_ref, w_ref, gate_ref, g_ref, sh_ref, sc_ref,
         hn_ref, xn_ref) = refs

    def project():
        acc = (_dot(ya_ref[...], w_ref[0:2 * GW, :]) + _dot(ymb_ref[...], w_ref[2 * GW:3 * GW, :])
               + _dot(ygla_ref[...], w_ref[3 * GW:4 * GW, :]))
        hn = h_ref[...] + gate_ref[...] * acc
        hn_ref[...] = hn
        xn = _rms(hn, g_ref[...]) * (1.0 + sc_ref[...]) + sh_ref[...]
        xn_ref[...] = xn.astype(xn_ref.dtype)
        return xn

    if not route:
        project()
        return

    i = pl.program_id(0)
    n_tiles = pl.num_programs(0) - 1

    def route_prev():
        _route_tile(xprev_ref[...], wr_ref, ri_ref, rw_ref, cnt_ref, carry_ref)

    @pl.when(i == 0)
    def _():
        carry_ref[...] = jnp.zeros_like(carry_ref)
        xprev_ref[...] = project()

    @pl.when((i > 0) & (i < n_tiles))
    def _():
        route_prev()
        xprev_ref[...] = project()

    @pl.when(i == n_tiles)
    def _():
        route_prev()


def _route_tile(xn, wr_ref, ri_ref, rw_ref, cnt_ref, carry_ref):
    tm = xn.shape[0]
    xh = xn.astype(BF16)
    xl = (xn - xh.astype(F32)).astype(BF16)
    wr = wr_ref[...]
    wh = wr.astype(BF16)
    wl = (wr - wh.astype(F32)).astype(BF16)
    hh_hl = _dot(xh, jnp.concatenate([wh, wl], axis=1))
    logits = hh_hl[:, 0:LANE] + hh_hl[:, LANE:2 * LANE] + _dot(xl, wh)
    lane = lax.broadcasted_iota(jnp.int32, (tm, LANE), 1)
    lane_f = lane.astype(F32)
    neg = jnp.float32(-3.0e38)
    lm = jnp.where(lane < N_EXPERTS, logits, neg)
    m1 = jnp.max(lm, axis=-1, keepdims=True)
    i1 = jnp.min(jnp.where(lm == m1, lane_f, float(LANE)), axis=-1, keepdims=True)
    lm2 = jnp.where(lane_f == i1, neg, lm)
    m2 = jnp.max(lm2, axis=-1, keepdims=True)
    i2 = jnp.min(jnp.where(lm2 == m2, lane_f, float(LANE)), axis=-1, keepdims=True)
    e = jnp.exp(m2 - m1)
    w1 = 1.0 / (1.0 + e)
    w2 = e / (1.0 + e)
    sel1 = lane_f == i1
    sel2 = lane_f == i2
    onehot = jnp.where(sel1 | sel2, 1.0, 0.0)
    ti = lax.broadcasted_iota(jnp.int32, (tm, tm), 0)
    tj = lax.broadcasted_iota(jnp.int32, (tm, tm), 1)
    incl = _dot((tj <= ti).astype(BF16), onehot.astype(BF16)) + carry_ref[...]
    excl = incl - onehot
    r1 = jnp.sum(jnp.where(sel1, excl, 0.0), axis=-1, keepdims=True)
    r2 = jnp.sum(jnp.where(sel2, excl, 0.0), axis=-1, keepdims=True)
    info = jnp.where(lane == 0, i1, jnp.where(lane == 1, i2, jnp.where(lane == 2, r1, jnp.where(lane == 3, r2, 0.0))))
    ri_ref[...] = info.astype(jnp.int32)
    rw_ref[...] = jnp.where(lane == 0, w1, jnp.where(lane == 1, w2, 0.0))
    carry_ref[...] = incl[tm - 1:tm, :]
    cnt_ref[...] = incl[tm - 1:tm, :]


def _outproj(h, ya, ymb, ygla, w, modr, row_fn, g, w_router=None):
    m = h.shape[0]
    tm = 512
    route = w_router is not None
    n_tiles = m // tm
    cur = (lambda i: jnp.minimum(i, n_tiles - 1)) if route else (lambda i: i)
    prev = lambda i: jnp.maximum(i - 1, 0)
    mod = lambda k: pl.BlockSpec((None, None, 1, D), lambda i: (row_fn(cur(i)), k, 0, 0))
    in_specs = [pl.BlockSpec((tm, D), lambda i: (cur(i), 0)),
                pl.BlockSpec((tm, 2 * GW), lambda i: (cur(i), 0)),
                pl.BlockSpec((tm, GW), lambda i: (cur(i), 0)),
                pl.BlockSpec((tm, GW), lambda i: (cur(i), 0)),
                pl.BlockSpec((D, D), lambda i: (0, 0)),
                mod(2), pl.BlockSpec((1, D), lambda i: (0, 0)), mod(3), mod(4)]
    args = [h, ya, ymb, ygla, w, modr, g, modr, modr]
    out_shape = [jax.ShapeDtypeStruct((m, D), F32), jax.ShapeDtypeStruct((m, D), F32 if route else BF16)]
    out_specs = [pl.BlockSpec((tm, D), lambda i: (cur(i), 0)), pl.BlockSpec((tm, D), lambda i: (cur(i), 0))]
    scratch = []
    if route:
        in_specs.append(pl.BlockSpec((D, LANE), lambda i: (0, 0)))
        args.append(w_router)
        out_shape += [jax.ShapeDtypeStruct((m, LANE), jnp.int32), jax.ShapeDtypeStruct((m, LANE), F32),
                      jax.ShapeDtypeStruct((1, LANE), F32)]
        out_specs += [pl.BlockSpec((tm, LANE), lambda i: (prev(i), 0)), pl.BlockSpec((tm, LANE), lambda i: (prev(i), 0)),
                      pl.BlockSpec((1, LANE), lambda i: (0, 0))]
        scratch = [pltpu.VMEM((1, LANE), F32), pltpu.VMEM((tm, D), F32)]
    return pl.pallas_call(
        functools.partial(_outproj_kernel, route=route),
        out_shape=tuple(out_shape),
        grid=(n_tiles + 1 if route else n_tiles,),
        in_specs=in_specs,
        out_specs=tuple(out_specs),
        scratch_shapes=scratch,
        name="outproj_route" if route else "outproj",
    )(*args)


def _swiglu_rows(x_ref, wg_ref, wu_ref, wd_ref, o_ref, wgb_ref, wub_ref, wdb_ref, sb_rows, nsb, gate,
                 prologue=None):
    def sub_block(sb):
        start = sb * sb_rows
        return pl.ds(start if isinstance(sb, int) else pl.multiple_of(start, BF16_SUBLANES), sb_rows)

    def up(sb):
        x = x_ref[sub_block(sb), :]
        return (_silu(_dot(x, wgb_ref[...])) * _dot(x, wub_ref[...])).astype(BF16)

    def down(sb, hidden):
        rows = sub_block(sb)
        y = _dot(hidden, wdb_ref[...])
        o_ref[rows, :] += y if gate is None else gate * y

    if prologue is not None:
        prologue()
    if wg_ref is not None:
        wgb_ref[...] = wg_ref[...].astype(BF16)
        wub_ref[...] = wu_ref[...].astype(BF16)
    hidden = up(0)
    if wd_ref is not None:
        wdb_ref[...] = wd_ref[...].astype(BF16)
    if isinstance(nsb, int):
        for sb in range(1, nsb):
            nxt = up(sb)
            down(sb - 1, hidden)
            hidden = nxt
    else:
        def body(sb, hid):
            nxt = up(sb)
            down(sb - 1, hid)
            return nxt

        hidden = lax.fori_loop(1, nsb, body, hidden)
    down(nsb - 1, hidden)


def _ffn_kernel(*refs, emit):
    if emit:
        x_ref, h_hbm, gate_ref, wg_ref, wu_ref, wd_ref, o_ref, wgb_ref, wub_ref, wdb_ref = refs
    else:
        x_ref, h_hbm, gate_ref, wgb_ref, wub_ref, wdb_ref, o_ref = refs
        wg_ref = wu_ref = wd_ref = None
    tm = o_ref.shape[0]

    @pl.when(pl.program_id(1) == 0)
    def _():
        pltpu.sync_copy(h_hbm.at[pl.ds(pl.program_id(0) * tm, tm)], o_ref)

    _swiglu_rows(x_ref, wg_ref, wu_ref, wd_ref, o_ref, wgb_ref, wub_ref, wdb_ref, FFN_SB, tm // FFN_SB,
                 gate_ref[...])


def _ffn(xn, h, modr, row_fn, wg, wu, wd, tm, emit):
    m = h.shape[0]
    tf = MOE_TF
    dff = wg.shape[1]
    assert not emit or m == tm
    w_specs = [pl.BlockSpec((D, tf), lambda i, f: (0, f)),
               pl.BlockSpec((D, tf), lambda i, f: (0, f)),
               pl.BlockSpec((tf, D), lambda i, f: (f, 0))]
    out_shape = [jax.ShapeDtypeStruct((m, D), F32)]
    out_specs = [pl.BlockSpec((tm, D), lambda i, f: (i, 0))]
    if emit:
        out_shape += [jax.ShapeDtypeStruct((D, dff), BF16), jax.ShapeDtypeStruct((D, dff), BF16),
                      jax.ShapeDtypeStruct((dff, D), BF16)]
        out_specs += w_specs
    return pl.pallas_call(
        functools.partial(_ffn_kernel, emit=emit),
        out_shape=tuple(out_shape),
        grid=(m // tm, dff // tf),
        in_specs=[pl.BlockSpec((tm, D), lambda i, f: (i, 0)),
                  pl.BlockSpec(memory_space=pl.ANY),
                  pl.BlockSpec((None, None, 1, D), lambda i, f: (row_fn(i), 5, 0, 0))] + w_specs,
        out_specs=tuple(out_specs),
        name="ffn_emit" if emit else "ffn",
    )(xn, h, modr, wg, wu, wd)


def _row_copy(src_hbm, row, dst, r, sem):
    return pltpu.make_async_copy(src_hbm.at[pl.ds(row, 1)], dst.at[pl.ds(r, 1)], sem)


def _expert_kernel(te_ref, nv_ref, nu_ref, src_ref, xn_hbm, wg_ref, wu_ref, wd_ref, o_ref,
                   xb_ref, stage_ref, wgb_ref, wub_ref, wdb_ref, sem):
    i = pl.program_id(0)
    f = pl.program_id(1)
    n_tiles = pl.num_programs(0)
    nf = pl.num_programs(1)
    used = i < nu_ref[0]

    def wait_stage():
        def wait(r, carry):
            _row_copy(xn_hbm, 0, stage_ref, r, sem).wait()
            return carry

        lax.fori_loop(0, MOE_STAGE, wait, 0, unroll=MOE_UNROLL)

    @pl.when(f == 0)
    def _():
        @pl.when(i == 0)
        def _():
            def issue(r, carry):
                _row_copy(xn_hbm, src_ref[r], stage_ref, r, sem).start()
                return carry

            lax.fori_loop(0, MOE_STAGE, issue, 0, unroll=MOE_UNROLL)

        @pl.when(i <= nu_ref[0])
        def _():
            wait_stage()
            xb_ref[...] = stage_ref[0:MOE_TM, :].astype(BF16)

    def issue_next_rows():
        base = (i + 1) * MOE_TM + f * MOE_ISSUE
        for k in range(MOE_ISSUE):
            _row_copy(xn_hbm, src_ref[base + k], stage_ref, f * MOE_ISSUE + k, sem).start()

    @pl.when(used)
    def _():
        @pl.when(f == 0)
        def _():
            o_ref[...] = jnp.zeros_like(o_ref)

        nsb = (nv_ref[i] + MOE_SB - 1) // MOE_SB
        refs = (xb_ref, wg_ref, wu_ref, wd_ref, o_ref, wgb_ref, wub_ref, wdb_ref)

        @pl.when(nsb == MOE_TM // MOE_SB)
        def _():
            _swiglu_rows(*refs, MOE_SB, MOE_TM // MOE_SB, None, issue_next_rows)

        @pl.when(nsb < MOE_TM // MOE_SB)
        def _():
            _swiglu_rows(*refs, MOE_SB, nsb, None, issue_next_rows)

    @pl.when(jnp.logical_not(used) & (f == 0))
    def _():
        o_ref[...] = jnp.zeros_like(o_ref)

    @pl.when(used & (i == n_tiles - 1) & (f == nf - 1))
    def _():
        wait_stage()


def _experts(xn, src, te, nv, nu, wg, wu, wd):
    dff = wg.shape[2]
    nf = dff // MOE_TF
    assert nf * MOE_ISSUE == MOE_STAGE and MOE_STAGE >= MOE_TM

    def f_eff(i, f, nu):
        return jnp.where(i < nu[0], f, nf - 1)

    return pl.pallas_call(
        _expert_kernel,
        out_shape=jax.ShapeDtypeStruct((MOE_NT * MOE_TM, D), F32),
        grid_spec=pltpu.PrefetchScalarGridSpec(
            num_scalar_prefetch=4,
            grid=(MOE_NT, nf),
            in_specs=[pl.BlockSpec(memory_space=pl.ANY),
                      pl.BlockSpec((None, D, MOE_TF), lambda i, f, te, nv, nu, src: (te[i], 0, f_eff(i, f, nu))),
                      pl.BlockSpec((None, D, MOE_TF), lambda i, f, te, nv, nu, src: (te[i], 0, f_eff(i, f, nu))),
                      pl.BlockSpec((None, MOE_TF, D), lambda i, f, te, nv, nu, src: (te[i], f_eff(i, f, nu), 0))],
            out_specs=pl.BlockSpec((MOE_TM, D), lambda i, f, te, nv, nu, src: (i, 0)),
            scratch_shapes=[pltpu.VMEM((MOE_TM, D), BF16), pltpu.VMEM((MOE_STAGE, D), F32),
                            pltpu.VMEM((D, MOE_TF), BF16), pltpu.VMEM((D, MOE_TF), BF16),
                            pltpu.VMEM((MOE_TF, D), BF16), pltpu.SemaphoreType.DMA(())]),
        compiler_params=pltpu.CompilerParams(vmem_limit_bytes=MOE_VMEM_LIMIT),
        name="moe_experts",
    )(te, nv, nu, src, xn, wg, wu, wd)


def _combine_kernel(pos_ref, h_ref, rw_ref, gate_ref, gfin_ref, y_hbm, o_ref, buf_ref, sem):
    i = pl.program_id(0)
    r_blk = h_ref.shape[0]
    slot = i % 2

    def issue_tile(tile, s):
        def issue(r, carry):
            t = tile * r_blk + r
            _row_copy(y_hbm, pos_ref[2 * t], buf_ref.at[s, 0], r, sem.at[s]).start()
            _row_copy(y_hbm, pos_ref[2 * t + 1], buf_ref.at[s, 1], r, sem.at[s]).start()
            return carry

        lax.fori_loop(0, r_blk, issue, 0, unroll=8)

    @pl.when(i == 0)
    def _():
        issue_tile(0, 0)

    @pl.when(i + 1 < pl.num_programs(0))
    def _():
        issue_tile(i + 1, 1 - slot)

    def wait(r, carry):
        _row_copy(y_hbm, 0, buf_ref.at[slot, 0], r, sem.at[slot]).wait()
        _row_copy(y_hbm, 0, buf_ref.at[slot, 1], r, sem.at[slot]).wait()
        return carry

    lax.fori_loop(0, r_blk, wait, 0, unroll=8)
    y = rw_ref[:, 0:1] * buf_ref[slot, 0] + rw_ref[:, 1:2] * buf_ref[slot, 1]
    hn = h_ref[...] + gate_ref[...] * y
    o_ref[...] = _rms(hn, gfin_ref[...])


def _combine(pos, h, rw, modr, row_fn, gfin, ys):
    m = h.shape[0]
    r = GATHER_R
    return pl.pallas_call(
        _combine_kernel,
        out_shape=jax.ShapeDtypeStruct((m, D), F32),
        grid_spec=pltpu.PrefetchScalarGridSpec(
            num_scalar_prefetch=1,
            grid=(m // r,),
            in_specs=[pl.BlockSpec((r, D), lambda i, pos: (i, 0)),
                      pl.BlockSpec((r, LANE), lambda i, pos: (i, 0)),
                      pl.BlockSpec((None, None, 1, D), lambda i, pos: (row_fn(i), 5, 0, 0)),
                      pl.BlockSpec((1, D), lambda i, pos: (0, 0)),
                      pl.BlockSpec(memory_space=pl.ANY)],
            out_specs=pl.BlockSpec((r, D), lambda i, pos: (i, 0)),
            scratch_shapes=[pltpu.VMEM((2, 2, r, D), F32), pltpu.SemaphoreType.DMA((2,))]),
        name="moe_combine",
    )(pos, h, rw, modr, gfin, ys)


def _moe(xn, h, ri, rw, cnt, modr, row_fn, gfin, wg, wu, wd):
    m = h.shape[0]
    tm = MOE_TM
    nt = MOE_NT
    counts = cnt[0, :N_EXPERTS].astype(jnp.int32)
    nt_e = (counts + tm - 1) // tm
    t_end = jnp.cumsum(nt_e)
    t_start = t_end - nt_e
    n_used = t_end[-1]
    tid = jnp.arange(nt, dtype=jnp.int32)
    te_raw = jnp.sum((tid[:, None] >= t_end[None, :]).astype(jnp.int32), axis=1)
    te_last = jnp.sum((n_used - 1 >= t_end).astype(jnp.int32))
    te = jnp.minimum(te_raw, te_last).astype(jnp.int32)
    nv = jnp.where(tid < n_used, jnp.clip(counts[te] - (tid - t_start[te]) * tm, 0, tm), 0).astype(jnp.int32)
    pos = (t_start[ri[:, 0:2]] * tm + ri[:, 2:4]).astype(jnp.int32)
    tok = jnp.repeat(jnp.arange(m, dtype=jnp.int32), 2)
    src = jnp.zeros(((nt + 2) * tm,), jnp.int32).at[pos.reshape(-1)].set(tok)
    nu = n_used.reshape(1).astype(jnp.int32)
    ys = _experts(xn, src, te, nv, nu, wg, wu, wd)
    return _combine(pos.reshape(-1), h, rw, modr, row_fn, gfin, ys)


def _relayout_w_in_kernel(w_ref, main_ref, small_ref):
    main_ref[:, 0:4096] = w_ref[:, 0:4096].astype(BF16)
    tail = w_ref[:, 4096:5680]
    main_ref[:, 4096:P_MAIN] = tail[:, 16:16 + P_MAIN - 4096].astype(BF16)
    small_ref[...] = jnp.zeros_like(small_ref)
    small_ref[:, 0:16] = tail[:, 0:16].astype(BF16)
    small_ref[:, 16:48] = tail[:, P_MAIN - 4096 + 16:].astype(BF16)


def _relayout_w_in(w_in):
    depth, rows, cols = w_in.shape
    tr = 256
    return pl.pallas_call(
        _relayout_w_in_kernel,
        out_shape=(jax.ShapeDtypeStruct((depth, rows, P_MAIN), BF16), jax.ShapeDtypeStruct((depth, rows, LANE), BF16)),
        grid=(depth, rows // tr),
        in_specs=[pl.BlockSpec((None, tr, cols), lambda l, r: (l, r, 0))],
        out_specs=(pl.BlockSpec((None, tr, P_MAIN), lambda l, r: (l, r, 0)),
                   pl.BlockSpec((None, tr, LANE), lambda l, r: (l, r, 0))),
        name="relayout_w_in",
    )(w_in)


def _head_rows(v, direction):
    return jnp.zeros((1, LANE), F32).at[0, 8 * direction:8 * direction + SSD_HEADS].set(v.astype(F32))


def kernel(x, c, ctx, c_ctx, w_ada, b_ada, g_mix, g_ffn, w_in, w_out, sc_conv_w, cm_w_s, cm_b_s, mb_conv_w,
           mb_conv_b, mb_a_log, mb_dt_bias, mb_d, mb_norm_g, gla_w_gate, gla_b_gate, gla_norm_g, ffn_w_gate,
           ffn_w_up, ffn_w_down, moe_router, moe_w_gate, moe_w_up, moe_w_down, g_final):
    nb, seq, _ = x.shape
    ctx_len = ctx.shape[1]
    depth = w_ada.shape[0]
    assert depth == 2, "supported stack: a dense-FFN layer followed by a last, routed-FFN layer"
    m_lat = nb * seq
    h = x.reshape(m_lat, D)
    hc = ctx.reshape(nb * ctx_len, D)

    cond = jnp.concatenate([c, c_ctx[None, :]], axis=0)
    cb = jnp.broadcast_to(cond[:, :, None], (nb + 1, D, LANE))
    mods = _ada(cb, w_ada, b_ada).reshape(depth, 8, N_MOD, 1, D)

    tm_in = 1024
    lat_row_in = lambda i: i // (seq // tm_in)
    lat_row_512 = lambda i: i // (seq // 512)
    lat_row_g = lambda i: i // (seq // GATHER_R)
    ctx_row = lambda i: nb

    head_of_col = np.arange(GW) // SSD_HD
    expand = [jnp.asarray(np.arange(LANE)[:, None] == 8 * d + head_of_col[None, :], dtype=BF16) for d in range(2)]
    w_in_main, w_in_small = _relayout_w_in(w_in)
    out = None
    for i in range(depth):
        last = i == depth - 1
        modr = mods[i]
        g_mix_i = g_mix[i].reshape(1, D)
        g_ffn_i = g_ffn[i].reshape(1, D)
        p, ps = _inproj(h, g_mix_i, modr, lat_row_in, w_in_main, w_in_small, i, tm_in)
        pc, psc = _inproj(hc, g_mix_i, modr, ctx_row, w_in_main, w_in_small, i, nb * ctx_len)

        ws = cm_w_s[i].astype(BF16)
        bs = jnp.repeat(cm_b_s[i].T, 128, axis=1)
        ya = _scm(p, sc_conv_w[i], ws, bs, GRID_W)
        if not last:
            yac = _scm(pc, sc_conv_w[i], ws, bs, ctx_len)

        conv_b = mb_conv_b[i].reshape(1, -1)
        s_zero = jnp.zeros((nb, SSD_Q, GW), F32)
        skip = jnp.repeat(mb_d[i, 0] + mb_d[i, 1], SSD_HD).reshape(1, GW)
        fin = (skip, mb_norm_g[i].reshape(1, GW))
        ssd_c = [(mb_conv_w[i], conv_b, _head_rows(mb_a_log[i, d], d), _head_rows(mb_dt_bias[i, d], d), expand[d])
                 for d in range(2)]
        y0c, s0c = _ssd(pc, psc, s_zero, ssd_c[0], nb, ctx_len, 0, False)
        y0, _ = _ssd(p, ps, s0c, ssd_c[0], nb, GRID_W, 0, False)
        if last:
            _, s1c = _ssd(pc, psc, s_zero, ssd_c[1], nb, ctx_len, 1, False)
        else:
            ymbc, s1c = _ssd(pc, psc, s_zero, ssd_c[1], nb, ctx_len, 1, True, y0c, fin)
        ymb, _ = _ssd(p, ps, s1c, ssd_c[1], nb, GRID_W, 1, True, y0, fin)

        g_zero = jnp.zeros((nb, GLA_DV, 256), F32)
        gla_c = [(jnp.zeros((LANE, 256), F32).at[16 + 16 * d:32 + 16 * d].set(gla_w_gate[i, d]).astype(BF16),
                  gla_b_gate[i, d].reshape(1, 256)) for d in range(2)]
        ng = gla_norm_g[i].reshape(1, GLA_DV)
        o0c, t0c = _gla(pc, psc, g_zero, gla_c[0], nb, 0, False)
        o0, _ = _gla(p, ps, t0c, gla_c[0], nb, 0, False)
        if last:
            _, t1c = _gla(pc, psc, g_zero, gla_c[1], nb, 1, False)
        else:
            yglac, t1c = _gla(pc, psc, g_zero, gla_c[1], nb, 1, True, o0c, ng)
        ygla = _gla(p, ps, t1c, gla_c[1], nb, 1, True, o0, ng)[0]

        w_out_b = w_out[i].astype(BF16)
        j = i // 2
        if i % 2 == 0:
            hcn, xcn = _outproj(hc, yac, ymbc, yglac, w_out_b, modr, ctx_row, g_ffn_i)
            hc, wg, wu, wd = _ffn(xcn, hcn, modr, ctx_row, ffn_w_gate[j], ffn_w_up[j], ffn_w_down[j],
                                  nb * ctx_len, True)
            hn, xn = _outproj(h, ya, ymb, ygla, w_out_b, modr, lat_row_512, g_ffn_i)
            h = _ffn(xn, hn, modr, lat_row_in, wg, wu, wd, tm_in, False)[0]
        else:
            w_router = jnp.pad(moe_router[j], ((0, 0), (0, LANE - N_EXPERTS)))
            hn, xn, ri, rw, cnt = _outproj(h, ya, ymb, ygla, w_out_b, modr, lat_row_512, g_ffn_i, w_router)
            if last:
                out = _moe(xn, hn, ri, rw, cnt, modr, lat_row_g, g_final.reshape(1, D),
                           moe_w_gate[j], moe_w_up[j], moe_w_down[j])
            else:
                raise NotImplementedError("routed FFN is only implemented as the last layer's channel mixer")
    return out.reshape(nb, seq, D)
```

```python
import functools

import jax
import jax.numpy as jnp
import numpy as np
from jax import lax
from jax.experimental import pallas as pl
from jax.experimental.pallas import tpu as pltpu

F32 = jnp.float32
BF16 = jnp.bfloat16
EPS = 1e-6

D = 2048
GW = D // 4
GRID_W = 64
N_MOD = 6
LANE = 128
BF16_SUBLANES = 16
P_MAIN = 5632
COL_DT, N_DT = 4096, 16
COL_GLR, N_GLR = 5648, 32

SSD_Q = 128
SSD_HEADS = 8
SSD_HD = 64
GLA_C = 64
GLA_HEADS = 4
GLA_DK = 64
GLA_DV = 128
GLA_RANK = 16
N_EXPERTS = 8

FFN_SB = 256
MOE_SB = 272
MOE_TM = 4 * MOE_SB
MOE_TF = 512
MOE_NT = 2 * 8192 // MOE_TM + N_EXPERTS
MOE_ISSUE = 78
MOE_STAGE = 14 * MOE_ISSUE
MOE_UNROLL = 6
MOE_VMEM_LIMIT = 127 * 512 * 1024
GATHER_R = 256


def _silu(x):
    return x * jax.nn.sigmoid(x)


def _softplus(x):
    return jnp.maximum(x, 0.0) + jnp.log1p(jnp.exp(-jnp.abs(x)))


def _split3(x):
    hi = x.astype(BF16)
    r = x - hi.astype(F32)
    mid = r.astype(BF16)
    lo = (r - mid.astype(F32)).astype(BF16)
    return hi, mid, lo


def _dot(a, b):
    return jnp.dot(a, b, preferred_element_type=F32)


def _dot_nt(a, b):
    return lax.dot_general(a, b, (((1,), (1,)), ((), ())), preferred_element_type=F32)


def _dot_tn(a, b):
    return lax.dot_general(a, b, (((0,), (0,)), ((), ())), preferred_element_type=F32)


def _dot01_left(m01, x):
    hi, mid, lo = _split3(x)
    return _dot(m01, hi) + _dot(m01, mid) + _dot(m01, lo)


def _dot01_right(x, m01):
    hi, mid, lo = _split3(x)
    return _dot(hi, m01) + _dot(mid, m01) + _dot(lo, m01)


def _rms(x, g):
    return x * lax.rsqrt(jnp.mean(x * x, axis=-1, keepdims=True) + EPS) * g


def _conv3(x, w, seg):
    n = x.shape[0]
    t = lax.broadcasted_iota(jnp.int32, (n, 1), 0) & (seg - 1)
    prev = jnp.where(t == 0, 0.0, pltpu.roll(x, 1, 0))
    nxt = jnp.where(t == seg - 1, 0.0, pltpu.roll(x, n - 1, 0))
    return prev * w[0:1] + x * w[1:2] + nxt * w[2:3]


def _ada_kernel(cb_ref, w_ref, b_ref, o_ref, s_ref):
    @pl.when((pl.program_id(0) == 0) & (pl.program_id(1) == 0))
    def _():
        s_ref[...] = _silu(cb_ref[...])

    tn = w_ref.shape[1]
    o_ref[...] = jnp.zeros_like(o_ref)
    for j in range(tn // LANE):
        wj = w_ref[:, j * LANE:(j + 1) * LANE]
        for m in range(3):
            o_ref[m:m + 1, j * LANE:(j + 1) * LANE] = (
                jnp.sum(wj * s_ref[m], axis=0, keepdims=True) + b_ref[:, j * LANE:(j + 1) * LANE])


def _ada(cb, w_ada, b_ada):
    depth = w_ada.shape[0]
    n = w_ada.shape[2]
    tn = 1024
    return pl.pallas_call(
        _ada_kernel,
        out_shape=jax.ShapeDtypeStruct((depth, 8, n), F32),
        grid=(depth, n // tn),
        in_specs=[pl.BlockSpec((3, D, LANE), lambda l, j: (0, 0, 0)),
                  pl.BlockSpec((None, D, tn), lambda l, j: (l, 0, j)),
                  pl.BlockSpec((None, 1, tn), lambda l, j: (l, 0, j))],
        out_specs=pl.BlockSpec((None, 8, tn), lambda l, j: (l, 0, j)),
        scratch_shapes=[pltpu.VMEM((3, D, LANE), F32)],
        name="ada",
    )(cb, w_ada, b_ada.reshape(depth, 1, n))


def _inproj_kernel(h_ref, g_ref, sh_ref, sc_ref, w_ref, ws_ref, o_ref, os_ref, xn_ref):
    @pl.when(pl.program_id(1) == 0)
    def _():
        xn = _rms(h_ref[...], g_ref[...])
        xn_ref[...] = (xn * (1.0 + sc_ref[...]) + sh_ref[...]).astype(BF16)
        os_ref[...] = _dot(xn_ref[...], ws_ref[...])

    o_ref[...] = _dot(xn_ref[...], w_ref[...]).astype(BF16)


def _mod_spec(row_fn, k):
    return pl.BlockSpec((None, None, 1, D), lambda i, j: (row_fn(i), k, 0, 0))


def _inproj(h, g, modr, row_fn, w, w_small, layer, tm):
    m = h.shape[0]
    tn = P_MAIN // 4
    return pl.pallas_call(
        _inproj_kernel,
        out_shape=(jax.ShapeDtypeStruct((m, P_MAIN), BF16), jax.ShapeDtypeStruct((m, LANE), F32)),
        grid=(m // tm, P_MAIN // tn),
        in_specs=[pl.BlockSpec((tm, D), lambda i, j: (i, 0)),
                  pl.BlockSpec((1, D), lambda i, j: (0, 0)),
                  _mod_spec(row_fn, 0), _mod_spec(row_fn, 1),
                  pl.BlockSpec((None, D, tn), lambda i, j: (layer, 0, j)),
                  pl.BlockSpec((None, D, LANE), lambda i, j: (layer, 0, 0))],
        out_specs=(pl.BlockSpec((tm, tn), lambda i, j: (i, j)),
                   pl.BlockSpec((tm, LANE), lambda i, j: (i, 0))),
        scratch_shapes=[pltpu.VMEM((tm, D), BF16)],
        name="inproj",
    )(h, g, modr, modr, w, w_small)


def _gelu_tanh(x):
    c = 0.7978845608028654
    return x * (0.5 * (1.0 + jnp.tanh(c * (x + 0.044715 * (x * x * x)))))


def _scm_kernel(sc_ref, u_ref, v_ref, cw_ref, ws_ref, bs_ref, o_ref, *, seg):
    t = sc_ref.shape[0]
    bgate = sc_ref[:, 0:GW].astype(F32)
    gated = sc_ref[:, GW:2 * GW].astype(F32) * sc_ref[:, 2 * GW:3 * GW].astype(F32)
    y_sc = bgate * _conv3(gated, cw_ref[...], seg)
    o_ref[:, 0:GW] = y_sc.astype(BF16)

    u = _gelu_tanh(u_ref[...].astype(F32))
    v = _gelu_tanh(v_ref[...].astype(F32)).astype(BF16)
    for c in range(t // 128):
        rows = slice(c * 128, (c + 1) * 128)
        for g in range(4):
            cols = slice(g * 128, (g + 1) * 128)
            s = _dot(ws_ref[g], v[rows, cols]) + bs_ref[:, cols]
            o_ref[rows, GW + g * 128:GW + (g + 1) * 128] = (u[rows, cols] * s).astype(BF16)


def _scm(p, cw, ws, bs, seg):
    m = p.shape[0]
    t = 512
    return pl.pallas_call(
        functools.partial(_scm_kernel, seg=seg),
        out_shape=jax.ShapeDtypeStruct((m, 2 * GW), BF16),
        grid=(m // t,),
        in_specs=[pl.BlockSpec((t, 3 * GW), lambda i: (i, 0)),
                  pl.BlockSpec((t, GW), lambda i: (i, 3)),
                  pl.BlockSpec((t, GW), lambda i: (i, 4)),
                  pl.BlockSpec((3, GW), lambda i: (0, 0)),
                  pl.BlockSpec((4, 128, 128), lambda i: (0, 0, 0)),
                  pl.BlockSpec((128, GW), lambda i: (0, 0))],
        out_specs=pl.BlockSpec((t, 2 * GW), lambda i: (i, 0)),
        name="scm",
    )(p, p, p, cw, ws, bs)


def _ssd_kernel(*refs, direction, final, seg):
    if final:
        (xbc_ref, sm_ref, cw_ref, cb_ref, alog_ref, dtb_ref, e_ref, s0_ref,
         y0_ref, z_ref, skip_ref, ng_ref, y_ref, sfin_ref, st_ref) = refs
    else:
        (xbc_ref, sm_ref, cw_ref, cb_ref, alog_ref, dtb_ref, e_ref, s0_ref,
         y_ref, sfin_ref, st_ref) = refs
    j = pl.program_id(0)
    nb, tb = xbc_ref.shape[0], xbc_ref.shape[1]
    q = SSD_Q

    @pl.when(j == 0)
    def _():
        st_ref[...] = s0_ref[...]

    a_neg = -jnp.exp(alog_ref[...])
    ii = lax.broadcasted_iota(jnp.int32, (q, q), 0)
    jj = lax.broadcasted_iota(jnp.int32, (q, q), 1)
    mask = (jj <= ii) if direction == 0 else (jj >= ii)
    tri = mask.astype(BF16)
    last = q - 1 if direction == 0 else 0
    lane0 = 8 * direction
    upper_half = lax.broadcasted_iota(jnp.int32, (q, 128), 1) >= SSD_HD
    expand = e_ref[...]

    xcs = [_silu(_conv3(xbc_ref[b].astype(F32), cw_ref[...], seg) + cb_ref[...])
           for b in range(nb)]
    states = [st_ref[b] for b in range(nb)]

    def chunk(b, c, st):
        rows = slice(c * q, (c + 1) * q)
        xc = xcs[b]
        dt = _softplus(sm_ref[b, rows, :] + dtb_ref[...])
        acs = _dot01_left(tri, dt * a_neg)
        tot = acs[last:last + 1, :]
        acs_t = acs.T
        dt_t = dt.T
        wx = _dot01_right(jnp.exp(tot - acs) * dt, expand)
        ea = _dot01_right(jnp.exp(acs), expand)
        xs = xc[rows, 0:GW]
        xw = (wx * xs).astype(BF16)
        stb = st.astype(BF16)
        ys = []
        new_st = []
        for g in range(2):
            bg = xc[rows, GW + g * 128:GW + (g + 1) * 128].astype(BF16)
            cg = xc[rows, GW + 256 + g * 128:GW + 256 + (g + 1) * 128].astype(BF16)
            cb = _dot_nt(cg, bg)
            gcols = slice(g * 256, (g + 1) * 256)
            yint = _dot(cg, stb[:, gcols])
            for pr in range(2):
                pidx = 2 * g + pr
                pcols = slice(pidx * 128, (pidx + 1) * 128)
                xp = xs[:, pcols]
                acc = yint[:, pr * 128:(pr + 1) * 128] * ea[:, pcols]
                for hh in range(2):
                    ln = lane0 + 2 * pidx + hh
                    sg = acs[:, ln:ln + 1] - acs_t[ln:ln + 1, :]
                    decay = jnp.where(mask, jnp.exp(jnp.where(mask, sg, 0.0)), 0.0)
                    scores = (cb * decay * dt_t[ln:ln + 1, :]).astype(BF16)
                    xh = jnp.where(upper_half == (hh == 1), xp, 0.0).astype(BF16)
                    acc = acc + _dot(scores, xh)
                ys.append(acc)
            new_st.append(ea[last:last + 1, gcols] * st[:, gcols] + _dot_tn(bg, xw[:, gcols]))
        y = jnp.concatenate(ys, axis=1)
        if final:
            y = y0_ref[b, rows, :] + y + xs * skip_ref[...]
            y = y * _silu(z_ref[b, rows, :].astype(F32))
            y_ref[b, rows, :] = _rms(y, ng_ref[...]).astype(y_ref.dtype)
        else:
            y_ref[b, rows, :] = y
        return jnp.concatenate(new_st, axis=1)

    nchunk = tb // q
    order = range(nchunk) if direction == 0 else range(nchunk - 1, -1, -1)
    for c in order:
        for b in range(nb):
            states[b] = chunk(b, c, states[b])
    for b in range(nb):
        st_ref[b] = states[b]

    @pl.when(j == pl.num_programs(0) - 1)
    def _():
        sfin_ref[...] = st_ref[...]


def _scan_block_map(nblk, direction):
    return (lambda j: j) if direction == 0 else (lambda j: nblk - 1 - j)


def _ssd(p, ps, s0, consts, nb, seg, direction, final, y0=None, fin=None):
    m = p.shape[0]
    seq = m // nb
    tb = 256
    nblk = seq // tb
    blk = _scan_block_map(nblk, direction)
    p3 = p.reshape(nb, seq, P_MAIN)
    full = lambda shape: pl.BlockSpec(shape, lambda j: (0,) * len(shape))
    rows = lambda width, col: pl.BlockSpec((nb, tb, width), lambda j: (0, blk(j), col))
    in_specs = [rows(1024, 3), rows(LANE, 0),
                full((3, 1024)), full((1, 1024)), full((1, LANE)), full((1, LANE)), full((LANE, GW)),
                full((nb, SSD_Q, GW))]
    args = [p3, ps.reshape(nb, seq, LANE), *consts, s0]
    if final:
        in_specs += [rows(GW, 0), rows(GW, 5), full((1, GW)), full((1, GW))]
        args += [y0.reshape(nb, seq, GW), p3, *fin]
    y, s_fin = pl.pallas_call(
        functools.partial(_ssd_kernel, direction=direction, final=final, seg=seg),
        out_shape=(jax.ShapeDtypeStruct((nb, seq, GW), BF16 if final else F32),
                   jax.ShapeDtypeStruct((nb, SSD_Q, GW), F32)),
        grid=(nblk,),
        in_specs=in_specs,
        out_specs=(rows(GW, 0), full((nb, SSD_Q, GW))),
        scratch_shapes=[pltpu.VMEM((nb, SSD_Q, GW), F32)],
        name=f"ssd_d{direction}",
    )(*args)
    return y.reshape(m, GW), s_fin


def _gla_kernel(*refs, direction, final):
    if final:
        (q_ref, k_ref, v_ref, sm_ref, wg_ref, bg_ref, s0_ref, o0_ref, g_ref, ng_ref,
         o_ref, sfin_ref, st_ref) = refs
    else:
        (q_ref, k_ref, v_ref, sm_ref, wg_ref, bg_ref, s0_ref, o_ref, sfin_ref, st_ref) = refs
    j = pl.program_id(0)
    nb, tb = q_ref.shape[0], q_ref.shape[1]
    c_len = GLA_C

    @pl.when(j == 0)
    def _():
        st_ref[...] = s0_ref[...]

    ii = lax.broadcasted_iota(jnp.int32, (tb, tb), 0)
    jj = lax.broadcasted_iota(jnp.int32, (tb, tb), 1)
    same = (ii >> 6) == (jj >> 6)
    btri = (same & ((jj <= ii) if direction == 0 else (jj >= ii))).astype(BF16)
    gcums = []
    for b in range(nb):
        gk = -_softplus(-(_dot(sm_ref[b].astype(BF16), wg_ref[...]) + bg_ref[...])) * (1.0 / 16.0)
        gcums.append(_dot01_left(btri, gk))

    hs = GLA_HEADS * c_len
    ri = lax.broadcasted_iota(jnp.int32, (hs, hs), 0)
    ci = lax.broadcasted_iota(jnp.int32, (hs, hs), 1)
    own_lanes = (ri >> 6) == (ci >> 6)
    ti, tj = ri & (c_len - 1), ci & (c_len - 1)
    own_causal = own_lanes & ((tj <= ti) if direction == 0 else (tj >= ti))
    last = c_len - 1 if direction == 0 else 0
    ref = c_len // 2 if direction == 0 else c_len - 1 - c_len // 2

    def stack_heads(x):
        return jnp.where(own_lanes, jnp.concatenate([x] * GLA_HEADS, axis=0), 0.0).astype(BF16)

    states = [st_ref[b] for b in range(nb)]

    def chunk(b, c, st):
        rows = slice(c * c_len, (c + 1) * c_len)
        gc = gcums[b][rows, :]
        gl = gc[last:last + 1, :]
        gr = gc[ref:ref + 1, :]
        qc = q_ref[b, rows, :].astype(F32) * (GLA_DK ** -0.5)
        kc = k_ref[b, rows, :].astype(F32)
        vc = v_ref[b, rows, :]
        v_rows = jnp.concatenate([vc[:, h * GLA_DV:(h + 1) * GLA_DV] for h in range(GLA_HEADS)], axis=0)
        kb = (kc * jnp.exp(gr - gc)).astype(BF16)
        att = _dot_nt(stack_heads(qc * jnp.exp(gc - gr)), jnp.concatenate([kb] * GLA_HEADS, axis=0))
        att = jnp.where(own_causal, att, 0.0).astype(BF16)
        y = _dot(att, v_rows) + _dot_nt(stack_heads(qc * jnp.exp(gc)), st.astype(BF16))
        contrib = _dot_tn(v_rows, stack_heads(kc * jnp.exp(gl - gc)))
        new_st = jnp.exp(gl) * st + contrib
        o = jnp.concatenate([y[h * c_len:(h + 1) * c_len, :] for h in range(GLA_HEADS)], axis=1)
        if final:
            o = o0_ref[b, rows, :] + o
            normed = [_rms(o[:, h * GLA_DV:(h + 1) * GLA_DV], ng_ref[...]) for h in range(GLA_HEADS)]
            gate = _silu(g_ref[b, rows, :].astype(F32))
            o_ref[b, rows, :] = (jnp.concatenate(normed, axis=1) * gate).astype(o_ref.dtype)
        else:
            o_ref[b, rows, :] = o
        return new_st

    nchunk = tb // c_len
    order = range(nchunk) if direction == 0 else range(nchunk - 1, -1, -1)
    for c in order:
        for b in range(nb):
            states[b] = chunk(b, c, states[b])
    for b in range(nb):
        st_ref[b] = states[b]

    @pl.when(j == pl.num_programs(0) - 1)
    def _():
        sfin_ref[...] = st_ref[...]


def _gla(p, ps, s0, consts, nb, direction, final, o0=None, ng=None):
    m = p.shape[0]
    seq = m // nb
    tb = 256
    nblk = seq // tb
    blk = _scan_block_map(nblk, direction)
    p3 = p.reshape(nb, seq, P_MAIN)
    full = lambda shape: pl.BlockSpec(shape, lambda j: (0,) * len(shape))
    rows = lambda width, col: pl.BlockSpec((nb, tb, width), lambda j: (0, blk(j), col))
    in_specs = [rows(256, 16), rows(256, 17), rows(GW, 9), rows(LANE, 0),
                full((LANE, 256)), full((1, 256)), full((nb, GLA_DV, 256))]
    args = [p3, p3, p3, ps.reshape(nb, seq, LANE), *consts, s0]
    if final:
        in_specs += [rows(GW, 0), rows(GW, 10), full((1, GLA_DV))]
        args += [o0.reshape(nb, seq, GW), p3, ng]
    o, s_fin = pl.pallas_call(
        functools.partial(_gla_kernel, direction=direction, final=final),
        out_shape=(jax.ShapeDtypeStruct((nb, seq, GW), BF16 if final else F32),
                   jax.ShapeDtypeStruct((nb, GLA_DV, 256), F32)),
        grid=(nblk,),
        in_specs=in_specs,
        out_specs=(rows(GW, 0), full((nb, GLA_DV, 256))),
        scratch_shapes=[pltpu.VMEM((nb, GLA_DV, 256), F32)],
        name=f"gla_d{direction}",
    )(*args)
    return o.reshape(m, GW), s_fin


def _outproj_kernel(*refs, route):
    if route:
        (h_ref, ya_ref, ymb_ref, ygla_ref, w_ref, gate_ref, g_ref, sh_ref, sc_ref, wr_ref,
         hn_ref, xn_ref, ri_ref, rw_ref, cnt_ref, carry_ref, xprev_ref) = refs
    else:
        (h_ref, ya_ref, ymb_ref, ygla_ref, w_ref, gate_ref, g_ref, sh_ref, sc_ref,
         hn_ref, xn_ref) = refs

    def project():
        acc = (_dot(ya_ref[...], w_ref[0:2 * GW, :]) + _dot(ymb_ref[...], w_ref[2 * GW:3 * GW, :])
               + _dot(ygla_ref[...], w_ref[3 * GW:4 * GW, :]))
        hn = h_ref[...] + gate_ref[...] * acc
        hn_ref[...] = hn
        xn = _rms(hn, g_ref[...]) * (1.0 + sc_ref[...]) + sh_ref[...]
        xn_ref[...] = xn.astype(xn_ref.dtype)
        return xn

    if not route:
        project()
        return

    i = pl.program_id(0)
    n_tiles = pl.num_programs(0) - 1

    def route_prev():
        _route_tile(xprev_ref[...], wr_ref, ri_ref, rw_ref, cnt_ref, carry_ref)

    @pl.when(i == 0)
    def _():
        carry_ref[...] = jnp.zeros_like(carry_ref)
        xprev_ref[...] = project()

    @pl.when((i > 0) & (i < n_tiles))
    def _():
        route_prev()
        xprev_ref[...] = project()

    @pl.when(i == n_tiles)
    def _():
        route_prev()


def _route_tile(xn, wr_ref, ri_ref, rw_ref, cnt_ref, carry_ref):
    tm = xn.shape[0]
    xh = xn.astype(BF16)
    xl = (xn - xh.astype(F32)).astype(BF16)
    wr = wr_ref[...]
    wh = wr.astype(BF16)
    wl = (wr - wh.astype(F32)).astype(BF16)
    hh_hl = _dot(xh, jnp.concatenate([wh, wl], axis=1))
    logits = hh_hl[:, 0:LANE] + hh_hl[:, LANE:2 * LANE] + _dot(xl, wh)
    lane = lax.broadcasted_iota(jnp.int32, (tm, LANE), 1)
    lane_f = lane.astype(F32)
    neg = jnp.float32(-3.0e38)
    lm = jnp.where(lane < N_EXPERTS, logits, neg)
    m1 = jnp.max(lm, axis=-1, keepdims=True)
    i1 = jnp.min(jnp.where(lm == m1, lane_f, float(LANE)), axis=-1, keepdims=True)
    lm2 = jnp.where(lane_f == i1, neg, lm)
    m2 = jnp.max(lm2, axis=-1, keepdims=True)
    i2 = jnp.min(jnp.where(lm2 == m2, lane_f, float(LANE)), axis=-1, keepdims=True)
    e = jnp.exp(m2 - m1)
    w1 = 1.0 / (1.0 + e)
    w2 = e / (1.0 + e)
    sel1 = lane_f == i1
    sel2 = lane_f == i2
    onehot = jnp.where(sel1 | sel2, 1.0, 0.0)
    ti = lax.broadcasted_iota(jnp.int32, (tm, tm), 0)
    tj = lax.broadcasted_iota(jnp.int32, (tm, tm), 1)
    incl = _dot((tj <= ti).astype(BF16), onehot.astype(BF16)) + carry_ref[...]
    excl = incl - onehot
    r1 = jnp.sum(jnp.where(sel1, excl, 0.0), axis=-1, keepdims=True)
    r2 = jnp.sum(jnp.where(sel2, excl, 0.0), axis=-1, keepdims=True)
    info = jnp.where(lane == 0, i1, jnp.where(lane == 1, i2, jnp.where(lane == 2, r1, jnp.where(lane == 3, r2, 0.0))))
    ri_ref[...] = info.astype(jnp.int32)
    rw_ref[...] = jnp.where(lane == 0, w1, jnp.where(lane == 1, w2, 0.0))
    carry_ref[...] = incl[tm - 1:tm, :]
    cnt_ref[...] = incl[tm - 1:tm, :]


def _outproj(h, ya, ymb, ygla, w, modr, row_fn, g, w_router=None):
    m = h.shape[0]
    tm = 512
    route = w_router is not None
    n_tiles = m // tm
    cur = (lambda i: jnp.minimum(i, n_tiles - 1)) if route else (lambda i: i)
    prev = lambda i: jnp.maximum(i - 1, 0)
    mod = lambda k: pl.BlockSpec((None, None, 1, D), lambda i: (row_fn(cur(i)), k, 0, 0))
    in_specs = [pl.BlockSpec((tm, D), lambda i: (cur(i), 0)),
                pl.BlockSpec((tm, 2 * GW), lambda i: (cur(i), 0)),
                pl.BlockSpec((tm, GW), lambda i: (cur(i), 0)),
                pl.BlockSpec((tm, GW), lambda i: (cur(i), 0)),
                pl.BlockSpec((D, D), lambda i: (0, 0)),
                mod(2), pl.BlockSpec((1, D), lambda i: (0, 0)), mod(3), mod(4)]
    args = [h, ya, ymb, ygla, w, modr, g, modr, modr]
    out_shape = [jax.ShapeDtypeStruct((m, D), F32), jax.ShapeDtypeStruct((m, D), F32 if route else BF16)]
    out_specs = [pl.BlockSpec((tm, D), lambda i: (cur(i), 0)), pl.BlockSpec((tm, D), lambda i: (cur(i), 0))]
    scratch = []
    if route:
        in_specs.append(pl.BlockSpec((D, LANE), lambda i: (0, 0)))
        args.append(w_router)
        out_shape += [jax.ShapeDtypeStruct((m, LANE), jnp.int32), jax.ShapeDtypeStruct((m, LANE), F32),
                      jax.ShapeDtypeStruct((1, LANE), F32)]
        out_specs += [pl.BlockSpec((tm, LANE), lambda i: (prev(i), 0)), pl.BlockSpec((tm, LANE), lambda i: (prev(i), 0)),
                      pl.BlockSpec((1, LANE), lambda i: (0, 0))]
        scratch = [pltpu.VMEM((1, LANE), F32), pltpu.VMEM((tm, D), F32)]
    return pl.pallas_call(
        functools.partial(_outproj_kernel, route=route),
        out_shape=tuple(out_shape),
        grid=(n_tiles + 1 if route else n_tiles,),
        in_specs=in_specs,
        out_specs=tuple(out_specs),
        scratch_shapes=scratch,
        name="outproj_route" if route else "outproj",
    )(*args)


def _swiglu_rows(x_ref, wg_ref, wu_ref, wd_ref, o_ref, wgb_ref, wub_ref, wdb_ref, sb_rows, nsb, gate,
                 prologue=None):
    def sub_block(sb):
        start = sb * sb_rows
        return pl.ds(start if isinstance(sb, int) else pl.multiple_of(start, BF16_SUBLANES), sb_rows)

    def up(sb):
        x = x_ref[sub_block(sb), :]
        return (_silu(_dot(x, wgb_ref[...])) * _dot(x, wub_ref[...])).astype(BF16)

    def down(sb, hidden):
        rows = sub_block(sb)
        y = _dot(hidden, wdb_ref[...])
        o_ref[rows, :] += y if gate is None else gate * y

    if prologue is not None:
        prologue()
    if wg_ref is not None:
        wgb_ref[...] = wg_ref[...].astype(BF16)
        wub_ref[...] = wu_ref[...].astype(BF16)
    hidden = up(0)
    if wd_ref is not None:
        wdb_ref[...] = wd_ref[...].astype(BF16)
    if isinstance(nsb, int):
        for sb in range(1, nsb):
            nxt = up(sb)
            down(sb - 1, hidden)
            hidden = nxt
    else:
        def body(sb, hid):
            nxt = up(sb)
            down(sb - 1, hid)
            return nxt

        hidden = lax.fori_loop(1, nsb, body, hidden)
    down(nsb - 1, hidden)


def _ffn_kernel(*refs, emit):
    if emit:
        x_ref, h_hbm, gate_ref, wg_ref, wu_ref, wd_ref, o_ref, wgb_ref, wub_ref, wdb_ref = refs
    else:
        x_ref, h_hbm, gate_ref, wgb_ref, wub_ref, wdb_ref, o_ref = refs
        wg_ref = wu_ref = wd_ref = None
    tm = o_ref.shape[0]

    @pl.when(pl.program_id(1) == 0)
    def _():
        pltpu.sync_copy(h_hbm.at[pl.ds(pl.program_id(0) * tm, tm)], o_ref)

    _swiglu_rows(x_ref, wg_ref, wu_ref, wd_ref, o_ref, wgb_ref, wub_ref, wdb_ref, FFN_SB, tm // FFN_SB,
                 gate_ref[...])


def _ffn(xn, h, modr, row_fn, wg, wu, wd, tm, emit):
    m = h.shape[0]
    tf = MOE_TF
    dff = wg.shape[1]
    assert not emit or m == tm
    w_specs = [pl.BlockSpec((D, tf), lambda i, f: (0, f)),
               pl.BlockSpec((D, tf), lambda i, f: (0, f)),
               pl.BlockSpec((tf, D), lambda i, f: (f, 0))]
    out_shape = [jax.ShapeDtypeStruct((m, D), F32)]
    out_specs = [pl.BlockSpec((tm, D), lambda i, f: (i, 0))]
    if emit:
        out_shape += [jax.ShapeDtypeStruct((D, dff), BF16), jax.ShapeDtypeStruct((D, dff), BF16),
                      jax.ShapeDtypeStruct((dff, D), BF16)]
        out_specs += w_specs
    return pl.pallas_call(
        functools.partial(_ffn_kernel, emit=emit),
        out_shape=tuple(out_shape),
        grid=(m // tm, dff // tf),
        in_specs=[pl.BlockSpec((tm, D), lambda i, f: (i, 0)),
                  pl.BlockSpec(memory_space=pl.ANY),
                  pl.BlockSpec((None, None, 1, D), lambda i, f: (row_fn(i), 5, 0, 0))] + w_specs,
        out_specs=tuple(out_specs),
        name="ffn_emit" if emit else "ffn",
    )(xn, h, modr, wg, wu, wd)


def _row_copy(src_hbm, row, dst, r, sem):
    return pltpu.make_async_copy(src_hbm.at[pl.ds(row, 1)], dst.at[pl.ds(r, 1)], sem)


def _expert_kernel(te_ref, nv_ref, nu_ref, src_ref, xn_hbm, wg_ref, wu_ref, wd_ref, o_ref,
                   xb_ref, stage_ref, wgb_ref, wub_ref, wdb_ref, sem):
    i = pl.program_id(0)
    f = pl.program_id(1)
    n_tiles = pl.num_programs(0)
    nf = pl.num_programs(1)
    used = i < nu_ref[0]

    def wait_stage():
        def wait(r, carry):
            _row_copy(xn_hbm, 0, stage_ref, r, sem).wait()
            return carry

        lax.fori_loop(0, MOE_STAGE, wait, 0, unroll=MOE_UNROLL)

    @pl.when(f == 0)
    def _():
        @pl.when(i == 0)
        def _():
            def issue(r, carry):
                _row_copy(xn_hbm, src_ref[r], stage_ref, r, sem).start()
                return carry

            lax.fori_loop(0, MOE_STAGE, issue, 0, unroll=MOE_UNROLL)

        @pl.when(i <= nu_ref[0])
        def _():
            wait_stage()
            xb_ref[...] = stage_ref[0:MOE_TM, :].astype(BF16)

    def issue_next_rows():
        base = (i + 1) * MOE_TM + f * MOE_ISSUE
        for k in range(MOE_ISSUE):
            _row_copy(xn_hbm, src_ref[base + k], stage_ref, f * MOE_ISSUE + k, sem).start()

    @pl.when(used)
    def _():
        @pl.when(f == 0)
        def _():
            o_ref[...] = jnp.zeros_like(o_ref)

        nsb = (nv_ref[i] + MOE_SB - 1) // MOE_SB
        refs = (xb_ref, wg_ref, wu_ref, wd_ref, o_ref, wgb_ref, wub_ref, wdb_ref)

        @pl.when(nsb == MOE_TM // MOE_SB)
        def _():
            _swiglu_rows(*refs, MOE_SB, MOE_TM // MOE_SB, None, issue_next_rows)

        @pl.when(nsb < MOE_TM // MOE_SB)
        def _():
            _swiglu_rows(*refs, MOE_SB, nsb, None, issue_next_rows)

    @pl.when(jnp.logical_not(used) & (f == 0))
    def _():
        o_ref[...] = jnp.zeros_like(o_ref)

    @pl.when(used & (i == n_tiles - 1) & (f == nf - 1))
    def _():
        wait_stage()


def _experts(xn, src, te, nv, nu, wg, wu, wd):
    dff = wg.shape[2]
    nf = dff // MOE_TF
    assert nf * MOE_ISSUE == MOE_STAGE and MOE_STAGE >= MOE_TM

    def f_eff(i, f, nu):
        return jnp.where(i < nu[0], f, nf - 1)

    return pl.pallas_call(
        _expert_kernel,
        out_shape=jax.ShapeDtypeStruct((MOE_NT * MOE_TM, D), F32),
        grid_spec=pltpu.PrefetchScalarGridSpec(
            num_scalar_prefetch=4,
            grid=(MOE_NT, nf),
            in_specs=[pl.BlockSpec(memory_space=pl.ANY),
                      pl.BlockSpec((None, D, MOE_TF), lambda i, f, te, nv, nu, src: (te[i], 0, f_eff(i, f, nu))),
                      pl.BlockSpec((None, D, MOE_TF), lambda i, f, te, nv, nu, src: (te[i], 0, f_eff(i, f, nu))),
                      pl.BlockSpec((None, MOE_TF, D), lambda i, f, te, nv, nu, src: (te[i], f_eff(i, f, nu), 0))],
            out_specs=pl.BlockSpec((MOE_TM, D), lambda i, f, te, nv, nu, src: (i, 0)),
            scratch_shapes=[pltpu.VMEM((MOE_TM, D), BF16), pltpu.VMEM((MOE_STAGE, D), F32),
                            pltpu.VMEM((D, MOE_TF), BF16), pltpu.VMEM((D, MOE_TF), BF16),
                            pltpu.VMEM((MOE_TF, D), BF16), pltpu.SemaphoreType.DMA(())]),
        compiler_params=pltpu.CompilerParams(vmem_limit_bytes=MOE_VMEM_LIMIT),
        name="moe_experts",
    )(te, nv, nu, src, xn, wg, wu, wd)


def _combine_kernel(pos_ref, h_ref, rw_ref, gate_ref, gfin_ref, y_hbm, o_ref, buf_ref, sem):
    i = pl.program_id(0)
    r_blk = h_ref.shape[0]
    slot = i % 2

    def issue_tile(tile, s):
        def issue(r, carry):
            t = tile * r_blk + r
            _row_copy(y_hbm, pos_ref[2 * t], buf_ref.at[s, 0], r, sem.at[s]).start()
            _row_copy(y_hbm, pos_ref[2 * t + 1], buf_ref.at[s, 1], r, sem.at[s]).start()
            return carry

        lax.fori_loop(0, r_blk, issue, 0, unroll=8)

    @pl.when(i == 0)
    def _():
        issue_tile(0, 0)

    @pl.when(i + 1 < pl.num_programs(0))
    def _():
        issue_tile(i + 1, 1 - slot)

    def wait(r, carry):
        _row_copy(y_hbm, 0, buf_ref.at[slot, 0], r, sem.at[slot]).wait()
        _row_copy(y_hbm, 0, buf_ref.at[slot, 1], r, sem.at[slot]).wait()
        return carry

    lax.fori_loop(0, r_blk, wait, 0, unroll=8)
    y = rw_ref[:, 0:1] * buf_ref[slot, 0] + rw_ref[:, 1:2] * buf_ref[slot, 1]
    hn = h_ref[...] + gate_ref[...] * y
    o_ref[...] = _rms(hn, gfin_ref[...])


def _combine(pos, h, rw, modr, row_fn, gfin, ys):
    m = h.shape[0]
    r = GATHER_R
    return pl.pallas_call(
        _combine_kernel,
        out_shape=jax.ShapeDtypeStruct((m, D), F32),
        grid_spec=pltpu.PrefetchScalarGridSpec(
            num_scalar_prefetch=1,
            grid=(m // r,),
            in_specs=[pl.BlockSpec((r, D), lambda i, pos: (i, 0)),
                      pl.BlockSpec((r, LANE), lambda i, pos: (i, 0)),
                      pl.BlockSpec((None, None, 1, D), lambda i, pos: (row_fn(i), 5, 0, 0)),
                      pl.BlockSpec((1, D), lambda i, pos: (0, 0)),
                      pl.BlockSpec(memory_space=pl.ANY)],
            out_specs=pl.BlockSpec((r, D), lambda i, pos: (i, 0)),
            scratch_shapes=[pltpu.VMEM((2, 2, r, D), F32), pltpu.SemaphoreType.DMA((2,))]),
        name="moe_combine",
    )(pos, h, rw, modr, gfin, ys)


def _moe(xn, h, ri, rw, cnt, modr, row_fn, gfin, wg, wu, wd):
    m = h.shape[0]
    tm = MOE_TM
    nt = MOE_NT
    counts = cnt[0, :N_EXPERTS].astype(jnp.int32)
    nt_e = (counts + tm - 1) // tm
    t_end = jnp.cumsum(nt_e)
    t_start = t_end - nt_e
    n_used = t_end[-1]
    tid = jnp.arange(nt, dtype=jnp.int32)
    te_raw = jnp.sum((tid[:, None] >= t_end[None, :]).astype(jnp.int32), axis=1)
    te_last = jnp.sum((n_used - 1 >= t_end).astype(jnp.int32))
    te = jnp.minimum(te_raw, te_last).astype(jnp.int32)
    nv = jnp.where(tid < n_used, jnp.clip(counts[te] - (tid - t_start[te]) * tm, 0, tm), 0).astype(jnp.int32)
    pos = (t_start[ri[:, 0:2]] * tm + ri[:, 2:4]).astype(jnp.int32)
    tok = jnp.repeat(jnp.arange(m, dtype=jnp.int32), 2)
    src = jnp.zeros(((nt + 2) * tm,), jnp.int32).at[pos.reshape(-1)].set(tok)
    nu = n_used.reshape(1).astype(jnp.int32)
    ys = _experts(xn, src, te, nv, nu, wg, wu, wd)
    return _combine(pos.reshape(-1), h, rw, modr, row_fn, gfin, ys)


def _relayout_main_kernel(wt_ref, main_ref):
    main_ref[...] = wt_ref[0].T.astype(BF16)


def _relayout_small_kernel(dt_ref, glr_ref, small_ref, buf_ref):
    buf_ref[...] = jnp.zeros_like(buf_ref)
    buf_ref[0:N_DT, :] = dt_ref[0]
    buf_ref[N_DT:N_DT + N_GLR, :] = glr_ref[0]
    small_ref[...] = buf_ref[...].T.astype(BF16)


def _relayout_w_in(w_in):
    depth, rows, cols = w_in.shape
    wt = jnp.swapaxes(w_in, 1, 2)
    tc = 256
    shifted = COL_DT // tc
    assert P_MAIN // tc <= 2 * shifted
    src_row = lambda k: pl.multiple_of(k * tc + N_DT * (k // shifted), N_DT)
    main = pl.pallas_call(
        _relayout_main_kernel,
        out_shape=jax.ShapeDtypeStruct((depth, rows, P_MAIN), BF16),
        grid=(depth, P_MAIN // tc),
        in_specs=[pl.BlockSpec((pl.Element(1), pl.Element(tc), pl.Element(rows)), lambda l, k: (l, src_row(k), 0))],
        out_specs=pl.BlockSpec((None, rows, tc), lambda l, k: (l, 0, k)),
        name="relayout_w_in",
    )(wt)
    small = pl.pallas_call(
        _relayout_small_kernel,
        out_shape=jax.ShapeDtypeStruct((depth, rows, LANE), BF16),
        grid=(depth,),
        in_specs=[pl.BlockSpec((pl.Element(1), pl.Element(N_DT), pl.Element(rows)), lambda l: (l, COL_DT, 0)),
                  pl.BlockSpec((pl.Element(1), pl.Element(N_GLR), pl.Element(rows)), lambda l: (l, COL_GLR, 0))],
        out_specs=pl.BlockSpec((None, rows, LANE), lambda l: (l, 0, 0)),
        scratch_shapes=[pltpu.VMEM((LANE, rows), F32)],
        name="relayout_w_small",
    )(wt, wt)
    return main, small


def _head_rows(v, direction):
    return jnp.zeros((1, LANE), F32).at[0, 8 * direction:8 * direction + SSD_HEADS].set(v.astype(F32))


def kernel(x, c, ctx, c_ctx, w_ada, b_ada, g_mix, g_ffn, w_in, w_out, sc_conv_w, cm_w_s, cm_b_s, mb_conv_w,
           mb_conv_b, mb_a_log, mb_dt_bias, mb_d, mb_norm_g, gla_w_gate, gla_b_gate, gla_norm_g, ffn_w_gate,
           ffn_w_up, ffn_w_down, moe_router, moe_w_gate, moe_w_up, moe_w_down, g_final):
    nb, seq, _ = x.shape
    ctx_len = ctx.shape[1]
    depth = w_ada.shape[0]
    assert depth == 2, "supported stack: a dense-FFN layer followed by a last, routed-FFN layer"
    m_lat = nb * seq
    h = x.reshape(m_lat, D)
    hc = ctx.reshape(nb * ctx_len, D)

    cond = jnp.concatenate([c, c_ctx[None, :]], axis=0)
    cb = jnp.broadcast_to(cond[:, :, None], (nb + 1, D, LANE))
    mods = _ada(cb, w_ada, b_ada).reshape(depth, 8, N_MOD, 1, D)

    tm_in = 1024
    lat_row_in = lambda i: i // (seq // tm_in)
    lat_row_512 = lambda i: i // (seq // 512)
    lat_row_g = lambda i: i // (seq // GATHER_R)
    ctx_row = lambda i: nb

    head_of_col = np.arange(GW) // SSD_HD
    expand = [jnp.asarray(np.arange(LANE)[:, None] == 8 * d + head_of_col[None, :], dtype=BF16) for d in range(2)]
    w_in_main, w_in_small = _relayout_w_in(w_in)
    out = None
    for i in range(depth):
        last = i == depth - 1
        modr = mods[i]
        g_mix_i = g_mix[i].reshape(1, D)
        g_ffn_i = g_ffn[i].reshape(1, D)
        p, ps = _inproj(h, g_mix_i, modr, lat_row_in, w_in_main, w_in_small, i, tm_in)
        pc, psc = _inproj(hc, g_mix_i, modr, ctx_row, w_in_main, w_in_small, i, nb * ctx_len)

        ws = cm_w_s[i].astype(BF16)
        bs = jnp.repeat(cm_b_s[i].T, 128, axis=1)
        ya = _scm(p, sc_conv_w[i], ws, bs, GRID_W)
        if not last:
            yac = _scm(pc, sc_conv_w[i], ws, bs, ctx_len)

        conv_b = mb_conv_b[i].reshape(1, -1)
        s_zero = jnp.zeros((nb, SSD_Q, GW), F32)
        skip = jnp.repeat(mb_d[i, 0] + mb_d[i, 1], SSD_HD).reshape(1, GW)
        fin = (skip, mb_norm_g[i].reshape(1, GW))
        ssd_c = [(mb_conv_w[i], conv_b, _head_rows(mb_a_log[i, d], d), _head_rows(mb_dt_bias[i, d], d), expand[d])
                 for d in range(2)]
        y0c, s0c = _ssd(pc, psc, s_zero, ssd_c[0], nb, ctx_len, 0, False)
        y0, _ = _ssd(p, ps, s0c, ssd_c[0], nb, GRID_W, 0, False)
        if last:
            _, s1c = _ssd(pc, psc, s_zero, ssd_c[1], nb, ctx_len, 1, False)
        else:
            ymbc, s1c = _ssd(pc, psc, s_zero, ssd_c[1], nb, ctx_len, 1, True, y0c, fin)
        ymb, _ = _ssd(p, ps, s1c, ssd_c[1], nb, GRID_W, 1, True, y0, fin)

        g_zero = jnp.zeros((nb, GLA_DV, 256), F32)
        gla_c = [(jnp.zeros((LANE, 256), F32).at[16 + 16 * d:32 + 16 * d].set(gla_w_gate[i, d]).astype(BF16),
                  gla_b_gate[i, d].reshape(1, 256)) for d in range(2)]
        ng = gla_norm_g[i].reshape(1, GLA_DV)
        o0c, t0c = _gla(pc, psc, g_zero, gla_c[0], nb, 0, False)
        o0, _ = _gla(p, ps, t0c, gla_c[0], nb, 0, False)
        if last:
            _, t1c = _gla(pc, psc, g_zero, gla_c[1], nb, 1, False)
        else:
            yglac, t1c = _gla(pc, psc, g_zero, gla_c[1], nb, 1, True, o0c, ng)
        ygla = _gla(p, ps, t1c, gla_c[1], nb, 1, True, o0, ng)[0]

        w_out_b = w_out[i].astype(BF16)
        j = i // 2
        if i % 2 == 0:
            hcn, xcn = _outproj(hc, yac, ymbc, yglac, w_out_b, modr, ctx_row, g_ffn_i)
            hc, wg, wu, wd = _ffn(xcn, hcn, modr, ctx_row, ffn_w_gate[j], ffn_w_up[j], ffn_w_down[j],
                                  nb * ctx_len, True)
            hn, xn = _outproj(h, ya, ymb, ygla, w_out_b, modr, lat_row_512, g_ffn_i)
            h = _ffn(xn, hn, modr, lat_row_in, wg, wu, wd, tm_in, False)[0]
        else:
            w_router = jnp.pad(moe_router[j], ((0, 0), (0, LANE - N_EXPERTS)))
            hn, xn, ri, rw, cnt = _outproj(h, ya, ymb, ygla, w_out_b, modr, lat_row_512, g_ffn_i, w_router)
            if last:
                out = _moe(xn, hn, ri, rw, cnt, modr, lat_row_g, g_final.reshape(1, D),
                           moe_w_gate[j], moe_w_up[j], moe_w_down[j])
            else:
                raise NotImplementedError("routed FFN is only implemented as the last layer's channel mixer")
    return out.reshape(nb, seq, D)
```

```python
import functools
import math

import jax
import jax.numpy as jnp
import numpy as np
from jax import lax
from jax.experimental import pallas as pl
from jax.experimental.pallas import tpu as pltpu

F32 = jnp.float32
BF16 = jnp.bfloat16
EPS = 1e-6

D = 2048
GW = D // 4
GRID_W = 64
N_MOD = 6
LANE = 128
BF16_SUBLANES = 16
P_MAIN = 5632
COL_DT, N_DT = 4096, 16
COL_GLR, N_GLR = 5648, 32

SSD_Q = 128
SSD_HEADS = 8
SSD_HD = 64
GLA_C = 64
GLA_HEADS = 4
GLA_DK = 64
GLA_DV = 128
GLA_RANK = 16
N_EXPERTS = 8

FFN_SB = 256
MOE_SB = 272
MOE_TM = 4 * MOE_SB
MOE_TF = 512
MOE_ISSUE = 78
MOE_VMEM_LIMIT = 127 * 512 * 1024
GATHER_R = 256

ADA_TN = 1024
IN_TM = 1024
SCM_T = 512
SCAN_TB = 256
OUT_TM = 512
RELAYOUT_TC = 256


def _silu(x):
    return x * jax.nn.sigmoid(x)


def _softplus(x):
    return jnp.maximum(x, 0.0) + jnp.log1p(jnp.exp(-jnp.abs(x)))


def _split3(x):
    hi = x.astype(BF16)
    r = x - hi.astype(F32)
    mid = r.astype(BF16)
    lo = (r - mid.astype(F32)).astype(BF16)
    return hi, mid, lo


def _dot(a, b):
    return jnp.dot(a, b, preferred_element_type=F32)


def _dot_nt(a, b):
    return lax.dot_general(a, b, (((1,), (1,)), ((), ())), preferred_element_type=F32)


def _dot_tn(a, b):
    return lax.dot_general(a, b, (((0,), (0,)), ((), ())), preferred_element_type=F32)


def _dot01_left(m01, x):
    hi, mid, lo = _split3(x)
    return _dot(m01, hi) + _dot(m01, mid) + _dot(m01, lo)


def _dot01_right(x, m01):
    hi, mid, lo = _split3(x)
    return _dot(hi, m01) + _dot(mid, m01) + _dot(lo, m01)


def _rms(x, g):
    return x * lax.rsqrt(jnp.mean(x * x, axis=-1, keepdims=True) + EPS) * g


def _conv3(x, w, seg):
    n = x.shape[0]
    t = lax.broadcasted_iota(jnp.int32, (n, 1), 0) & (seg - 1)
    prev = jnp.where(t == 0, 0.0, pltpu.roll(x, 1, 0))
    nxt = jnp.where(t == seg - 1, 0.0, pltpu.roll(x, n - 1, 0))
    return prev * w[0:1] + x * w[1:2] + nxt * w[2:3]


def _ada_kernel(cb_ref, w_ref, b_ref, o_ref, s_ref):
    @pl.when((pl.program_id(0) == 0) & (pl.program_id(1) == 0))
    def _():
        s_ref[...] = _silu(cb_ref[...])

    tn = w_ref.shape[1]
    o_ref[...] = jnp.zeros_like(o_ref)
    for j in range(tn // LANE):
        wj = w_ref[:, j * LANE:(j + 1) * LANE]
        for m in range(3):
            o_ref[m:m + 1, j * LANE:(j + 1) * LANE] = (
                jnp.sum(wj * s_ref[m], axis=0, keepdims=True) + b_ref[:, j * LANE:(j + 1) * LANE])


def _ada(cb, w_ada, b_ada):
    depth = w_ada.shape[0]
    n = w_ada.shape[2]
    tn = ADA_TN
    return pl.pallas_call(
        _ada_kernel,
        out_shape=jax.ShapeDtypeStruct((depth, 8, n), F32),
        grid=(depth, n // tn),
        in_specs=[pl.BlockSpec((3, D, LANE), lambda l, j: (0, 0, 0)),
                  pl.BlockSpec((None, D, tn), lambda l, j: (l, 0, j)),
                  pl.BlockSpec((None, 1, tn), lambda l, j: (l, 0, j))],
        out_specs=pl.BlockSpec((None, 8, tn), lambda l, j: (l, 0, j)),
        scratch_shapes=[pltpu.VMEM((3, D, LANE), F32)],
        name="ada",
    )(cb, w_ada, b_ada.reshape(depth, 1, n))


def _inproj_kernel(h_ref, g_ref, sh_ref, sc_ref, w_ref, ws_ref, o_ref, os_ref, xn_ref):
    @pl.when(pl.program_id(1) == 0)
    def _():
        xn = _rms(h_ref[...], g_ref[...])
        xn_ref[...] = (xn * (1.0 + sc_ref[...]) + sh_ref[...]).astype(BF16)
        os_ref[...] = _dot(xn_ref[...], ws_ref[...])

    o_ref[...] = _dot(xn_ref[...], w_ref[...]).astype(BF16)


def _mod_spec(row_fn, k):
    return pl.BlockSpec((None, None, 1, D), lambda i, j: (row_fn(i), k, 0, 0))


def _inproj(h, g, modr, row_fn, w, w_small, layer, tm):
    m = h.shape[0]
    tn = P_MAIN // 4
    return pl.pallas_call(
        _inproj_kernel,
        out_shape=(jax.ShapeDtypeStruct((m, P_MAIN), BF16), jax.ShapeDtypeStruct((m, LANE), F32)),
        grid=(m // tm, P_MAIN // tn),
        in_specs=[pl.BlockSpec((tm, D), lambda i, j: (i, 0)),
                  pl.BlockSpec((1, D), lambda i, j: (0, 0)),
                  _mod_spec(row_fn, 0), _mod_spec(row_fn, 1),
                  pl.BlockSpec((None, D, tn), lambda i, j: (layer, 0, j)),
                  pl.BlockSpec((None, D, LANE), lambda i, j: (layer, 0, 0))],
        out_specs=(pl.BlockSpec((tm, tn), lambda i, j: (i, j)),
                   pl.BlockSpec((tm, LANE), lambda i, j: (i, 0))),
        scratch_shapes=[pltpu.VMEM((tm, D), BF16)],
        name="inproj",
    )(h, g, modr, modr, w, w_small)


def _gelu_tanh(x):
    c = 0.7978845608028654
    return x * (0.5 * (1.0 + jnp.tanh(c * (x + 0.044715 * (x * x * x)))))


def _scm_kernel(sc_ref, u_ref, v_ref, cw_ref, ws_ref, bs_ref, o_ref, *, seg):
    t = sc_ref.shape[0]
    bgate = sc_ref[:, 0:GW].astype(F32)
    gated = sc_ref[:, GW:2 * GW].astype(F32) * sc_ref[:, 2 * GW:3 * GW].astype(F32)
    y_sc = bgate * _conv3(gated, cw_ref[...], seg)
    o_ref[:, 0:GW] = y_sc.astype(BF16)

    u = _gelu_tanh(u_ref[...].astype(F32))
    v = _gelu_tanh(v_ref[...].astype(F32)).astype(BF16)
    for c in range(t // 128):
        rows = slice(c * 128, (c + 1) * 128)
        for g in range(4):
            cols = slice(g * 128, (g + 1) * 128)
            s = _dot(ws_ref[g], v[rows, cols]) + bs_ref[:, cols]
            o_ref[rows, GW + g * 128:GW + (g + 1) * 128] = (u[rows, cols] * s).astype(BF16)


def _scm(p, cw, ws, bs, seg):
    m = p.shape[0]
    t = SCM_T
    return pl.pallas_call(
        functools.partial(_scm_kernel, seg=seg),
        out_shape=jax.ShapeDtypeStruct((m, 2 * GW), BF16),
        grid=(m // t,),
        in_specs=[pl.BlockSpec((t, 3 * GW), lambda i: (i, 0)),
                  pl.BlockSpec((t, GW), lambda i: (i, 3)),
                  pl.BlockSpec((t, GW), lambda i: (i, 4)),
                  pl.BlockSpec((3, GW), lambda i: (0, 0)),
                  pl.BlockSpec((4, 128, 128), lambda i: (0, 0, 0)),
                  pl.BlockSpec((128, GW), lambda i: (0, 0))],
        out_specs=pl.BlockSpec((t, 2 * GW), lambda i: (i, 0)),
        name="scm",
    )(p, p, p, cw, ws, bs)


def _ssd_kernel(*refs, direction, final, seg):
    if final:
        (xbc_ref, sm_ref, cw_ref, cb_ref, alog_ref, dtb_ref, e_ref, s0_ref,
         y0_ref, z_ref, skip_ref, ng_ref, y_ref, sfin_ref, st_ref) = refs
    else:
        (xbc_ref, sm_ref, cw_ref, cb_ref, alog_ref, dtb_ref, e_ref, s0_ref,
         y_ref, sfin_ref, st_ref) = refs
    j = pl.program_id(0)
    nb, tb = xbc_ref.shape[0], xbc_ref.shape[1]
    q = SSD_Q

    @pl.when(j == 0)
    def _():
        st_ref[...] = s0_ref[...]

    a_neg = -jnp.exp(alog_ref[...])
    ii = lax.broadcasted_iota(jnp.int32, (q, q), 0)
    jj = lax.broadcasted_iota(jnp.int32, (q, q), 1)
    mask = (jj <= ii) if direction == 0 else (jj >= ii)
    tri = mask.astype(BF16)
    last = q - 1 if direction == 0 else 0
    lane0 = 8 * direction
    upper_half = lax.broadcasted_iota(jnp.int32, (q, 128), 1) >= SSD_HD
    expand = e_ref[...]

    xcs = [_silu(_conv3(xbc_ref[b].astype(F32), cw_ref[...], seg) + cb_ref[...])
           for b in range(nb)]
    states = [st_ref[b] for b in range(nb)]

    def chunk(b, c, st):
        rows = slice(c * q, (c + 1) * q)
        xc = xcs[b]
        dt = _softplus(sm_ref[b, rows, :] + dtb_ref[...])
        acs = _dot01_left(tri, dt * a_neg)
        tot = acs[last:last + 1, :]
        acs_t = acs.T
        dt_t = dt.T
        wx = _dot01_right(jnp.exp(tot - acs) * dt, expand)
        ea = _dot01_right(jnp.exp(acs), expand)
        xs = xc[rows, 0:GW]
        xw = (wx * xs).astype(BF16)
        stb = st.astype(BF16)
        ys = []
        new_st = []
        for g in range(2):
            bg = xc[rows, GW + g * 128:GW + (g + 1) * 128].astype(BF16)
            cg = xc[rows, GW + 256 + g * 128:GW + 256 + (g + 1) * 128].astype(BF16)
            cb = _dot_nt(cg, bg)
            gcols = slice(g * 256, (g + 1) * 256)
            yint = _dot(cg, stb[:, gcols])
            for pr in range(2):
                pidx = 2 * g + pr
                pcols = slice(pidx * 128, (pidx + 1) * 128)
                xp = xs[:, pcols]
                acc = yint[:, pr * 128:(pr + 1) * 128] * ea[:, pcols]
                for hh in range(2):
                    ln = lane0 + 2 * pidx + hh
                    sg = acs[:, ln:ln + 1] - acs_t[ln:ln + 1, :]
                    decay = jnp.where(mask, jnp.exp(jnp.where(mask, sg, 0.0)), 0.0)
                    scores = (cb * decay * dt_t[ln:ln + 1, :]).astype(BF16)
                    xh = jnp.where(upper_half == (hh == 1), xp, 0.0).astype(BF16)
                    acc = acc + _dot(scores, xh)
                ys.append(acc)
            new_st.append(ea[last:last + 1, gcols] * st[:, gcols] + _dot_tn(bg, xw[:, gcols]))
        y = jnp.concatenate(ys, axis=1)
        if final:
            y = y0_ref[b, rows, :] + y + xs * skip_ref[...]
            y = y * _silu(z_ref[b, rows, :].astype(F32))
            y_ref[b, rows, :] = _rms(y, ng_ref[...]).astype(y_ref.dtype)
        else:
            y_ref[b, rows, :] = y
        return jnp.concatenate(new_st, axis=1)

    nchunk = tb // q
    order = range(nchunk) if direction == 0 else range(nchunk - 1, -1, -1)
    for c in order:
        for b in range(nb):
            states[b] = chunk(b, c, states[b])
    for b in range(nb):
        st_ref[b] = states[b]

    @pl.when(j == pl.num_programs(0) - 1)
    def _():
        sfin_ref[...] = st_ref[...]


def _scan_block_map(nblk, direction):
    return (lambda j: j) if direction == 0 else (lambda j: nblk - 1 - j)


def _ssd(p, ps, s0, consts, nb, seg, direction, final, y0=None, fin=None):
    m = p.shape[0]
    seq = m // nb
    tb = SCAN_TB
    nblk = seq // tb
    blk = _scan_block_map(nblk, direction)
    p3 = p.reshape(nb, seq, P_MAIN)
    full = lambda shape: pl.BlockSpec(shape, lambda j: (0,) * len(shape))
    rows = lambda width, col: pl.BlockSpec((nb, tb, width), lambda j: (0, blk(j), col))
    in_specs = [rows(1024, 3), rows(LANE, 0),
                full((3, 1024)), full((1, 1024)), full((1, LANE)), full((1, LANE)), full((LANE, GW)),
                full((nb, SSD_Q, GW))]
    args = [p3, ps.reshape(nb, seq, LANE), *consts, s0]
    if final:
        in_specs += [rows(GW, 0), rows(GW, 5), full((1, GW)), full((1, GW))]
        args += [y0.reshape(nb, seq, GW), p3, *fin]
    y, s_fin = pl.pallas_call(
        functools.partial(_ssd_kernel, direction=direction, final=final, seg=seg),
        out_shape=(jax.ShapeDtypeStruct((nb, seq, GW), BF16 if final else F32),
                   jax.ShapeDtypeStruct((nb, SSD_Q, GW), F32)),
        grid=(nblk,),
        in_specs=in_specs,
        out_specs=(rows(GW, 0), full((nb, SSD_Q, GW))),
        scratch_shapes=[pltpu.VMEM((nb, SSD_Q, GW), F32)],
        name=f"ssd_d{direction}",
    )(*args)
    return y.reshape(m, GW), s_fin


def _gla_kernel(*refs, direction, final):
    if final:
        (q_ref, k_ref, v_ref, sm_ref, wg_ref, bg_ref, s0_ref, o0_ref, g_ref, ng_ref,
         o_ref, sfin_ref, st_ref) = refs
    else:
        (q_ref, k_ref, v_ref, sm_ref, wg_ref, bg_ref, s0_ref, o_ref, sfin_ref, st_ref) = refs
    j = pl.program_id(0)
    nb, tb = q_ref.shape[0], q_ref.shape[1]
    c_len = GLA_C

    @pl.when(j == 0)
    def _():
        st_ref[...] = s0_ref[...]

    ii = lax.broadcasted_iota(jnp.int32, (tb, tb), 0)
    jj = lax.broadcasted_iota(jnp.int32, (tb, tb), 1)
    same = (ii >> 6) == (jj >> 6)
    btri = (same & ((jj <= ii) if direction == 0 else (jj >= ii))).astype(BF16)
    gcums = []
    for b in range(nb):
        gk = -_softplus(-(_dot(sm_ref[b].astype(BF16), wg_ref[...]) + bg_ref[...])) * (1.0 / 16.0)
        gcums.append(_dot01_left(btri, gk))

    hs = GLA_HEADS * c_len
    ri = lax.broadcasted_iota(jnp.int32, (hs, hs), 0)
    ci = lax.broadcasted_iota(jnp.int32, (hs, hs), 1)
    own_lanes = (ri >> 6) == (ci >> 6)
    ti, tj = ri & (c_len - 1), ci & (c_len - 1)
    own_causal = own_lanes & ((tj <= ti) if direction == 0 else (tj >= ti))
    last = c_len - 1 if direction == 0 else 0
    ref = c_len // 2 if direction == 0 else c_len - 1 - c_len // 2

    def stack_heads(x):
        return jnp.where(own_lanes, jnp.concatenate([x] * GLA_HEADS, axis=0), 0.0).astype(BF16)

    states = [st_ref[b] for b in range(nb)]

    def chunk(b, c, st):
        rows = slice(c * c_len, (c + 1) * c_len)
        gc = gcums[b][rows, :]
        gl = gc[last:last + 1, :]
        gr = gc[ref:ref + 1, :]
        qc = q_ref[b, rows, :].astype(F32) * (GLA_DK ** -0.5)
        kc = k_ref[b, rows, :].astype(F32)
        vc = v_ref[b, rows, :]
        v_rows = jnp.concatenate([vc[:, h * GLA_DV:(h + 1) * GLA_DV] for h in range(GLA_HEADS)], axis=0)
        kb = (kc * jnp.exp(gr - gc)).astype(BF16)
        att = _dot_nt(stack_heads(qc * jnp.exp(gc - gr)), jnp.concatenate([kb] * GLA_HEADS, axis=0))
        att = jnp.where(own_causal, att, 0.0).astype(BF16)
        y = _dot(att, v_rows) + _dot_nt(stack_heads(qc * jnp.exp(gc)), st.astype(BF16))
        contrib = _dot_tn(v_rows, stack_heads(kc * jnp.exp(gl - gc)))
        new_st = jnp.exp(gl) * st + contrib
        o = jnp.concatenate([y[h * c_len:(h + 1) * c_len, :] for h in range(GLA_HEADS)], axis=1)
        if final:
            o = o0_ref[b, rows, :] + o
            normed = [_rms(o[:, h * GLA_DV:(h + 1) * GLA_DV], ng_ref[...]) for h in range(GLA_HEADS)]
            gate = _silu(g_ref[b, rows, :].astype(F32))
            o_ref[b, rows, :] = (jnp.concatenate(normed, axis=1) * gate).astype(o_ref.dtype)
        else:
            o_ref[b, rows, :] = o
        return new_st

    nchunk = tb // c_len
    order = range(nchunk) if direction == 0 else range(nchunk - 1, -1, -1)
    for c in order:
        for b in range(nb):
            states[b] = chunk(b, c, states[b])
    for b in range(nb):
        st_ref[b] = states[b]

    @pl.when(j == pl.num_programs(0) - 1)
    def _():
        sfin_ref[...] = st_ref[...]


def _gla(p, ps, s0, consts, nb, direction, final, o0=None, ng=None):
    m = p.shape[0]
    seq = m // nb
    tb = SCAN_TB
    nblk = seq // tb
    blk = _scan_block_map(nblk, direction)
    p3 = p.reshape(nb, seq, P_MAIN)
    full = lambda shape: pl.BlockSpec(shape, lambda j: (0,) * len(shape))
    rows = lambda width, col: pl.BlockSpec((nb, tb, width), lambda j: (0, blk(j), col))
    in_specs = [rows(256, 16), rows(256, 17), rows(GW, 9), rows(LANE, 0),
                full((LANE, 256)), full((1, 256)), full((nb, GLA_DV, 256))]
    args = [p3, p3, p3, ps.reshape(nb, seq, LANE), *consts, s0]
    if final:
        in_specs += [rows(GW, 0), rows(GW, 10), full((1, GLA_DV))]
        args += [o0.reshape(nb, seq, GW), p3, ng]
    o, s_fin = pl.pallas_call(
        functools.partial(_gla_kernel, direction=direction, final=final),
        out_shape=(jax.ShapeDtypeStruct((nb, seq, GW), BF16 if final else F32),
                   jax.ShapeDtypeStruct((nb, GLA_DV, 256), F32)),
        grid=(nblk,),
        in_specs=in_specs,
        out_specs=(rows(GW, 0), full((nb, GLA_DV, 256))),
        scratch_shapes=[pltpu.VMEM((nb, GLA_DV, 256), F32)],
        name=f"gla_d{direction}",
    )(*args)
    return o.reshape(m, GW), s_fin


def _outproj_kernel(*refs, route):
    if route:
        (h_ref, ya_ref, ymb_ref, ygla_ref, w_ref, gate_ref, g_ref, sh_ref, sc_ref, wr_ref,
         hn_ref, xn_ref, ri_ref, rw_ref, cnt_ref, carry_ref, xprev_ref) = refs
    else:
        (h_ref, ya_ref, ymb_ref, ygla_ref, w_ref, gate_ref, g_ref, sh_ref, sc_ref,
         hn_ref, xn_ref) = refs

    def project():
        acc = (_dot(ya_ref[...], w_ref[0:2 * GW, :]) + _dot(ymb_ref[...], w_ref[2 * GW:3 * GW, :])
               + _dot(ygla_ref[...], w_ref[3 * GW:4 * GW, :]))
        hn = h_ref[...] + gate_ref[...] * acc
        hn_ref[...] = hn
        xn = _rms(hn, g_ref[...]) * (1.0 + sc_ref[...]) + sh_ref[...]
        xn_ref[...] = xn.astype(xn_ref.dtype)
        return xn

    if not route:
        project()
        return

    i = pl.program_id(0)
    n_tiles = pl.num_programs(0) - 1

    def route_prev():
        _route_tile(xprev_ref[...], wr_ref, ri_ref, rw_ref, cnt_ref, carry_ref)

    @pl.when(i == 0)
    def _():
        carry_ref[...] = jnp.zeros_like(carry_ref)
        xprev_ref[...] = project()

    @pl.when((i > 0) & (i < n_tiles))
    def _():
        route_prev()
        xprev_ref[...] = project()

    @pl.when(i == n_tiles)
    def _():
        route_prev()


def _route_tile(xn, wr_ref, ri_ref, rw_ref, cnt_ref, carry_ref):
    tm = xn.shape[0]
    xh = xn.astype(BF16)
    xl = (xn - xh.astype(F32)).astype(BF16)
    wr = wr_ref[...]
    wh = wr.astype(BF16)
    wl = (wr - wh.astype(F32)).astype(BF16)
    hh_hl = _dot(xh, jnp.concatenate([wh, wl], axis=1))
    logits = hh_hl[:, 0:LANE] + hh_hl[:, LANE:2 * LANE] + _dot(xl, wh)
    lane = lax.broadcasted_iota(jnp.int32, (tm, LANE), 1)
    lane_f = lane.astype(F32)
    neg = jnp.float32(-3.0e38)
    lm = jnp.where(lane < N_EXPERTS, logits, neg)
    m1 = jnp.max(lm, axis=-1, keepdims=True)
    i1 = jnp.min(jnp.where(lm == m1, lane_f, float(LANE)), axis=-1, keepdims=True)
    lm2 = jnp.where(lane_f == i1, neg, lm)
    m2 = jnp.max(lm2, axis=-1, keepdims=True)
    i2 = jnp.min(jnp.where(lm2 == m2, lane_f, float(LANE)), axis=-1, keepdims=True)
    e = jnp.exp(m2 - m1)
    w1 = 1.0 / (1.0 + e)
    w2 = e / (1.0 + e)
    sel1 = lane_f == i1
    sel2 = lane_f == i2
    onehot = jnp.where(sel1 | sel2, 1.0, 0.0)
    ti = lax.broadcasted_iota(jnp.int32, (tm, tm), 0)
    tj = lax.broadcasted_iota(jnp.int32, (tm, tm), 1)
    incl = _dot((tj <= ti).astype(BF16), onehot.astype(BF16)) + carry_ref[...]
    excl = incl - onehot
    r1 = jnp.sum(jnp.where(sel1, excl, 0.0), axis=-1, keepdims=True)
    r2 = jnp.sum(jnp.where(sel2, excl, 0.0), axis=-1, keepdims=True)
    info = jnp.where(lane == 0, i1, jnp.where(lane == 1, i2, jnp.where(lane == 2, r1, jnp.where(lane == 3, r2, 0.0))))
    ri_ref[...] = info.astype(jnp.int32)
    rw_ref[...] = jnp.where(lane == 0, w1, jnp.where(lane == 1, w2, 0.0))
    carry_ref[...] = incl[tm - 1:tm, :]
    cnt_ref[...] = incl[tm - 1:tm, :]


def _outproj(h, ya, ymb, ygla, w, modr, row_fn, g, w_router=None):
    m = h.shape[0]
    tm = OUT_TM
    route = w_router is not None
    n_tiles = m // tm
    cur = (lambda i: jnp.minimum(i, n_tiles - 1)) if route else (lambda i: i)
    prev = lambda i: jnp.maximum(i - 1, 0)
    mod = lambda k: pl.BlockSpec((None, None, 1, D), lambda i: (row_fn(cur(i)), k, 0, 0))
    in_specs = [pl.BlockSpec((tm, D), lambda i: (cur(i), 0)),
                pl.BlockSpec((tm, 2 * GW), lambda i: (cur(i), 0)),
                pl.BlockSpec((tm, GW), lambda i: (cur(i), 0)),
                pl.BlockSpec((tm, GW), lambda i: (cur(i), 0)),
                pl.BlockSpec((D, D), lambda i: (0, 0)),
                mod(2), pl.BlockSpec((1, D), lambda i: (0, 0)), mod(3), mod(4)]
    args = [h, ya, ymb, ygla, w, modr, g, modr, modr]
    out_shape = [jax.ShapeDtypeStruct((m, D), F32), jax.ShapeDtypeStruct((m, D), F32 if route else BF16)]
    out_specs = [pl.BlockSpec((tm, D), lambda i: (cur(i), 0)), pl.BlockSpec((tm, D), lambda i: (cur(i), 0))]
    scratch = []
    if route:
        in_specs.append(pl.BlockSpec((D, LANE), lambda i: (0, 0)))
        args.append(w_router)
        out_shape += [jax.ShapeDtypeStruct((m, LANE), jnp.int32), jax.ShapeDtypeStruct((m, LANE), F32),
                      jax.ShapeDtypeStruct((1, LANE), F32)]
        out_specs += [pl.BlockSpec((tm, LANE), lambda i: (prev(i), 0)), pl.BlockSpec((tm, LANE), lambda i: (prev(i), 0)),
                      pl.BlockSpec((1, LANE), lambda i: (0, 0))]
        scratch = [pltpu.VMEM((1, LANE), F32), pltpu.VMEM((tm, D), F32)]
    return pl.pallas_call(
        functools.partial(_outproj_kernel, route=route),
        out_shape=tuple(out_shape),
        grid=(n_tiles + 1 if route else n_tiles,),
        in_specs=in_specs,
        out_specs=tuple(out_specs),
        scratch_shapes=scratch,
        name="outproj_route" if route else "outproj",
    )(*args)


def _swiglu_rows(x_ref, wg_ref, wu_ref, wd_ref, o_ref, wgb_ref, wub_ref, wdb_ref, sb_rows, nsb, gate,
                 prologue=None):
    def sub_block(sb):
        start = sb * sb_rows
        return pl.ds(start if isinstance(sb, int) else pl.multiple_of(start, BF16_SUBLANES), sb_rows)

    def up(sb):
        x = x_ref[sub_block(sb), :]
        return (_silu(_dot(x, wgb_ref[...])) * _dot(x, wub_ref[...])).astype(BF16)

    def down(sb, hidden):
        rows = sub_block(sb)
        y = _dot(hidden, wdb_ref[...])
        o_ref[rows, :] += y if gate is None else gate * y

    if prologue is not None:
        prologue()
    if wg_ref is not None:
        wgb_ref[...] = wg_ref[...].astype(BF16)
        wub_ref[...] = wu_ref[...].astype(BF16)
    hidden = up(0)
    if wd_ref is not None:
        wdb_ref[...] = wd_ref[...].astype(BF16)
    if isinstance(nsb, int):
        for sb in range(1, nsb):
            nxt = up(sb)
            down(sb - 1, hidden)
            hidden = nxt
    else:
        def body(sb, hid):
            nxt = up(sb)
            down(sb - 1, hid)
            return nxt

        hidden = lax.fori_loop(1, nsb, body, hidden)
    down(nsb - 1, hidden)


def _ffn_kernel(*refs, emit):
    if emit:
        x_ref, h_hbm, gate_ref, wg_ref, wu_ref, wd_ref, o_ref, wgb_ref, wub_ref, wdb_ref = refs
    else:
        x_ref, h_hbm, gate_ref, wgb_ref, wub_ref, wdb_ref, o_ref = refs
        wg_ref = wu_ref = wd_ref = None
    tm = o_ref.shape[0]

    @pl.when(pl.program_id(1) == 0)
    def _():
        pltpu.sync_copy(h_hbm.at[pl.ds(pl.program_id(0) * tm, tm)], o_ref)

    _swiglu_rows(x_ref, wg_ref, wu_ref, wd_ref, o_ref, wgb_ref, wub_ref, wdb_ref, FFN_SB, tm // FFN_SB,
                 gate_ref[...])


def _ffn(xn, h, modr, row_fn, wg, wu, wd, tm, emit):
    m = h.shape[0]
    tf = MOE_TF
    dff = wg.shape[1]
    assert not emit or m == tm
    w_specs = [pl.BlockSpec((D, tf), lambda i, f: (0, f)),
               pl.BlockSpec((D, tf), lambda i, f: (0, f)),
               pl.BlockSpec((tf, D), lambda i, f: (f, 0))]
    out_shape = [jax.ShapeDtypeStruct((m, D), F32)]
    out_specs = [pl.BlockSpec((tm, D), lambda i, f: (i, 0))]
    if emit:
        out_shape += [jax.ShapeDtypeStruct((D, dff), BF16), jax.ShapeDtypeStruct((D, dff), BF16),
                      jax.ShapeDtypeStruct((dff, D), BF16)]
        out_specs += w_specs
    return pl.pallas_call(
        functools.partial(_ffn_kernel, emit=emit),
        out_shape=tuple(out_shape),
        grid=(m // tm, dff // tf),
        in_specs=[pl.BlockSpec((tm, D), lambda i, f: (i, 0)),
                  pl.BlockSpec(memory_space=pl.ANY),
                  pl.BlockSpec((None, None, 1, D), lambda i, f: (row_fn(i), 5, 0, 0))] + w_specs,
        out_specs=tuple(out_specs),
        name="ffn_emit" if emit else "ffn",
    )(xn, h, modr, wg, wu, wd)


def _row_copy(src_hbm, row, dst, r, sem):
    return pltpu.make_async_copy(src_hbm.at[pl.ds(row, 1)], dst.at[pl.ds(r, 1)], sem)


def _expert_kernel(te_ref, nv_ref, nu_ref, pos_ref, xn_hbm, wg_ref, wu_ref, wd_ref, o_ref,
                   xb_ref, stage_ref, wgb_ref, wub_ref, wdb_ref, src_ref, sem):
    i = pl.program_id(0)
    f = pl.program_id(1)
    n_tiles = pl.num_programs(0)
    nf = pl.num_programs(1)
    used = i < nu_ref[0]
    stage_rows = stage_ref.shape[0]
    unroll = math.gcd(stage_rows, 8)

    def wait_stage():
        def wait(r, carry):
            _row_copy(xn_hbm, 0, stage_ref, r, sem).wait()
            return carry

        lax.fori_loop(0, stage_rows, wait, 0, unroll=unroll)

    @pl.when(f == 0)
    def _():
        @pl.when(i == 0)
        def _():
            def clear(r, carry):
                src_ref[r] = 0
                return carry

            lax.fori_loop(0, src_ref.shape[0], clear, 0, unroll=8)

            def invert(t, carry):
                src_ref[pos_ref[2 * t]] = t
                src_ref[pos_ref[2 * t + 1]] = t
                return carry

            lax.fori_loop(0, pos_ref.shape[0] // 2, invert, 0, unroll=8)

            def issue(r, carry):
                _row_copy(xn_hbm, src_ref[r], stage_ref, r, sem).start()
                return carry

            lax.fori_loop(0, stage_rows, issue, 0, unroll=unroll)

        @pl.when(i <= nu_ref[0])
        def _():
            wait_stage()
            xb_ref[...] = stage_ref[0:MOE_TM, :].astype(BF16)

    def issue_next_rows():
        base = (i + 1) * MOE_TM + f * MOE_ISSUE
        for k in range(MOE_ISSUE):
            _row_copy(xn_hbm, src_ref[base + k], stage_ref, f * MOE_ISSUE + k, sem).start()

    @pl.when(used)
    def _():
        @pl.when(f == 0)
        def _():
            o_ref[...] = jnp.zeros_like(o_ref)

        nsb = (nv_ref[i] + MOE_SB - 1) // MOE_SB
        refs = (xb_ref, wg_ref, wu_ref, wd_ref, o_ref, wgb_ref, wub_ref, wdb_ref)

        @pl.when(nsb == MOE_TM // MOE_SB)
        def _():
            _swiglu_rows(*refs, MOE_SB, MOE_TM // MOE_SB, None, issue_next_rows)

        @pl.when(nsb < MOE_TM // MOE_SB)
        def _():
            _swiglu_rows(*refs, MOE_SB, nsb, None, issue_next_rows)

    @pl.when(jnp.logical_not(used) & (f == 0))
    def _():
        o_ref[...] = jnp.zeros_like(o_ref)

    @pl.when(used & (i == n_tiles - 1) & (f == nf - 1))
    def _():
        wait_stage()


def _experts(xn, pos, te, nv, nu, wg, wu, wd):
    dff = wg.shape[2]
    nf = dff // MOE_TF
    nt = te.shape[0]
    stage_rows = nf * MOE_ISSUE
    assert stage_rows >= MOE_TM
    src_rows = (nt + 2) * MOE_TM

    def f_eff(i, f, nu):
        return jnp.where(i < nu[0], f, nf - 1)

    return pl.pallas_call(
        _expert_kernel,
        out_shape=jax.ShapeDtypeStruct((nt * MOE_TM, D), F32),
        grid_spec=pltpu.PrefetchScalarGridSpec(
            num_scalar_prefetch=4,
            grid=(nt, nf),
            in_specs=[pl.BlockSpec(memory_space=pl.ANY),
                      pl.BlockSpec((None, D, MOE_TF), lambda i, f, te, nv, nu, pos: (te[i], 0, f_eff(i, f, nu))),
                      pl.BlockSpec((None, D, MOE_TF), lambda i, f, te, nv, nu, pos: (te[i], 0, f_eff(i, f, nu))),
                      pl.BlockSpec((None, MOE_TF, D), lambda i, f, te, nv, nu, pos: (te[i], f_eff(i, f, nu), 0))],
            out_specs=pl.BlockSpec((MOE_TM, D), lambda i, f, te, nv, nu, pos: (i, 0)),
            scratch_shapes=[pltpu.VMEM((MOE_TM, D), BF16), pltpu.VMEM((stage_rows, D), F32),
                            pltpu.VMEM((D, MOE_TF), BF16), pltpu.VMEM((D, MOE_TF), BF16),
                            pltpu.VMEM((MOE_TF, D), BF16), pltpu.SMEM((src_rows,), jnp.int32),
                            pltpu.SemaphoreType.DMA(())]),
        compiler_params=pltpu.CompilerParams(vmem_limit_bytes=MOE_VMEM_LIMIT),
        name="moe_experts",
    )(te, nv, nu, pos, xn, wg, wu, wd)


def _combine_kernel(pos_ref, h_ref, rw_ref, gate_ref, gfin_ref, y_hbm, o_ref, buf_ref, sem):
    i = pl.program_id(0)
    r_blk = h_ref.shape[0]
    slot = i % 2

    def issue_tile(tile, s):
        def issue(r, carry):
            t = tile * r_blk + r
            _row_copy(y_hbm, pos_ref[2 * t], buf_ref.at[s, 0], r, sem.at[s]).start()
            _row_copy(y_hbm, pos_ref[2 * t + 1], buf_ref.at[s, 1], r, sem.at[s]).start()
            return carry

        lax.fori_loop(0, r_blk, issue, 0, unroll=8)

    @pl.when(i == 0)
    def _():
        issue_tile(0, 0)

    @pl.when(i + 1 < pl.num_programs(0))
    def _():
        issue_tile(i + 1, 1 - slot)

    def wait(r, carry):
        _row_copy(y_hbm, 0, buf_ref.at[slot, 0], r, sem.at[slot]).wait()
        _row_copy(y_hbm, 0, buf_ref.at[slot, 1], r, sem.at[slot]).wait()
        return carry

    lax.fori_loop(0, r_blk, wait, 0, unroll=8)
    y = rw_ref[:, 0:1] * buf_ref[slot, 0] + rw_ref[:, 1:2] * buf_ref[slot, 1]
    hn = h_ref[...] + gate_ref[...] * y
    o_ref[...] = _rms(hn, gfin_ref[...])


def _combine(pos, h, rw, modr, row_fn, gfin, ys):
    m = h.shape[0]
    r = GATHER_R
    return pl.pallas_call(
        _combine_kernel,
        out_shape=jax.ShapeDtypeStruct((m, D), F32),
        grid_spec=pltpu.PrefetchScalarGridSpec(
            num_scalar_prefetch=1,
            grid=(m // r,),
            in_specs=[pl.BlockSpec((r, D), lambda i, pos: (i, 0)),
                      pl.BlockSpec((r, LANE), lambda i, pos: (i, 0)),
                      pl.BlockSpec((None, None, 1, D), lambda i, pos: (row_fn(i), 5, 0, 0)),
                      pl.BlockSpec((1, D), lambda i, pos: (0, 0)),
                      pl.BlockSpec(memory_space=pl.ANY)],
            out_specs=pl.BlockSpec((r, D), lambda i, pos: (i, 0)),
            scratch_shapes=[pltpu.VMEM((2, 2, r, D), F32), pltpu.SemaphoreType.DMA((2,))]),
        name="moe_combine",
    )(pos, h, rw, modr, gfin, ys)


def _moe(xn, h, ri, rw, cnt, modr, row_fn, gfin, wg, wu, wd):
    m = h.shape[0]
    tm = MOE_TM
    nt = 2 * m // tm + N_EXPERTS
    counts = cnt[0, :N_EXPERTS].astype(jnp.int32)
    nt_e = (counts + tm - 1) // tm
    t_end = jnp.cumsum(nt_e)
    t_start = t_end - nt_e
    n_used = t_end[-1]
    tid = jnp.arange(nt, dtype=jnp.int32)
    te_raw = jnp.sum((tid[:, None] >= t_end[None, :]).astype(jnp.int32), axis=1)
    te_last = jnp.sum((n_used - 1 >= t_end).astype(jnp.int32))
    te = jnp.minimum(te_raw, te_last).astype(jnp.int32)
    nv = jnp.where(tid < n_used, jnp.clip(counts[te] - (tid - t_start[te]) * tm, 0, tm), 0).astype(jnp.int32)
    pos = (t_start[ri[:, 0:2]] * tm + ri[:, 2:4]).astype(jnp.int32).reshape(-1)
    nu = n_used.reshape(1).astype(jnp.int32)
    ys = _experts(xn, pos, te, nv, nu, wg, wu, wd)
    return _combine(pos, h, rw, modr, row_fn, gfin, ys)


def _relayout_main_kernel(wt_ref, main_ref):
    main_ref[...] = wt_ref[0].T.astype(BF16)


def _relayout_small_kernel(dt_ref, glr_ref, small_ref, buf_ref):
    buf_ref[...] = jnp.zeros_like(buf_ref)
    buf_ref[0:N_DT, :] = dt_ref[0]
    buf_ref[N_DT:N_DT + N_GLR, :] = glr_ref[0]
    small_ref[...] = buf_ref[...].T.astype(BF16)


def _relayout_w_in(w_in):
    depth, rows, cols = w_in.shape
    wt = jnp.swapaxes(w_in, 1, 2)
    tc = RELAYOUT_TC
    shifted = COL_DT // tc
    assert P_MAIN // tc <= 2 * shifted
    src_row = lambda k: pl.multiple_of(k * tc + N_DT * (k // shifted), N_DT)
    main = pl.pallas_call(
        _relayout_main_kernel,
        out_shape=jax.ShapeDtypeStruct((depth, rows, P_MAIN), BF16),
        grid=(depth, P_MAIN // tc),
        in_specs=[pl.BlockSpec((pl.Element(1), pl.Element(tc), pl.Element(rows)), lambda l, k: (l, src_row(k), 0))],
        out_specs=pl.BlockSpec((None, rows, tc), lambda l, k: (l, 0, k)),
        name="relayout_w_in",
    )(wt)
    small = pl.pallas_call(
        _relayout_small_kernel,
        out_shape=jax.ShapeDtypeStruct((depth, rows, LANE), BF16),
        grid=(depth,),
        in_specs=[pl.BlockSpec((pl.Element(1), pl.Element(N_DT), pl.Element(rows)), lambda l: (l, COL_DT, 0)),
                  pl.BlockSpec((pl.Element(1), pl.Element(N_GLR), pl.Element(rows)), lambda l: (l, COL_GLR, 0))],
        out_specs=pl.BlockSpec((None, rows, LANE), lambda l: (l, 0, 0)),
        scratch_shapes=[pltpu.VMEM((LANE, rows), F32)],
        name="relayout_w_small",
    )(wt, wt)
    return main, small


def _head_row(v, direction):
    lo = SSD_HEADS * direction
    return jnp.pad(v.astype(F32).reshape(1, SSD_HEADS), ((0, 0), (lo, LANE - SSD_HEADS - lo)))


def kernel(x, c, ctx, c_ctx, w_ada, b_ada, g_mix, g_ffn, w_in, w_out, sc_conv_w, cm_w_s, cm_b_s, mb_conv_w,
           mb_conv_b, mb_a_log, mb_dt_bias, mb_d, mb_norm_g, gla_w_gate, gla_b_gate, gla_norm_g, ffn_w_gate,
           ffn_w_up, ffn_w_down, moe_router, moe_w_gate, moe_w_up, moe_w_down, g_final):
    nb, seq, _ = x.shape
    ctx_len = ctx.shape[1]
    depth = w_ada.shape[0]
    assert depth == 2, "supported stack: a dense-FFN layer followed by a last, routed-FFN layer"
    m_lat = nb * seq
    h = x.reshape(m_lat, D)
    hc = ctx.reshape(nb * ctx_len, D)

    cond = jnp.concatenate([c, c_ctx[None, :]], axis=0)
    cb = jnp.broadcast_to(cond[:, :, None], (nb + 1, D, LANE))
    mods = _ada(cb, w_ada, b_ada).reshape(depth, 8, N_MOD, 1, D)

    tm_in = IN_TM
    lat_row_in = lambda i: i // (seq // tm_in)
    lat_row_512 = lambda i: i // (seq // OUT_TM)
    lat_row_g = lambda i: i // (seq // GATHER_R)
    ctx_row = lambda i: nb

    head_of_col = np.arange(GW) // SSD_HD
    expand = [jnp.asarray(np.arange(LANE)[:, None] == 8 * d + head_of_col[None, :], dtype=BF16) for d in range(2)]
    w_in_main, w_in_small = _relayout_w_in(w_in)
    out = None
    for i in range(depth):
        last = i == depth - 1
        modr = mods[i]
        g_mix_i = g_mix[i].reshape(1, D)
        g_ffn_i = g_ffn[i].reshape(1, D)
        p, ps = _inproj(h, g_mix_i, modr, lat_row_in, w_in_main, w_in_small, i, tm_in)
        pc, psc = _inproj(hc, g_mix_i, modr, ctx_row, w_in_main, w_in_small, i, nb * ctx_len)

        ws = cm_w_s[i].astype(BF16)
        bs = jnp.repeat(cm_b_s[i].T, 128, axis=1)
        ya = _scm(p, sc_conv_w[i], ws, bs, GRID_W)
        if not last:
            yac = _scm(pc, sc_conv_w[i], ws, bs, ctx_len)

        conv_b = mb_conv_b[i].reshape(1, -1)
        s_zero = jnp.zeros((nb, SSD_Q, GW), F32)
        skip = jnp.repeat(mb_d[i, 0] + mb_d[i, 1], SSD_HD).reshape(1, GW)
        fin = (skip, mb_norm_g[i].reshape(1, GW))
        ssd_c = [(mb_conv_w[i], conv_b, _head_row(mb_a_log[i, d], d), _head_row(mb_dt_bias[i, d], d), expand[d])
                 for d in range(2)]
        y0c, s0c = _ssd(pc, psc, s_zero, ssd_c[0], nb, ctx_len, 0, False)
        y0, _ = _ssd(p, ps, s0c, ssd_c[0], nb, GRID_W, 0, False)
        if last:
            _, s1c = _ssd(pc, psc, s_zero, ssd_c[1], nb, ctx_len, 1, False)
        else:
            ymbc, s1c = _ssd(pc, psc, s_zero, ssd_c[1], nb, ctx_len, 1, True, y0c, fin)
        ymb, _ = _ssd(p, ps, s1c, ssd_c[1], nb, GRID_W, 1, True, y0, fin)

        g_zero = jnp.zeros((nb, GLA_DV, 256), F32)
        gla_c = [(jnp.zeros((LANE, 256), F32).at[16 + 16 * d:32 + 16 * d].set(gla_w_gate[i, d]).astype(BF16),
                  gla_b_gate[i, d].reshape(1, 256)) for d in range(2)]
        ng = gla_norm_g[i].reshape(1, GLA_DV)
        o0c, t0c = _gla(pc, psc, g_zero, gla_c[0], nb, 0, False)
        o0, _ = _gla(p, ps, t0c, gla_c[0], nb, 0, False)
        if last:
            _, t1c = _gla(pc, psc, g_zero, gla_c[1], nb, 1, False)
        else:
            yglac, t1c = _gla(pc, psc, g_zero, gla_c[1], nb, 1, True, o0c, ng)
        ygla = _gla(p, ps, t1c, gla_c[1], nb, 1, True, o0, ng)[0]

        w_out_b = w_out[i].astype(BF16)
        j = i // 2
        if i % 2 == 0:
            hcn, xcn = _outproj(hc, yac, ymbc, yglac, w_out_b, modr, ctx_row, g_ffn_i)
            hc, wg, wu, wd = _ffn(xcn, hcn, modr, ctx_row, ffn_w_gate[j], ffn_w_up[j], ffn_w_down[j],
                                  nb * ctx_len, True)
            hn, xn = _outproj(h, ya, ymb, ygla, w_out_b, modr, lat_row_512, g_ffn_i)
            h = _ffn(xn, hn, modr, lat_row_in, wg, wu, wd, tm_in, False)[0]
        else:
            w_router = jnp.pad(moe_router[j], ((0, 0), (0, LANE - N_EXPERTS)))
            hn, xn, ri, rw, cnt = _outproj(h, ya, ymb, ygla, w_out_b, modr, lat_row_512, g_ffn_i, w_router)
            if last:
                out = _moe(xn, hn, ri, rw, cnt, modr, lat_row_g, g_final.reshape(1, D),
                           moe_w_gate[j], moe_w_up[j], moe_w_down[j])
            else:
                raise NotImplementedError("routed FFN is only implemented as the last layer's channel mixer")
    return out.reshape(nb, seq, D)
```

```python
import functools
import math

import jax
import jax.numpy as jnp
import numpy as np
from jax import lax
from jax.experimental import pallas as pl
from jax.experimental.pallas import tpu as pltpu

F32 = jnp.float32
BF16 = jnp.bfloat16
EPS = 1e-6

D = 2048
GW = D // 4
GRID_W = 64
N_MOD = 6
LANE = 128
BF16_SUBLANES = 16
P_MAIN = 5632
COL_DT, N_DT = 4096, 16
COL_GLR, N_GLR = 5648, 32

SSD_Q = 128
SSD_HEADS = 8
SSD_HD = 64
GLA_C = 64
GLA_HEADS = 4
GLA_DK = 64
GLA_DV = 128
GLA_RANK = 16
N_EXPERTS = 8

FFN_SB = 256
MOE_SB = 272
MOE_TM = 4 * MOE_SB
MOE_TF = 512
MOE_ISSUE = 78
MOE_VMEM_LIMIT = 127 * 512 * 1024
GATHER_R = 256

ADA_TN = 1024
IN_TM = 1024
SCM_T = 512
SCAN_TB = 256
OUT_TM = 512
RELAYOUT_TC = 256


def _silu(x):
    return x * jax.nn.sigmoid(x)


def _softplus(x):
    return jnp.maximum(x, 0.0) + jnp.log1p(jnp.exp(-jnp.abs(x)))


def _split3(x):
    hi = x.astype(BF16)
    r = x - hi.astype(F32)
    mid = r.astype(BF16)
    lo = (r - mid.astype(F32)).astype(BF16)
    return hi, mid, lo


def _dot(a, b):
    return jnp.dot(a, b, preferred_element_type=F32)


def _dot_nt(a, b):
    return lax.dot_general(a, b, (((1,), (1,)), ((), ())), preferred_element_type=F32)


def _dot_tn(a, b):
    return lax.dot_general(a, b, (((0,), (0,)), ((), ())), preferred_element_type=F32)


def _dot01_left(m01, x):
    hi, mid, lo = _split3(x)
    return _dot(m01, hi) + _dot(m01, mid) + _dot(m01, lo)


def _dot01_right(x, m01):
    hi, mid, lo = _split3(x)
    return _dot(hi, m01) + _dot(mid, m01) + _dot(lo, m01)


def _rms(x, g):
    return x * lax.rsqrt(jnp.mean(x * x, axis=-1, keepdims=True) + EPS) * g


def _conv3(x, w, seg):
    n = x.shape[0]
    t = lax.broadcasted_iota(jnp.int32, (n, 1), 0) & (seg - 1)
    prev = jnp.where(t == 0, 0.0, pltpu.roll(x, 1, 0))
    nxt = jnp.where(t == seg - 1, 0.0, pltpu.roll(x, n - 1, 0))
    return prev * w[0:1] + x * w[1:2] + nxt * w[2:3]


def _ada_kernel(cb_ref, w_ref, b_ref, o_ref, s_ref):
    @pl.when((pl.program_id(0) == 0) & (pl.program_id(1) == 0))
    def _():
        s_ref[...] = _silu(cb_ref[...])

    tn = w_ref.shape[1]
    o_ref[...] = jnp.zeros_like(o_ref)
    for j in range(tn // LANE):
        wj = w_ref[:, j * LANE:(j + 1) * LANE]
        for m in range(3):
            o_ref[m:m + 1, j * LANE:(j + 1) * LANE] = (
                jnp.sum(wj * s_ref[m], axis=0, keepdims=True) + b_ref[:, j * LANE:(j + 1) * LANE])


def _ada(cb, w_ada, b_ada):
    depth = w_ada.shape[0]
    n = w_ada.shape[2]
    tn = ADA_TN
    return pl.pallas_call(
        _ada_kernel,
        out_shape=jax.ShapeDtypeStruct((depth, 8, n), F32),
        grid=(depth, n // tn),
        in_specs=[pl.BlockSpec((3, D, LANE), lambda l, j: (0, 0, 0)),
                  pl.BlockSpec((None, D, tn), lambda l, j: (l, 0, j)),
                  pl.BlockSpec((None, 1, tn), lambda l, j: (l, 0, j))],
        out_specs=pl.BlockSpec((None, 8, tn), lambda l, j: (l, 0, j)),
        scratch_shapes=[pltpu.VMEM((3, D, LANE), F32)],
        name="ada",
    )(cb, w_ada, b_ada.reshape(depth, 1, n))


def _inproj_kernel(h_ref, g_ref, sh_ref, sc_ref, w_ref, ws_ref, o_ref, os_ref, xn_ref):
    @pl.when(pl.program_id(1) == 0)
    def _():
        xn = _rms(h_ref[...], g_ref[...])
        xn_ref[...] = (xn * (1.0 + sc_ref[...]) + sh_ref[...]).astype(BF16)
        os_ref[...] = _dot(xn_ref[...], ws_ref[...])

    o_ref[...] = _dot(xn_ref[...], w_ref[...]).astype(BF16)


def _mod_spec(row_fn, k):
    return pl.BlockSpec((None, None, 1, D), lambda i, j: (row_fn(i), k, 0, 0))


def _inproj(h, g, modr, row_fn, w, w_small, layer, tm):
    m = h.shape[0]
    tn = P_MAIN // 4
    return pl.pallas_call(
        _inproj_kernel,
        out_shape=(jax.ShapeDtypeStruct((m, P_MAIN), BF16), jax.ShapeDtypeStruct((m, LANE), F32)),
        grid=(m // tm, P_MAIN // tn),
        in_specs=[pl.BlockSpec((tm, D), lambda i, j: (i, 0)),
                  pl.BlockSpec((1, D), lambda i, j: (0, 0)),
                  _mod_spec(row_fn, 0), _mod_spec(row_fn, 1),
                  pl.BlockSpec((None, D, tn), lambda i, j: (layer, 0, j)),
                  pl.BlockSpec((None, D, LANE), lambda i, j: (layer, 0, 0))],
        out_specs=(pl.BlockSpec((tm, tn), lambda i, j: (i, j)),
                   pl.BlockSpec((tm, LANE), lambda i, j: (i, 0))),
        scratch_shapes=[pltpu.VMEM((tm, D), BF16)],
        name="inproj",
    )(h, g, modr, modr, w, w_small)


def _gelu_tanh(x):
    c = 0.7978845608028654
    return x * (0.5 * (1.0 + jnp.tanh(c * (x + 0.044715 * (x * x * x)))))


def _scm_kernel(sc_ref, u_ref, v_ref, cw_ref, ws_ref, bs_ref, o_ref, *, seg):
    t = sc_ref.shape[0]
    bgate = sc_ref[:, 0:GW].astype(F32)
    gated = sc_ref[:, GW:2 * GW].astype(F32) * sc_ref[:, 2 * GW:3 * GW].astype(F32)
    y_sc = bgate * _conv3(gated, cw_ref[...], seg)
    o_ref[:, 0:GW] = y_sc.astype(BF16)

    u = _gelu_tanh(u_ref[...].astype(F32))
    v = _gelu_tanh(v_ref[...].astype(F32)).astype(BF16)
    for c in range(t // 128):
        rows = slice(c * 128, (c + 1) * 128)
        for g in range(4):
            cols = slice(g * 128, (g + 1) * 128)
            s = _dot(ws_ref[g], v[rows, cols]) + bs_ref[:, cols]
            o_ref[rows, GW + g * 128:GW + (g + 1) * 128] = (u[rows, cols] * s).astype(BF16)


def _scm(p, cw, ws, bs, seg):
    m = p.shape[0]
    t = SCM_T
    return pl.pallas_call(
        functools.partial(_scm_kernel, seg=seg),
        out_shape=jax.ShapeDtypeStruct((m, 2 * GW), BF16),
        grid=(m // t,),
        in_specs=[pl.BlockSpec((t, 3 * GW), lambda i: (i, 0)),
                  pl.BlockSpec((t, GW), lambda i: (i, 3)),
                  pl.BlockSpec((t, GW), lambda i: (i, 4)),
                  pl.BlockSpec((3, GW), lambda i: (0, 0)),
                  pl.BlockSpec((4, 128, 128), lambda i: (0, 0, 0)),
                  pl.BlockSpec((128, GW), lambda i: (0, 0))],
        out_specs=pl.BlockSpec((t, 2 * GW), lambda i: (i, 0)),
        name="scm",
    )(p, p, p, cw, ws, bs)


def _ssd_kernel(*refs, direction, final, seg):
    if final:
        (xbc_ref, sm_ref, cw_ref, cb_ref, alog_ref, dtb_ref, e_ref, s0_ref,
         y0_ref, z_ref, skip_ref, ng_ref, y_ref, sfin_ref, st_ref) = refs
    else:
        (xbc_ref, sm_ref, cw_ref, cb_ref, alog_ref, dtb_ref, e_ref, s0_ref,
         y_ref, sfin_ref, st_ref) = refs
    j = pl.program_id(0)
    nb, tb = xbc_ref.shape[0], xbc_ref.shape[1]
    q = SSD_Q

    @pl.when(j == 0)
    def _():
        st_ref[...] = s0_ref[...]

    a_neg = -jnp.exp(alog_ref[...])
    ii = lax.broadcasted_iota(jnp.int32, (q, q), 0)
    jj = lax.broadcasted_iota(jnp.int32, (q, q), 1)
    mask = (jj <= ii) if direction == 0 else (jj >= ii)
    tri = mask.astype(BF16)
    last = q - 1 if direction == 0 else 0
    lane0 = 8 * direction
    upper_half = lax.broadcasted_iota(jnp.int32, (q, 128), 1) >= SSD_HD
    expand = e_ref[...]

    xcs = [_silu(_conv3(xbc_ref[b].astype(F32), cw_ref[...], seg) + cb_ref[...])
           for b in range(nb)]
    states = [st_ref[b] for b in range(nb)]

    def chunk(b, c, st):
        rows = slice(c * q, (c + 1) * q)
        xc = xcs[b]
        dt = _softplus(sm_ref[b, rows, :] + dtb_ref[...])
        acs = _dot01_left(tri, dt * a_neg)
        tot = acs[last:last + 1, :]
        acs_t = acs.T
        dt_t = dt.T
        wx = _dot01_right(jnp.exp(tot - acs) * dt, expand)
        ea = _dot01_right(jnp.exp(acs), expand)
        xs = xc[rows, 0:GW]
        xw = (wx * xs).astype(BF16)
        stb = st.astype(BF16)
        ys = []
        new_st = []
        for g in range(2):
            bg = xc[rows, GW + g * 128:GW + (g + 1) * 128].astype(BF16)
            cg = xc[rows, GW + 256 + g * 128:GW + 256 + (g + 1) * 128].astype(BF16)
            cb = _dot_nt(cg, bg)
            gcols = slice(g * 256, (g + 1) * 256)
            yint = _dot(cg, stb[:, gcols])
            for pr in range(2):
                pidx = 2 * g + pr
                pcols = slice(pidx * 128, (pidx + 1) * 128)
                xp = xs[:, pcols]
                acc = yint[:, pr * 128:(pr + 1) * 128] * ea[:, pcols]
                for hh in range(2):
                    ln = lane0 + 2 * pidx + hh
                    sg = acs[:, ln:ln + 1] - acs_t[ln:ln + 1, :]
                    decay = jnp.where(mask, jnp.exp(jnp.where(mask, sg, 0.0)), 0.0)
                    scores = (cb * decay * dt_t[ln:ln + 1, :]).astype(BF16)
                    xh = jnp.where(upper_half == (hh == 1), xp, 0.0).astype(BF16)
                    acc = acc + _dot(scores, xh)
                ys.append(acc)
            new_st.append(ea[last:last + 1, gcols] * st[:, gcols] + _dot_tn(bg, xw[:, gcols]))
        y = jnp.concatenate(ys, axis=1)
        if final:
            y = y0_ref[b, rows, :] + y + xs * skip_ref[...]
            y = y * _silu(z_ref[b, rows, :].astype(F32))
            y_ref[b, rows, :] = _rms(y, ng_ref[...]).astype(y_ref.dtype)
        else:
            y_ref[b, rows, :] = y
        return jnp.concatenate(new_st, axis=1)

    nchunk = tb // q
    order = range(nchunk) if direction == 0 else range(nchunk - 1, -1, -1)
    for c in order:
        for b in range(nb):
            states[b] = chunk(b, c, states[b])
    for b in range(nb):
        st_ref[b] = states[b]

    @pl.when(j == pl.num_programs(0) - 1)
    def _():
        sfin_ref[...] = st_ref[...]


def _scan_block_map(nblk, direction):
    return (lambda j: j) if direction == 0 else (lambda j: nblk - 1 - j)


def _ssd(p, ps, s0, consts, nb, seg, direction, final, y0=None, fin=None):
    m = p.shape[0]
    seq = m // nb
    tb = SCAN_TB
    nblk = seq // tb
    blk = _scan_block_map(nblk, direction)
    p3 = p.reshape(nb, seq, P_MAIN)
    full = lambda shape: pl.BlockSpec(shape, lambda j: (0,) * len(shape))
    rows = lambda width, col: pl.BlockSpec((nb, tb, width), lambda j: (0, blk(j), col))
    in_specs = [rows(1024, 3), rows(LANE, 0),
                full((3, 1024)), full((1, 1024)), full((1, LANE)), full((1, LANE)), full((LANE, GW)),
                full((nb, SSD_Q, GW))]
    args = [p3, ps.reshape(nb, seq, LANE), *consts, s0]
    if final:
        in_specs += [rows(GW, 0), rows(GW, 5), full((1, GW)), full((1, GW))]
        args += [y0.reshape(nb, seq, GW), p3, *fin]
    y, s_fin = pl.pallas_call(
        functools.partial(_ssd_kernel, direction=direction, final=final, seg=seg),
        out_shape=(jax.ShapeDtypeStruct((nb, seq, GW), BF16 if final else F32),
                   jax.ShapeDtypeStruct((nb, SSD_Q, GW), F32)),
        grid=(nblk,),
        in_specs=in_specs,
        out_specs=(rows(GW, 0), full((nb, SSD_Q, GW))),
        scratch_shapes=[pltpu.VMEM((nb, SSD_Q, GW), F32)],
        name=f"ssd_d{direction}",
    )(*args)
    return y.reshape(m, GW), s_fin


def _gla_kernel(*refs, direction, final):
    if final:
        (q_ref, k_ref, v_ref, sm_ref, wg_ref, bg_ref, s0_ref, o0_ref, g_ref, ng_ref,
         o_ref, sfin_ref, st_ref) = refs
    else:
        (q_ref, k_ref, v_ref, sm_ref, wg_ref, bg_ref, s0_ref, o_ref, sfin_ref, st_ref) = refs
    j = pl.program_id(0)
    nb, tb = q_ref.shape[0], q_ref.shape[1]
    c_len = GLA_C

    @pl.when(j == 0)
    def _():
        st_ref[...] = s0_ref[...]

    ii = lax.broadcasted_iota(jnp.int32, (tb, tb), 0)
    jj = lax.broadcasted_iota(jnp.int32, (tb, tb), 1)
    same = (ii >> 6) == (jj >> 6)
    btri = (same & ((jj <= ii) if direction == 0 else (jj >= ii))).astype(BF16)
    gcums = []
    for b in range(nb):
        gk = -_softplus(-(_dot(sm_ref[b].astype(BF16), wg_ref[...]) + bg_ref[...])) * (1.0 / 16.0)
        gcums.append(_dot01_left(btri, gk))

    hs = GLA_HEADS * c_len
    ri = lax.broadcasted_iota(jnp.int32, (hs, hs), 0)
    ci = lax.broadcasted_iota(jnp.int32, (hs, hs), 1)
    own_lanes = (ri >> 6) == (ci >> 6)
    ti, tj = ri & (c_len - 1), ci & (c_len - 1)
    own_causal = own_lanes & ((tj <= ti) if direction == 0 else (tj >= ti))
    last = c_len - 1 if direction == 0 else 0
    ref = c_len // 2 if direction == 0 else c_len - 1 - c_len // 2

    def stack_heads(x):
        return jnp.where(own_lanes, jnp.concatenate([x] * GLA_HEADS, axis=0), 0.0).astype(BF16)

    states = [st_ref[b] for b in range(nb)]

    def chunk(b, c, st):
        rows = slice(c * c_len, (c + 1) * c_len)
        gc = gcums[b][rows, :]
        gl = gc[last:last + 1, :]
        gr = gc[ref:ref + 1, :]
        qc = q_ref[b, rows, :].astype(F32) * (GLA_DK ** -0.5)
        kc = k_ref[b, rows, :].astype(F32)
        vc = v_ref[b, rows, :]
        v_rows = jnp.concatenate([vc[:, h * GLA_DV:(h + 1) * GLA_DV] for h in range(GLA_HEADS)], axis=0)
        kb = (kc * jnp.exp(gr - gc)).astype(BF16)
        att = _dot_nt(stack_heads(qc * jnp.exp(gc - gr)), jnp.concatenate([kb] * GLA_HEADS, axis=0))
        att = jnp.where(own_causal, att, 0.0).astype(BF16)
        y = _dot(att, v_rows) + _dot_nt(stack_heads(qc * jnp.exp(gc)), st.astype(BF16))
        contrib = _dot_tn(v_rows, stack_heads(kc * jnp.exp(gl - gc)))
        new_st = jnp.exp(gl) * st + contrib
        o = jnp.concatenate([y[h * c_len:(h + 1) * c_len, :] for h in range(GLA_HEADS)], axis=1)
        if final:
            o = o0_ref[b, rows, :] + o
            normed = [_rms(o[:, h * GLA_DV:(h + 1) * GLA_DV], ng_ref[...]) for h in range(GLA_HEADS)]
            gate = _silu(g_ref[b, rows, :].astype(F32))
            o_ref[b, rows, :] = (jnp.concatenate(normed, axis=1) * gate).astype(o_ref.dtype)
        else:
            o_ref[b, rows, :] = o
        return new_st

    nchunk = tb // c_len
    order = range(nchunk) if direction == 0 else range(nchunk - 1, -1, -1)
    for c in order:
        for b in range(nb):
            states[b] = chunk(b, c, states[b])
    for b in range(nb):
        st_ref[b] = states[b]

    @pl.when(j == pl.num_programs(0) - 1)
    def _():
        sfin_ref[...] = st_ref[...]


def _gla(p, ps, s0, consts, nb, direction, final, o0=None, ng=None):
    m = p.shape[0]
    seq = m // nb
    tb = SCAN_TB
    nblk = seq // tb
    blk = _scan_block_map(nblk, direction)
    p3 = p.reshape(nb, seq, P_MAIN)
    full = lambda shape: pl.BlockSpec(shape, lambda j: (0,) * len(shape))
    rows = lambda width, col: pl.BlockSpec((nb, tb, width), lambda j: (0, blk(j), col))
    in_specs = [rows(256, 16), rows(256, 17), rows(GW, 9), rows(LANE, 0),
                full((LANE, 256)), full((1, 256)), full((nb, GLA_DV, 256))]
    args = [p3, p3, p3, ps.reshape(nb, seq, LANE), *consts, s0]
    if final:
        in_specs += [rows(GW, 0), rows(GW, 10), full((1, GLA_DV))]
        args += [o0.reshape(nb, seq, GW), p3, ng]
    o, s_fin = pl.pallas_call(
        functools.partial(_gla_kernel, direction=direction, final=final),
        out_shape=(jax.ShapeDtypeStruct((nb, seq, GW), BF16 if final else F32),
                   jax.ShapeDtypeStruct((nb, GLA_DV, 256), F32)),
        grid=(nblk,),
        in_specs=in_specs,
        out_specs=(rows(GW, 0), full((nb, GLA_DV, 256))),
        scratch_shapes=[pltpu.VMEM((nb, GLA_DV, 256), F32)],
        name=f"gla_d{direction}",
    )(*args)
    return o.reshape(m, GW), s_fin


def _outproj_kernel(*refs, route):
    if route:
        (h_ref, ya_ref, ymb_ref, ygla_ref, w_ref, gate_ref, g_ref, sh_ref, sc_ref, wr_ref,
         hn_ref, xn_ref, ri_ref, rw_ref, cnt_ref, carry_ref, xprev_ref) = refs
    else:
        (h_ref, ya_ref, ymb_ref, ygla_ref, w_ref, gate_ref, g_ref, sh_ref, sc_ref,
         hn_ref, xn_ref) = refs

    def project():
        acc = (_dot(ya_ref[...], w_ref[0:2 * GW, :]) + _dot(ymb_ref[...], w_ref[2 * GW:3 * GW, :])
               + _dot(ygla_ref[...], w_ref[3 * GW:4 * GW, :]))
        hn = h_ref[...] + gate_ref[...] * acc
        hn_ref[...] = hn
        xn = _rms(hn, g_ref[...]) * (1.0 + sc_ref[...]) + sh_ref[...]
        xn_ref[...] = xn.astype(xn_ref.dtype)
        return xn

    if not route:
        project()
        return

    i = pl.program_id(0)
    n_tiles = pl.num_programs(0) - 1

    def route_prev():
        _route_tile(xprev_ref[...], wr_ref, ri_ref, rw_ref, cnt_ref, carry_ref)

    @pl.when(i == 0)
    def _():
        carry_ref[...] = jnp.zeros_like(carry_ref)
        xprev_ref[...] = project()

    @pl.when((i > 0) & (i < n_tiles))
    def _():
        route_prev()
        xprev_ref[...] = project()

    @pl.when(i == n_tiles)
    def _():
        route_prev()


def _route_tile(xn, wr_ref, ri_ref, rw_ref, cnt_ref, carry_ref):
    tm = xn.shape[0]
    xh = xn.astype(BF16)
    xl = (xn - xh.astype(F32)).astype(BF16)
    wr = wr_ref[...]
    wh = wr.astype(BF16)
    wl = (wr - wh.astype(F32)).astype(BF16)
    hh_hl = _dot(xh, jnp.concatenate([wh, wl], axis=1))
    logits = hh_hl[:, 0:LANE] + hh_hl[:, LANE:2 * LANE] + _dot(xl, wh)
    lane = lax.broadcasted_iota(jnp.int32, (tm, LANE), 1)
    lane_f = lane.astype(F32)
    neg = jnp.float32(-3.0e38)
    lm = jnp.where(lane < N_EXPERTS, logits, neg)
    m1 = jnp.max(lm, axis=-1, keepdims=True)
    i1 = jnp.min(jnp.where(lm == m1, lane_f, float(LANE)), axis=-1, keepdims=True)
    lm2 = jnp.where(lane_f == i1, neg, lm)
    m2 = jnp.max(lm2, axis=-1, keepdims=True)
    i2 = jnp.min(jnp.where(lm2 == m2, lane_f, float(LANE)), axis=-1, keepdims=True)
    e = jnp.exp(m2 - m1)
    w1 = 1.0 / (1.0 + e)
    w2 = e / (1.0 + e)
    sel1 = lane_f == i1
    sel2 = lane_f == i2
    onehot = jnp.where(sel1 | sel2, 1.0, 0.0)
    ti = lax.broadcasted_iota(jnp.int32, (tm, tm), 0)
    tj = lax.broadcasted_iota(jnp.int32, (tm, tm), 1)
    incl = _dot((tj <= ti).astype(BF16), onehot.astype(BF16)) + carry_ref[...]
    excl = incl - onehot
    r1 = jnp.sum(jnp.where(sel1, excl, 0.0), axis=-1, keepdims=True)
    r2 = jnp.sum(jnp.where(sel2, excl, 0.0), axis=-1, keepdims=True)
    info = jnp.where(lane == 0, i1, jnp.where(lane == 1, i2, jnp.where(lane == 2, r1, jnp.where(lane == 3, r2, 0.0))))
    ri_ref[...] = info.astype(jnp.int32)
    rw_ref[...] = jnp.where(lane == 0, w1, jnp.where(lane == 1, w2, 0.0))
    carry_ref[...] = incl[tm - 1:tm, :]
    cnt_ref[...] = incl[tm - 1:tm, :]


def _outproj(h, ya, ymb, ygla, w, modr, row_fn, g, w_router=None):
    m = h.shape[0]
    tm = OUT_TM
    route = w_router is not None
    n_tiles = m // tm
    cur = (lambda i: jnp.minimum(i, n_tiles - 1)) if route else (lambda i: i)
    prev = lambda i: jnp.maximum(i - 1, 0)
    mod = lambda k: pl.BlockSpec((None, None, 1, D), lambda i: (row_fn(cur(i)), k, 0, 0))
    in_specs = [pl.BlockSpec((tm, D), lambda i: (cur(i), 0)),
                pl.BlockSpec((tm, 2 * GW), lambda i: (cur(i), 0)),
                pl.BlockSpec((tm, GW), lambda i: (cur(i), 0)),
                pl.BlockSpec((tm, GW), lambda i: (cur(i), 0)),
                pl.BlockSpec((D, D), lambda i: (0, 0)),
                mod(2), pl.BlockSpec((1, D), lambda i: (0, 0)), mod(3), mod(4)]
    args = [h, ya, ymb, ygla, w, modr, g, modr, modr]
    out_shape = [jax.ShapeDtypeStruct((m, D), F32), jax.ShapeDtypeStruct((m, D), F32 if route else BF16)]
    out_specs = [pl.BlockSpec((tm, D), lambda i: (cur(i), 0)), pl.BlockSpec((tm, D), lambda i: (cur(i), 0))]
    scratch = []
    if route:
        in_specs.append(pl.BlockSpec((D, LANE), lambda i: (0, 0)))
        args.append(w_router)
        out_shape += [jax.ShapeDtypeStruct((m, LANE), jnp.int32), jax.ShapeDtypeStruct((m, LANE), F32),
                      jax.ShapeDtypeStruct((1, LANE), F32)]
        out_specs += [pl.BlockSpec((tm, LANE), lambda i: (prev(i), 0)), pl.BlockSpec((tm, LANE), lambda i: (prev(i), 0)),
                      pl.BlockSpec((1, LANE), lambda i: (0, 0))]
        scratch = [pltpu.VMEM((1, LANE), F32), pltpu.VMEM((tm, D), F32)]
    return pl.pallas_call(
        functools.partial(_outproj_kernel, route=route),
        out_shape=tuple(out_shape),
        grid=(n_tiles + 1 if route else n_tiles,),
        in_specs=in_specs,
        out_specs=tuple(out_specs),
        scratch_shapes=scratch,
        name="outproj_route" if route else "outproj",
    )(*args)


def _swiglu_rows(x_ref, wg_ref, wu_ref, wd_ref, o_ref, wgb_ref, wub_ref, wdb_ref, sb_rows, nsb, gate,
                 prologue=None):
    def sub_block(sb):
        start = sb * sb_rows
        return pl.ds(start if isinstance(sb, int) else pl.multiple_of(start, BF16_SUBLANES), sb_rows)

    def up(sb):
        x = x_ref[sub_block(sb), :]
        return (_silu(_dot(x, wgb_ref[...])) * _dot(x, wub_ref[...])).astype(BF16)

    def down(sb, hidden):
        rows = sub_block(sb)
        y = _dot(hidden, wdb_ref[...])
        o_ref[rows, :] += y if gate is None else gate * y

    if prologue is not None:
        prologue()
    if wg_ref is not None:
        wgb_ref[...] = wg_ref[...].astype(BF16)
        wub_ref[...] = wu_ref[...].astype(BF16)
    hidden = up(0)
    if wd_ref is not None:
        wdb_ref[...] = wd_ref[...].astype(BF16)
    if isinstance(nsb, int):
        for sb in range(1, nsb):
            nxt = up(sb)
            down(sb - 1, hidden)
            hidden = nxt
    else:
        def body(sb, hid):
            nxt = up(sb)
            down(sb - 1, hid)
            return nxt

        hidden = lax.fori_loop(1, nsb, body, hidden)
    down(nsb - 1, hidden)


def _ffn_kernel(*refs, emit):
    if emit:
        x_ref, h_hbm, gate_ref, wg_ref, wu_ref, wd_ref, o_ref, wgb_ref, wub_ref, wdb_ref = refs
    else:
        x_ref, h_hbm, gate_ref, wgb_ref, wub_ref, wdb_ref, o_ref = refs
        wg_ref = wu_ref = wd_ref = None
    tm = o_ref.shape[0]

    @pl.when(pl.program_id(1) == 0)
    def _():
        pltpu.sync_copy(h_hbm.at[pl.ds(pl.program_id(0) * tm, tm)], o_ref)

    _swiglu_rows(x_ref, wg_ref, wu_ref, wd_ref, o_ref, wgb_ref, wub_ref, wdb_ref, FFN_SB, tm // FFN_SB,
                 gate_ref[...])


def _ffn(xn, h, modr, row_fn, wg, wu, wd, tm, emit):
    m = h.shape[0]
    tf = MOE_TF
    dff = wg.shape[1]
    assert not emit or m == tm
    w_specs = [pl.BlockSpec((D, tf), lambda i, f: (0, f)),
               pl.BlockSpec((D, tf), lambda i, f: (0, f)),
               pl.BlockSpec((tf, D), lambda i, f: (f, 0))]
    out_shape = [jax.ShapeDtypeStruct((m, D), F32)]
    out_specs = [pl.BlockSpec((tm, D), lambda i, f: (i, 0))]
    if emit:
        out_shape += [jax.ShapeDtypeStruct((D, dff), BF16), jax.ShapeDtypeStruct((D, dff), BF16),
                      jax.ShapeDtypeStruct((dff, D), BF16)]
        out_specs += w_specs
    return pl.pallas_call(
        functools.partial(_ffn_kernel, emit=emit),
        out_shape=tuple(out_shape),
        grid=(m // tm, dff // tf),
        in_specs=[pl.BlockSpec((tm, D), lambda i, f: (i, 0)),
                  pl.BlockSpec(memory_space=pl.ANY),
                  pl.BlockSpec((None, None, 1, D), lambda i, f: (row_fn(i), 5, 0, 0))] + w_specs,
        out_specs=tuple(out_specs),
        name="ffn_emit" if emit else "ffn",
    )(xn, h, modr, wg, wu, wd)


def _row_copy(src_hbm, row, dst, r, sem):
    return pltpu.make_async_copy(src_hbm.at[pl.ds(row, 1)], dst.at[pl.ds(r, 1)], sem)


def _expert_kernel(te_ref, nv_ref, nu_ref, pos_ref, xn_hbm, wg_ref, wu_ref, wd_ref, o_ref,
                   xb_ref, stage_ref, wgb_ref, wub_ref, wdb_ref, src_ref, sem):
    i = pl.program_id(0)
    f = pl.program_id(1)
    n_tiles = pl.num_programs(0)
    nf = pl.num_programs(1)
    used = i < nu_ref[0]
    stage_rows = stage_ref.shape[0]
    unroll = math.gcd(stage_rows, 8)

    def src_token(tile, r):
        tile = jnp.minimum(tile, n_tiles - 1)
        routed = nv_ref[tile]
        row = jnp.where(routed > 0, tile * MOE_TM + jnp.minimum(r, routed - 1), 0)
        return src_ref[row]

    def wait_stage():
        def wait(r, carry):
            _row_copy(xn_hbm, 0, stage_ref, r, sem).wait()
            return carry

        lax.fori_loop(0, stage_rows, wait, 0, unroll=unroll)

    @pl.when(f == 0)
    def _():
        @pl.when(i == 0)
        def _():
            def invert(t, carry):
                src_ref[pos_ref[2 * t]] = t
                src_ref[pos_ref[2 * t + 1]] = t
                return carry

            lax.fori_loop(0, pos_ref.shape[0] // 2, invert, 0, unroll=8)

            def issue(r, carry):
                _row_copy(xn_hbm, src_token(0, r), stage_ref, r, sem).start()
                return carry

            lax.fori_loop(0, stage_rows, issue, 0, unroll=unroll)

        @pl.when(i <= nu_ref[0])
        def _():
            wait_stage()
            xb_ref[...] = stage_ref[0:MOE_TM, :].astype(BF16)

    def issue_next_rows():
        for k in range(MOE_ISSUE):
            r = f * MOE_ISSUE + k
            _row_copy(xn_hbm, src_token(i + 1, r), stage_ref, r, sem).start()

    @pl.when(used)
    def _():
        @pl.when(f == 0)
        def _():
            o_ref[...] = jnp.zeros_like(o_ref)

        nsb = (nv_ref[i] + MOE_SB - 1) // MOE_SB
        refs = (xb_ref, wg_ref, wu_ref, wd_ref, o_ref, wgb_ref, wub_ref, wdb_ref)

        @pl.when(nsb == MOE_TM // MOE_SB)
        def _():
            _swiglu_rows(*refs, MOE_SB, MOE_TM // MOE_SB, None, issue_next_rows)

        @pl.when(nsb < MOE_TM // MOE_SB)
        def _():
            _swiglu_rows(*refs, MOE_SB, nsb, None, issue_next_rows)

    @pl.when(jnp.logical_not(used) & (f == 0))
    def _():
        o_ref[...] = jnp.zeros_like(o_ref)

    @pl.when(used & (i == n_tiles - 1) & (f == nf - 1))
    def _():
        wait_stage()


def _experts(xn, pos, te, nv, nu, wg, wu, wd):
    dff = wg.shape[2]
    nf = dff // MOE_TF
    nt = te.shape[0]
    stage_rows = nf * MOE_ISSUE
    assert stage_rows >= MOE_TM
    src_rows = nt * MOE_TM

    def f_eff(i, f, nu):
        return jnp.where(i < nu[0], f, nf - 1)

    return pl.pallas_call(
        _expert_kernel,
        out_shape=jax.ShapeDtypeStruct((nt * MOE_TM, D), F32),
        grid_spec=pltpu.PrefetchScalarGridSpec(
            num_scalar_prefetch=4,
            grid=(nt, nf),
            in_specs=[pl.BlockSpec(memory_space=pl.ANY),
                      pl.BlockSpec((None, D, MOE_TF), lambda i, f, te, nv, nu, pos: (te[i], 0, f_eff(i, f, nu))),
                      pl.BlockSpec((None, D, MOE_TF), lambda i, f, te, nv, nu, pos: (te[i], 0, f_eff(i, f, nu))),
                      pl.BlockSpec((None, MOE_TF, D), lambda i, f, te, nv, nu, pos: (te[i], f_eff(i, f, nu), 0))],
            out_specs=pl.BlockSpec((MOE_TM, D), lambda i, f, te, nv, nu, pos: (i, 0)),
            scratch_shapes=[pltpu.VMEM((MOE_TM, D), BF16), pltpu.VMEM((stage_rows, D), F32),
                            pltpu.VMEM((D, MOE_TF), BF16), pltpu.VMEM((D, MOE_TF), BF16),
                            pltpu.VMEM((MOE_TF, D), BF16), pltpu.SMEM((src_rows,), jnp.int32),
                            pltpu.SemaphoreType.DMA(())]),
        compiler_params=pltpu.CompilerParams(vmem_limit_bytes=MOE_VMEM_LIMIT),
        name="moe_experts",
    )(te, nv, nu, pos, xn, wg, wu, wd)


def _combine_kernel(pos_ref, h_ref, rw_ref, gate_ref, gfin_ref, y_hbm, o_ref, buf_ref, sem):
    i = pl.program_id(0)
    r_blk = h_ref.shape[0]
    slot = i % 2

    def issue_tile(tile, s):
        def issue(r, carry):
            t = tile * r_blk + r
            _row_copy(y_hbm, pos_ref[2 * t], buf_ref.at[s, 0], r, sem.at[s]).start()
            _row_copy(y_hbm, pos_ref[2 * t + 1], buf_ref.at[s, 1], r, sem.at[s]).start()
            return carry

        lax.fori_loop(0, r_blk, issue, 0, unroll=8)

    @pl.when(i == 0)
    def _():
        issue_tile(0, 0)

    @pl.when(i + 1 < pl.num_programs(0))
    def _():
        issue_tile(i + 1, 1 - slot)

    def wait(r, carry):
        _row_copy(y_hbm, 0, buf_ref.at[slot, 0], r, sem.at[slot]).wait()
        _row_copy(y_hbm, 0, buf_ref.at[slot, 1], r, sem.at[slot]).wait()
        return carry

    lax.fori_loop(0, r_blk, wait, 0, unroll=8)
    y = rw_ref[:, 0:1] * buf_ref[slot, 0] + rw_ref[:, 1:2] * buf_ref[slot, 1]
    hn = h_ref[...] + gate_ref[...] * y
    o_ref[...] = _rms(hn, gfin_ref[...])


def _combine(pos, h, rw, modr, row_fn, gfin, ys):
    m = h.shape[0]
    r = GATHER_R
    return pl.pallas_call(
        _combine_kernel,
        out_shape=jax.ShapeDtypeStruct((m, D), F32),
        grid_spec=pltpu.PrefetchScalarGridSpec(
            num_scalar_prefetch=1,
            grid=(m // r,),
            in_specs=[pl.BlockSpec((r, D), lambda i, pos: (i, 0)),
                      pl.BlockSpec((r, LANE), lambda i, pos: (i, 0)),
                      pl.BlockSpec((None, None, 1, D), lambda i, pos: (row_fn(i), 5, 0, 0)),
                      pl.BlockSpec((1, D), lambda i, pos: (0, 0)),
                      pl.BlockSpec(memory_space=pl.ANY)],
            out_specs=pl.BlockSpec((r, D), lambda i, pos: (i, 0)),
            scratch_shapes=[pltpu.VMEM((2, 2, r, D), F32), pltpu.SemaphoreType.DMA((2,))]),
        name="moe_combine",
    )(pos, h, rw, modr, gfin, ys)


def _moe(xn, h, ri, rw, cnt, modr, row_fn, gfin, wg, wu, wd):
    m = h.shape[0]
    tm = MOE_TM
    nt = 2 * m // tm + N_EXPERTS
    counts = cnt[0, :N_EXPERTS].astype(jnp.int32)
    nt_e = (counts + tm - 1) // tm
    t_end = jnp.cumsum(nt_e)
    t_start = t_end - nt_e
    n_used = t_end[-1]
    tid = jnp.arange(nt, dtype=jnp.int32)
    te_raw = jnp.sum((tid[:, None] >= t_end[None, :]).astype(jnp.int32), axis=1)
    te_last = jnp.sum((n_used - 1 >= t_end).astype(jnp.int32))
    te = jnp.minimum(te_raw, te_last).astype(jnp.int32)
    nv = jnp.where(tid < n_used, jnp.clip(counts[te] - (tid - t_start[te]) * tm, 0, tm), 0).astype(jnp.int32)
    pos = (t_start[ri[:, 0:2]] * tm + ri[:, 2:4]).astype(jnp.int32).reshape(-1)
    nu = n_used.reshape(1).astype(jnp.int32)
    ys = _experts(xn, pos, te, nv, nu, wg, wu, wd)
    return _combine(pos, h, rw, modr, row_fn, gfin, ys)


def _relayout_main_kernel(wt_ref, main_ref):
    main_ref[...] = wt_ref[0].T.astype(BF16)


def _relayout_small_kernel(dt_ref, glr_ref, small_ref, buf_ref):
    buf_ref[...] = jnp.zeros_like(buf_ref)
    buf_ref[0:N_DT, :] = dt_ref[0]
    buf_ref[N_DT:N_DT + N_GLR, :] = glr_ref[0]
    small_ref[...] = buf_ref[...].T.astype(BF16)


def _relayout_w_in(w_in):
    depth, rows, cols = w_in.shape
    wt = jnp.swapaxes(w_in, 1, 2)
    tc = RELAYOUT_TC
    shifted = COL_DT // tc
    assert P_MAIN // tc <= 2 * shifted
    src_row = lambda k: pl.multiple_of(k * tc + N_DT * (k // shifted), N_DT)
    main = pl.pallas_call(
        _relayout_main_kernel,
        out_shape=jax.ShapeDtypeStruct((depth, rows, P_MAIN), BF16),
        grid=(depth, P_MAIN // tc),
        in_specs=[pl.BlockSpec((pl.Element(1), pl.Element(tc), pl.Element(rows)), lambda l, k: (l, src_row(k), 0))],
        out_specs=pl.BlockSpec((None, rows, tc), lambda l, k: (l, 0, k)),
        name="relayout_w_in",
    )(wt)
    small = pl.pallas_call(
        _relayout_small_kernel,
        out_shape=jax.ShapeDtypeStruct((depth, rows, LANE), BF16),
        grid=(depth,),
        in_specs=[pl.BlockSpec((pl.Element(1), pl.Element(N_DT), pl.Element(rows)), lambda l: (l, COL_DT, 0)),
                  pl.BlockSpec((pl.Element(1), pl.Element(N_GLR), pl.Element(rows)), lambda l: (l, COL_GLR, 0))],
        out_specs=pl.BlockSpec((None, rows, LANE), lambda l: (l, 0, 0)),
        scratch_shapes=[pltpu.VMEM((LANE, rows), F32)],
        name="relayout_w_small",
    )(wt, wt)
    return main, small


def _head_row(v, direction):
    lo = SSD_HEADS * direction
    return jnp.pad(v.astype(F32).reshape(1, SSD_HEADS), ((0, 0), (lo, LANE - SSD_HEADS - lo)))


def kernel(x, c, ctx, c_ctx, w_ada, b_ada, g_mix, g_ffn, w_in, w_out, sc_conv_w, cm_w_s, cm_b_s, mb_conv_w,
           mb_conv_b, mb_a_log, mb_dt_bias, mb_d, mb_norm_g, gla_w_gate, gla_b_gate, gla_norm_g, ffn_w_gate,
           ffn_w_up, ffn_w_down, moe_router, moe_w_gate, moe_w_up, moe_w_down, g_final):
    nb, seq, _ = x.shape
    ctx_len = ctx.shape[1]
    depth = w_ada.shape[0]
    assert depth == 2, "supported stack: a dense-FFN layer followed by a last, routed-FFN layer"
    m_lat = nb * seq
    h = x.reshape(m_lat, D)
    hc = ctx.reshape(nb * ctx_len, D)

    cond = jnp.concatenate([c, c_ctx[None, :]], axis=0)
    cb = jnp.broadcast_to(cond[:, :, None], (nb + 1, D, LANE))
    mods = _ada(cb, w_ada, b_ada).reshape(depth, 8, N_MOD, 1, D)

    tm_in = IN_TM
    lat_row_in = lambda i: i // (seq // tm_in)
    lat_row_512 = lambda i: i // (seq // OUT_TM)
    lat_row_g = lambda i: i // (seq // GATHER_R)
    ctx_row = lambda i: nb

    head_of_col = np.arange(GW) // SSD_HD
    expand = [jnp.asarray(np.arange(LANE)[:, None] == 8 * d + head_of_col[None, :], dtype=BF16) for d in range(2)]
    w_in_main, w_in_small = _relayout_w_in(w_in)
    out = None
    for i in range(depth):
        last = i == depth - 1
        modr = mods[i]
        g_mix_i = g_mix[i].reshape(1, D)
        g_ffn_i = g_ffn[i].reshape(1, D)
        p, ps = _inproj(h, g_mix_i, modr, lat_row_in, w_in_main, w_in_small, i, tm_in)
        pc, psc = _inproj(hc, g_mix_i, modr, ctx_row, w_in_main, w_in_small, i, nb * ctx_len)

        ws = cm_w_s[i].astype(BF16)
        bs = jnp.repeat(cm_b_s[i].T, 128, axis=1)
        ya = _scm(p, sc_conv_w[i], ws, bs, GRID_W)
        if not last:
            yac = _scm(pc, sc_conv_w[i], ws, bs, ctx_len)

        conv_b = mb_conv_b[i].reshape(1, -1)
        s_zero = jnp.zeros((nb, SSD_Q, GW), F32)
        skip = jnp.repeat(mb_d[i, 0] + mb_d[i, 1], SSD_HD).reshape(1, GW)
        fin = (skip, mb_norm_g[i].reshape(1, GW))
        ssd_c = [(mb_conv_w[i], conv_b, _head_row(mb_a_log[i, d], d), _head_row(mb_dt_bias[i, d], d), expand[d])
                 for d in range(2)]
        y0c, s0c = _ssd(pc, psc, s_zero, ssd_c[0], nb, ctx_len, 0, False)
        y0, _ = _ssd(p, ps, s0c, ssd_c[0], nb, GRID_W, 0, False)
        if last:
            _, s1c = _ssd(pc, psc, s_zero, ssd_c[1], nb, ctx_len, 1, False)
        else:
            ymbc, s1c = _ssd(pc, psc, s_zero, ssd_c[1], nb, ctx_len, 1, True, y0c, fin)
        ymb, _ = _ssd(p, ps, s1c, ssd_c[1], nb, GRID_W, 1, True, y0, fin)

        g_zero = jnp.zeros((nb, GLA_DV, 256), F32)
        gla_c = [(jnp.zeros((LANE, 256), F32).at[16 + 16 * d:32 + 16 * d].set(gla_w_gate[i, d]).astype(BF16),
                  gla_b_gate[i, d].reshape(1, 256)) for d in range(2)]
        ng = gla_norm_g[i].reshape(1, GLA_DV)
        o0c, t0c = _gla(pc, psc, g_zero, gla_c[0], nb, 0, False)
        o0, _ = _gla(p, ps, t0c, gla_c[0], nb, 0, False)
        if last:
            _, t1c = _gla(pc, psc, g_zero, gla_c[1], nb, 1, False)
        else:
            yglac, t1c = _gla(pc, psc, g_zero, gla_c[1], nb, 1, True, o0c, ng)
        ygla = _gla(p, ps, t1c, gla_c[1], nb, 1, True, o0, ng)[0]

        w_out_b = w_out[i].astype(BF16)
        j = i // 2
        if i % 2 == 0:
            hcn, xcn = _outproj(hc, yac, ymbc, yglac, w_out_b, modr, ctx_row, g_ffn_i)
            hc, wg, wu, wd = _ffn(xcn, hcn, modr, ctx_row, ffn_w_gate[j], ffn_w_up[j], ffn_w_down[j],
                                  nb * ctx_len, True)
            hn, xn = _outproj(h, ya, ymb, ygla, w_out_b, modr, lat_row_512, g_ffn_i)
            h = _ffn(xn, hn, modr, lat_row_in, wg, wu, wd, tm_in, False)[0]
        else:
            w_router = jnp.pad(moe_router[j], ((0, 0), (0, LANE - N_EXPERTS)))
            hn, xn, ri, rw, cnt = _outproj(h, ya, ymb, ygla, w_out_b, modr, lat_row_512, g_ffn_i, w_router)
            if last:
                out = _moe(xn, hn, ri, rw, cnt, modr, lat_row_g, g_final.reshape(1, D),
                           moe_w_gate[j], moe_w_up[j], moe_w_down[j])
            else:
                raise NotImplementedError("routed FFN is only implemented as the last layer's channel mixer")
    return out.reshape(nb, seq, D)
```

```python
import functools
import math

import jax
import jax.numpy as jnp
import numpy as np
from jax import lax
from jax.experimental import pallas as pl
from jax.experimental.pallas import tpu as pltpu

F32 = jnp.float32
BF16 = jnp.bfloat16
EPS = 1e-6

D = 2048
GW = D // 4
GRID_W = 64
N_MOD = 6
LANE = 128
BF16_SUBLANES = 16
P_MAIN = 5632
COL_XBC, XBC_W = 3072, 1024
COL_DT, N_DT = 4096, 16
COL_GLR, N_GLR = 5648, 32

SSD_Q = 128
SSD_HEADS = 8
SSD_HD = 64
GLA_C = 64
GLA_HEADS = 4
GLA_DK = 64
GLA_DV = 128
GLA_RANK = 16
N_EXPERTS = 8

FFN_SB = 256
MOE_SB = 272
MOE_TM = 4 * MOE_SB
MOE_TF = 512
MOE_ISSUE = 78
MOE_VMEM_LIMIT = 127 * 512 * 1024
GATHER_R = 256

ADA_TN = 1024
IN_TM = 1024
SCM_T = 512
SCAN_TB = 512
OUT_TM = 512
RELAYOUT_TC = 256


def _silu(x):
    return x * jax.nn.sigmoid(x)


def _softplus(x):
    return jnp.maximum(x, 0.0) + jnp.log1p(jnp.exp(-jnp.abs(x)))


def _split3(x):
    hi = x.astype(BF16)
    r = x - hi.astype(F32)
    mid = r.astype(BF16)
    lo = (r - mid.astype(F32)).astype(BF16)
    return hi, mid, lo


def _dot(a, b):
    return jnp.dot(a, b, preferred_element_type=F32)


def _dot_nt(a, b):
    return lax.dot_general(a, b, (((1,), (1,)), ((), ())), preferred_element_type=F32)


def _dot_tn(a, b):
    return lax.dot_general(a, b, (((0,), (0,)), ((), ())), preferred_element_type=F32)


def _dot01_left(m01, x):
    hi, mid, lo = _split3(x)
    return _dot(m01, hi) + _dot(m01, mid) + _dot(m01, lo)


def _dot01_right(x, m01):
    hi, mid, lo = _split3(x)
    return _dot(hi, m01) + _dot(mid, m01) + _dot(lo, m01)


def _rms(x, g):
    return x * lax.rsqrt(jnp.mean(x * x, axis=-1, keepdims=True) + EPS) * g


def _conv3(x, w, seg):
    n = x.shape[0]
    t = lax.broadcasted_iota(jnp.int32, (n, 1), 0) & (seg - 1)
    prev = jnp.where(t == 0, 0.0, pltpu.roll(x, 1, 0))
    nxt = jnp.where(t == seg - 1, 0.0, pltpu.roll(x, n - 1, 0))
    return prev * w[0:1] + x * w[1:2] + nxt * w[2:3]


def _ada_kernel(cb_ref, w_ref, b_ref, o_ref, s_ref):
    @pl.when((pl.program_id(0) == 0) & (pl.program_id(1) == 0))
    def _():
        s_ref[...] = _silu(cb_ref[...])

    tn = w_ref.shape[1]
    o_ref[...] = jnp.zeros_like(o_ref)
    for j in range(tn // LANE):
        wj = w_ref[:, j * LANE:(j + 1) * LANE]
        for m in range(3):
            o_ref[m:m + 1, j * LANE:(j + 1) * LANE] = (
                jnp.sum(wj * s_ref[m], axis=0, keepdims=True) + b_ref[:, j * LANE:(j + 1) * LANE])


def _ada(cb, w_ada, b_ada):
    depth = w_ada.shape[0]
    n = w_ada.shape[2]
    tn = ADA_TN
    return pl.pallas_call(
        _ada_kernel,
        out_shape=jax.ShapeDtypeStruct((depth, 8, n), F32),
        grid=(depth, n // tn),
        in_specs=[pl.BlockSpec((3, D, LANE), lambda l, j: (0, 0, 0)),
                  pl.BlockSpec((None, D, tn), lambda l, j: (l, 0, j)),
                  pl.BlockSpec((None, 1, tn), lambda l, j: (l, 0, j))],
        out_specs=pl.BlockSpec((None, 8, tn), lambda l, j: (l, 0, j)),
        scratch_shapes=[pltpu.VMEM((3, D, LANE), F32)],
        name="ada",
    )(cb, w_ada, b_ada.reshape(depth, 1, n))


def _inproj_kernel(h_ref, g_ref, sh_ref, sc_ref, w_ref, ws_ref, cw_ref, cb_ref, o_ref, os_ref, xn_ref, *,
                   seg, conv_tile, conv_lo):
    j = pl.program_id(1)

    @pl.when(j == 0)
    def _():
        xn = _rms(h_ref[...], g_ref[...])
        xn_ref[...] = (xn * (1.0 + sc_ref[...]) + sh_ref[...]).astype(BF16)
        os_ref[...] = _dot(xn_ref[...], ws_ref[...])

    @pl.when(j != conv_tile)
    def _():
        o_ref[...] = _dot(xn_ref[...], w_ref[...]).astype(BF16)

    @pl.when(j == conv_tile)
    def _():
        y = _dot(xn_ref[...], w_ref[...])
        hi = conv_lo + XBC_W
        o_ref[:, 0:conv_lo] = y[:, 0:conv_lo].astype(BF16)
        o_ref[:, conv_lo:hi] = _silu(_conv3(y[:, conv_lo:hi], cw_ref[...], seg) + cb_ref[...]).astype(BF16)
        o_ref[:, hi:] = y[:, hi:].astype(BF16)


def _mod_spec(row_fn, k):
    return pl.BlockSpec((None, None, 1, D), lambda i, j: (row_fn(i), k, 0, 0))


def _inproj(h, g, modr, row_fn, w, w_small, conv_w, conv_b, layer, tm, seg):
    m = h.shape[0]
    tn = P_MAIN // 4
    conv_tile, conv_lo = COL_XBC // tn, COL_XBC % tn
    assert conv_lo % LANE == 0 and conv_lo + XBC_W <= tn and tm % seg == 0
    return pl.pallas_call(
        functools.partial(_inproj_kernel, seg=seg, conv_tile=conv_tile, conv_lo=conv_lo),
        out_shape=(jax.ShapeDtypeStruct((m, P_MAIN), BF16), jax.ShapeDtypeStruct((m, LANE), F32)),
        grid=(m // tm, P_MAIN // tn),
        in_specs=[pl.BlockSpec((tm, D), lambda i, j: (i, 0)),
                  pl.BlockSpec((1, D), lambda i, j: (0, 0)),
                  _mod_spec(row_fn, 0), _mod_spec(row_fn, 1),
                  pl.BlockSpec((None, D, tn), lambda i, j: (layer, 0, j)),
                  pl.BlockSpec((None, D, LANE), lambda i, j: (layer, 0, 0)),
                  pl.BlockSpec((3, XBC_W), lambda i, j: (0, 0)),
                  pl.BlockSpec((1, XBC_W), lambda i, j: (0, 0))],
        out_specs=(pl.BlockSpec((tm, tn), lambda i, j: (i, j)),
                   pl.BlockSpec((tm, LANE), lambda i, j: (i, 0))),
        scratch_shapes=[pltpu.VMEM((tm, D), BF16)],
        name="inproj",
    )(h, g, modr, modr, w, w_small, conv_w, conv_b)


def _gelu_tanh(x):
    c = 0.7978845608028654
    return x * (0.5 * (1.0 + jnp.tanh(c * (x + 0.044715 * (x * x * x)))))


def _scm_kernel(sc_ref, u_ref, v_ref, cw_ref, ws_ref, bs_ref, o_ref, *, seg):
    t = sc_ref.shape[0]
    bgate = sc_ref[:, 0:GW].astype(F32)
    gated = sc_ref[:, GW:2 * GW].astype(F32) * sc_ref[:, 2 * GW:3 * GW].astype(F32)
    y_sc = bgate * _conv3(gated, cw_ref[...], seg)
    o_ref[:, 0:GW] = y_sc.astype(BF16)

    u = _gelu_tanh(u_ref[...].astype(F32))
    v = _gelu_tanh(v_ref[...].astype(F32)).astype(BF16)
    for c in range(t // 128):
        rows = slice(c * 128, (c + 1) * 128)
        for g in range(4):
            cols = slice(g * 128, (g + 1) * 128)
            s = _dot(ws_ref[g], v[rows, cols]) + bs_ref[:, cols]
            o_ref[rows, GW + g * 128:GW + (g + 1) * 128] = (u[rows, cols] * s).astype(BF16)


def _scm(p, cw, ws, bs, seg):
    m = p.shape[0]
    t = SCM_T
    return pl.pallas_call(
        functools.partial(_scm_kernel, seg=seg),
        out_shape=jax.ShapeDtypeStruct((m, 2 * GW), BF16),
        grid=(m // t,),
        in_specs=[pl.BlockSpec((t, 3 * GW), lambda i: (i, 0)),
                  pl.BlockSpec((t, GW), lambda i: (i, 3)),
                  pl.BlockSpec((t, GW), lambda i: (i, 4)),
                  pl.BlockSpec((3, GW), lambda i: (0, 0)),
                  pl.BlockSpec((4, 128, 128), lambda i: (0, 0, 0)),
                  pl.BlockSpec((128, GW), lambda i: (0, 0))],
        out_specs=pl.BlockSpec((t, 2 * GW), lambda i: (i, 0)),
        name="scm",
    )(p, p, p, cw, ws, bs)


def _ssd_kernel(*refs, direction, final):
    if final:
        (xbc_ref, sm_ref, alog_ref, dtb_ref, e_ref, s0_ref,
         y0_ref, z_ref, skip_ref, ng_ref, y_ref, sfin_ref, st_ref) = refs
    else:
        (xbc_ref, sm_ref, alog_ref, dtb_ref, e_ref, s0_ref,
         y_ref, sfin_ref, st_ref) = refs
    j = pl.program_id(0)
    nb, tb = xbc_ref.shape[0], xbc_ref.shape[1]
    q = SSD_Q

    @pl.when(j == 0)
    def _():
        st_ref[...] = s0_ref[...]

    a_neg = -jnp.exp(alog_ref[...])
    ii = lax.broadcasted_iota(jnp.int32, (q, q), 0)
    jj = lax.broadcasted_iota(jnp.int32, (q, q), 1)
    mask = (jj <= ii) if direction == 0 else (jj >= ii)
    tri = mask.astype(BF16)
    last = q - 1 if direction == 0 else 0
    lane0 = 8 * direction
    upper_half = lax.broadcasted_iota(jnp.int32, (q, 128), 1) >= SSD_HD
    expand = e_ref[...]

    xcs = [xbc_ref[b].astype(F32) for b in range(nb)]
    states = [st_ref[b] for b in range(nb)]

    def chunk(b, c, st):
        rows = slice(c * q, (c + 1) * q)
        xc = xcs[b]
        dt = _softplus(sm_ref[b, rows, :] + dtb_ref[...])
        acs = _dot01_left(tri, dt * a_neg)
        tot = acs[last:last + 1, :]
        acs_t = acs.T
        dt_t = dt.T
        wx = _dot01_right(jnp.exp(tot - acs) * dt, expand)
        ea = _dot01_right(jnp.exp(acs), expand)
        xs = xc[rows, 0:GW]
        xw = (wx * xs).astype(BF16)
        stb = st.astype(BF16)
        ys = []
        new_st = []
        for g in range(2):
            bg = xc[rows, GW + g * 128:GW + (g + 1) * 128].astype(BF16)
            cg = xc[rows, GW + 256 + g * 128:GW + 256 + (g + 1) * 128].astype(BF16)
            cb = _dot_nt(cg, bg)
            gcols = slice(g * 256, (g + 1) * 256)
            yint = _dot(cg, stb[:, gcols])
            for pr in range(2):
                pidx = 2 * g + pr
                pcols = slice(pidx * 128, (pidx + 1) * 128)
                xp = xs[:, pcols]
                acc = yint[:, pr * 128:(pr + 1) * 128] * ea[:, pcols]
                for hh in range(2):
                    ln = lane0 + 2 * pidx + hh
                    sg = acs[:, ln:ln + 1] - acs_t[ln:ln + 1, :]
                    decay = jnp.where(mask, jnp.exp(jnp.where(mask, sg, 0.0)), 0.0)
                    scores = (cb * decay * dt_t[ln:ln + 1, :]).astype(BF16)
                    xh = jnp.where(upper_half == (hh == 1), xp, 0.0).astype(BF16)
                    acc = acc + _dot(scores, xh)
                ys.append(acc)
            new_st.append(ea[last:last + 1, gcols] * st[:, gcols] + _dot_tn(bg, xw[:, gcols]))
        y = jnp.concatenate(ys, axis=1)
        if final:
            y = y0_ref[b, rows, :] + y + xs * skip_ref[...]
            y = y * _silu(z_ref[b, rows, :].astype(F32))
            y_ref[b, rows, :] = _rms(y, ng_ref[...]).astype(y_ref.dtype)
        else:
            y_ref[b, rows, :] = y
        return jnp.concatenate(new_st, axis=1)

    nchunk = tb // q
    order = range(nchunk) if direction == 0 else range(nchunk - 1, -1, -1)
    for c in order:
        for b in range(nb):
            states[b] = chunk(b, c, states[b])
    for b in range(nb):
        st_ref[b] = states[b]

    @pl.when(j == pl.num_programs(0) - 1)
    def _():
        sfin_ref[...] = st_ref[...]


def _scan_block_map(nblk, direction):
    return (lambda j: j) if direction == 0 else (lambda j: nblk - 1 - j)


def _ssd(p, ps, s0, consts, nb, direction, final, y0=None, fin=None):
    m = p.shape[0]
    seq = m // nb
    tb = min(SCAN_TB, seq)
    nblk = seq // tb
    blk = _scan_block_map(nblk, direction)
    p3 = p.reshape(nb, seq, P_MAIN)
    full = lambda shape: pl.BlockSpec(shape, lambda j: (0,) * len(shape))
    rows = lambda width, col: pl.BlockSpec((nb, tb, width), lambda j: (0, blk(j), col))
    in_specs = [rows(1024, 3), rows(LANE, 0),
                full((1, LANE)), full((1, LANE)), full((LANE, GW)),
                full((nb, SSD_Q, GW))]
    args = [p3, ps.reshape(nb, seq, LANE), *consts, s0]
    if final:
        in_specs += [rows(GW, 0), rows(GW, 5), full((1, GW)), full((1, GW))]
        args += [y0.reshape(nb, seq, GW), p3, *fin]
    y, s_fin = pl.pallas_call(
        functools.partial(_ssd_kernel, direction=direction, final=final),
        out_shape=(jax.ShapeDtypeStruct((nb, seq, GW), BF16 if final else F32),
                   jax.ShapeDtypeStruct((nb, SSD_Q, GW), F32)),
        grid=(nblk,),
        in_specs=in_specs,
        out_specs=(rows(GW, 0), full((nb, SSD_Q, GW))),
        scratch_shapes=[pltpu.VMEM((nb, SSD_Q, GW), F32)],
        name=f"ssd_d{direction}",
    )(*args)
    return y.reshape(m, GW), s_fin


def _gla_kernel(*refs, direction, final):
    if final:
        (q_ref, k_ref, v_ref, sm_ref, wg_ref, bg_ref, s0_ref, o0_ref, g_ref, ng_ref,
         o_ref, sfin_ref, st_ref) = refs
    else:
        (q_ref, k_ref, v_ref, sm_ref, wg_ref, bg_ref, s0_ref, o_ref, sfin_ref, st_ref) = refs
    j = pl.program_id(0)
    nb, tb = q_ref.shape[0], q_ref.shape[1]
    c_len = GLA_C

    @pl.when(j == 0)
    def _():
        st_ref[...] = s0_ref[...]

    ii = lax.broadcasted_iota(jnp.int32, (tb, tb), 0)
    jj = lax.broadcasted_iota(jnp.int32, (tb, tb), 1)
    same = (ii >> 6) == (jj >> 6)
    btri = (same & ((jj <= ii) if direction == 0 else (jj >= ii))).astype(BF16)
    gcums = []
    for b in range(nb):
        gk = -_softplus(-(_dot(sm_ref[b].astype(BF16), wg_ref[...]) + bg_ref[...])) * (1.0 / 16.0)
        gcums.append(_dot01_left(btri, gk))

    hs = GLA_HEADS * c_len
    ri = lax.broadcasted_iota(jnp.int32, (hs, hs), 0)
    ci = lax.broadcasted_iota(jnp.int32, (hs, hs), 1)
    own_lanes = (ri >> 6) == (ci >> 6)
    ti, tj = ri & (c_len - 1), ci & (c_len - 1)
    own_causal = own_lanes & ((tj <= ti) if direction == 0 else (tj >= ti))
    last = c_len - 1 if direction == 0 else 0
    ref = c_len // 2 if direction == 0 else c_len - 1 - c_len // 2

    def stack_heads(x):
        return jnp.where(own_lanes, jnp.concatenate([x] * GLA_HEADS, axis=0), 0.0).astype(BF16)

    states = [st_ref[b] for b in range(nb)]

    def chunk(b, c, st):
        rows = slice(c * c_len, (c + 1) * c_len)
        gc = gcums[b][rows, :]
        gl = gc[last:last + 1, :]
        gr = gc[ref:ref + 1, :]
        qc = q_ref[b, rows, :].astype(F32) * (GLA_DK ** -0.5)
        kc = k_ref[b, rows, :].astype(F32)
        vc = v_ref[b, rows, :]
        v_rows = jnp.concatenate([vc[:, h * GLA_DV:(h + 1) * GLA_DV] for h in range(GLA_HEADS)], axis=0)
        kb = (kc * jnp.exp(gr - gc)).astype(BF16)
        att = _dot_nt(stack_heads(qc * jnp.exp(gc - gr)), jnp.concatenate([kb] * GLA_HEADS, axis=0))
        att = jnp.where(own_causal, att, 0.0).astype(BF16)
        y = _dot(att, v_rows) + _dot_nt(stack_heads(qc * jnp.exp(gc)), st.astype(BF16))
        contrib = _dot_tn(v_rows, stack_heads(kc * jnp.exp(gl - gc)))
        new_st = jnp.exp(gl) * st + contrib
        o = jnp.concatenate([y[h * c_len:(h + 1) * c_len, :] for h in range(GLA_HEADS)], axis=1)
        if final:
            o = o0_ref[b, rows, :] + o
            normed = [_rms(o[:, h * GLA_DV:(h + 1) * GLA_DV], ng_ref[...]) for h in range(GLA_HEADS)]
            gate = _silu(g_ref[b, rows, :].astype(F32))
            o_ref[b, rows, :] = (jnp.concatenate(normed, axis=1) * gate).astype(o_ref.dtype)
        else:
            o_ref[b, rows, :] = o
        return new_st

    nchunk = tb // c_len
    order = range(nchunk) if direction == 0 else range(nchunk - 1, -1, -1)
    for c in order:
        for b in range(nb):
            states[b] = chunk(b, c, states[b])
    for b in range(nb):
        st_ref[b] = states[b]

    @pl.when(j == pl.num_programs(0) - 1)
    def _():
        sfin_ref[...] = st_ref[...]


def _gla(p, ps, s0, consts, nb, direction, final, o0=None, ng=None):
    m = p.shape[0]
    seq = m // nb
    tb = min(SCAN_TB, seq)
    nblk = seq // tb
    blk = _scan_block_map(nblk, direction)
    p3 = p.reshape(nb, seq, P_MAIN)
    full = lambda shape: pl.BlockSpec(shape, lambda j: (0,) * len(shape))
    rows = lambda width, col: pl.BlockSpec((nb, tb, width), lambda j: (0, blk(j), col))
    in_specs = [rows(256, 16), rows(256, 17), rows(GW, 9), rows(LANE, 0),
                full((LANE, 256)), full((1, 256)), full((nb, GLA_DV, 256))]
    args = [p3, p3, p3, ps.reshape(nb, seq, LANE), *consts, s0]
    if final:
        in_specs += [rows(GW, 0), rows(GW, 10), full((1, GLA_DV))]
        args += [o0.reshape(nb, seq, GW), p3, ng]
    o, s_fin = pl.pallas_call(
        functools.partial(_gla_kernel, direction=direction, final=final),
        out_shape=(jax.ShapeDtypeStruct((nb, seq, GW), BF16 if final else F32),
                   jax.ShapeDtypeStruct((nb, GLA_DV, 256), F32)),
        grid=(nblk,),
        in_specs=in_specs,
        out_specs=(rows(GW, 0), full((nb, GLA_DV, 256))),
        scratch_shapes=[pltpu.VMEM((nb, GLA_DV, 256), F32)],
        name=f"gla_d{direction}",
    )(*args)
    return o.reshape(m, GW), s_fin


def _outproj_kernel(*refs, route):
    if route:
        (h_ref, ya_ref, ymb_ref, ygla_ref, w_ref, gate_ref, g_ref, sh_ref, sc_ref, wr_ref,
         hn_ref, xn_ref, ri_ref, rw_ref, cnt_ref, carry_ref, xprev_ref) = refs
    else:
        (h_ref, ya_ref, ymb_ref, ygla_ref, w_ref, gate_ref, g_ref, sh_ref, sc_ref,
         hn_ref, xn_ref) = refs

    def project():
        acc = (_dot(ya_ref[...], w_ref[0:2 * GW, :]) + _dot(ymb_ref[...], w_ref[2 * GW:3 * GW, :])
               + _dot(ygla_ref[...], w_ref[3 * GW:4 * GW, :]))
        hn = h_ref[...] + gate_ref[...] * acc
        hn_ref[...] = hn
        xn = _rms(hn, g_ref[...]) * (1.0 + sc_ref[...]) + sh_ref[...]
        xn_ref[...] = xn.astype(xn_ref.dtype)
        return xn

    if not route:
        project()
        return

    i = pl.program_id(0)
    n_tiles = pl.num_programs(0) - 1

    def route_prev():
        _route_tile(xprev_ref[...], wr_ref, ri_ref, rw_ref, cnt_ref, carry_ref)

    @pl.when(i == 0)
    def _():
        carry_ref[...] = jnp.zeros_like(carry_ref)
        xprev_ref[...] = project()

    @pl.when((i > 0) & (i < n_tiles))
    def _():
        route_prev()
        xprev_ref[...] = project()

    @pl.when(i == n_tiles)
    def _():
        route_prev()


def _route_tile(xn, wr_ref, ri_ref, rw_ref, cnt_ref, carry_ref):
    tm = xn.shape[0]
    xh = xn.astype(BF16)
    xl = (xn - xh.astype(F32)).astype(BF16)
    wr = wr_ref[...]
    wh = wr.astype(BF16)
    wl = (wr - wh.astype(F32)).astype(BF16)
    hh_hl = _dot(xh, jnp.concatenate([wh, wl], axis=1))
    logits = hh_hl[:, 0:LANE] + hh_hl[:, LANE:2 * LANE] + _dot(xl, wh)
    lane = lax.broadcasted_iota(jnp.int32, (tm, LANE), 1)
    lane_f = lane.astype(F32)
    neg = jnp.float32(-3.0e38)
    lm = jnp.where(lane < N_EXPERTS, logits, neg)
    m1 = jnp.max(lm, axis=-1, keepdims=True)
    i1 = jnp.min(jnp.where(lm == m1, lane_f, float(LANE)), axis=-1, keepdims=True)
    lm2 = jnp.where(lane_f == i1, neg, lm)
    m2 = jnp.max(lm2, axis=-1, keepdims=True)
    i2 = jnp.min(jnp.where(lm2 == m2, lane_f, float(LANE)), axis=-1, keepdims=True)
    e = jnp.exp(m2 - m1)
    w1 = 1.0 / (1.0 + e)
    w2 = e / (1.0 + e)
    sel1 = lane_f == i1
    sel2 = lane_f == i2
    onehot = jnp.where(sel1 | sel2, 1.0, 0.0)
    ti = lax.broadcasted_iota(jnp.int32, (tm, tm), 0)
    tj = lax.broadcasted_iota(jnp.int32, (tm, tm), 1)
    incl = _dot((tj <= ti).astype(BF16), onehot.astype(BF16)) + carry_ref[...]
    excl = incl - onehot
    r1 = jnp.sum(jnp.where(sel1, excl, 0.0), axis=-1, keepdims=True)
    r2 = jnp.sum(jnp.where(sel2, excl, 0.0), axis=-1, keepdims=True)
    info = jnp.where(lane == 0, i1, jnp.where(lane == 1, i2, jnp.where(lane == 2, r1, jnp.where(lane == 3, r2, 0.0))))
    ri_ref[...] = info.astype(jnp.int32)
    rw_ref[...] = jnp.where(lane == 0, w1, jnp.where(lane == 1, w2, 0.0))
    carry_ref[...] = incl[tm - 1:tm, :]
    cnt_ref[...] = incl[tm - 1:tm, :]


def _outproj(h, ya, ymb, ygla, w, modr, row_fn, g, w_router=None):
    m = h.shape[0]
    tm = OUT_TM
    route = w_router is not None
    n_tiles = m // tm
    cur = (lambda i: jnp.minimum(i, n_tiles - 1)) if route else (lambda i: i)
    prev = lambda i: jnp.maximum(i - 1, 0)
    mod = lambda k: pl.BlockSpec((None, None, 1, D), lambda i: (row_fn(cur(i)), k, 0, 0))
    in_specs = [pl.BlockSpec((tm, D), lambda i: (cur(i), 0)),
                pl.BlockSpec((tm, 2 * GW), lambda i: (cur(i), 0)),
                pl.BlockSpec((tm, GW), lambda i: (cur(i), 0)),
                pl.BlockSpec((tm, GW), lambda i: (cur(i), 0)),
                pl.BlockSpec((D, D), lambda i: (0, 0)),
                mod(2), pl.BlockSpec((1, D), lambda i: (0, 0)), mod(3), mod(4)]
    args = [h, ya, ymb, ygla, w, modr, g, modr, modr]
    out_shape = [jax.ShapeDtypeStruct((m, D), F32), jax.ShapeDtypeStruct((m, D), F32 if route else BF16)]
    out_specs = [pl.BlockSpec((tm, D), lambda i: (cur(i), 0)), pl.BlockSpec((tm, D), lambda i: (cur(i), 0))]
    scratch = []
    if route:
        in_specs.append(pl.BlockSpec((D, LANE), lambda i: (0, 0)))
        args.append(w_router)
        out_shape += [jax.ShapeDtypeStruct((m, LANE), jnp.int32), jax.ShapeDtypeStruct((m, LANE), F32),
                      jax.ShapeDtypeStruct((1, LANE), F32)]
        out_specs += [pl.BlockSpec((tm, LANE), lambda i: (prev(i), 0)), pl.BlockSpec((tm, LANE), lambda i: (prev(i), 0)),
                      pl.BlockSpec((1, LANE), lambda i: (0, 0))]
        scratch = [pltpu.VMEM((1, LANE), F32), pltpu.VMEM((tm, D), F32)]
    return pl.pallas_call(
        functools.partial(_outproj_kernel, route=route),
        out_shape=tuple(out_shape),
        grid=(n_tiles + 1 if route else n_tiles,),
        in_specs=in_specs,
        out_specs=tuple(out_specs),
        scratch_shapes=scratch,
        name="outproj_route" if route else "outproj",
    )(*args)


def _swiglu_rows(x_ref, wg_ref, wu_ref, wd_ref, o_ref, wgb_ref, wub_ref, wdb_ref, sb_rows, nsb, gate,
                 prologue=None):
    def sub_block(sb):
        start = sb * sb_rows
        return pl.ds(start if isinstance(sb, int) else pl.multiple_of(start, BF16_SUBLANES), sb_rows)

    def up(sb):
        x = x_ref[sub_block(sb), :]
        return (_silu(_dot(x, wgb_ref[...])) * _dot(x, wub_ref[...])).astype(BF16)

    def down(sb, hidden):
        rows = sub_block(sb)
        y = _dot(hidden, wdb_ref[...])
        o_ref[rows, :] += y if gate is None else gate * y

    if prologue is not None:
        prologue()
    if wg_ref is not None:
        wgb_ref[...] = wg_ref[...].astype(BF16)
        wub_ref[...] = wu_ref[...].astype(BF16)
    hidden = up(0)
    if wd_ref is not None:
        wdb_ref[...] = wd_ref[...].astype(BF16)
    if isinstance(nsb, int):
        for sb in range(1, nsb):
            nxt = up(sb)
            down(sb - 1, hidden)
            hidden = nxt
    else:
        def body(sb, hid):
            nxt = up(sb)
            down(sb - 1, hid)
            return nxt

        hidden = lax.fori_loop(1, nsb, body, hidden)
    down(nsb - 1, hidden)


def _ffn_kernel(*refs, emit):
    if emit:
        x_ref, h_hbm, gate_ref, wg_ref, wu_ref, wd_ref, o_ref, wgb_ref, wub_ref, wdb_ref = refs
    else:
        x_ref, h_hbm, gate_ref, wgb_ref, wub_ref, wdb_ref, o_ref = refs
        wg_ref = wu_ref = wd_ref = None
    tm = o_ref.shape[0]

    @pl.when(pl.program_id(1) == 0)
    def _():
        pltpu.sync_copy(h_hbm.at[pl.ds(pl.program_id(0) * tm, tm)], o_ref)

    _swiglu_rows(x_ref, wg_ref, wu_ref, wd_ref, o_ref, wgb_ref, wub_ref, wdb_ref, FFN_SB, tm // FFN_SB,
                 gate_ref[...])


def _ffn(xn, h, modr, row_fn, wg, wu, wd, tm, emit):
    m = h.shape[0]
    tf = MOE_TF
    dff = wg.shape[1]
    assert not emit or m == tm
    w_specs = [pl.BlockSpec((D, tf), lambda i, f: (0, f)),
               pl.BlockSpec((D, tf), lambda i, f: (0, f)),
               pl.BlockSpec((tf, D), lambda i, f: (f, 0))]
    out_shape = [jax.ShapeDtypeStruct((m, D), F32)]
    out_specs = [pl.BlockSpec((tm, D), lambda i, f: (i, 0))]
    if emit:
        out_shape += [jax.ShapeDtypeStruct((D, dff), BF16), jax.ShapeDtypeStruct((D, dff), BF16),
                      jax.ShapeDtypeStruct((dff, D), BF16)]
        out_specs += w_specs
    return pl.pallas_call(
        functools.partial(_ffn_kernel, emit=emit),
        out_shape=tuple(out_shape),
        grid=(m // tm, dff // tf),
        in_specs=[pl.BlockSpec((tm, D), lambda i, f: (i, 0)),
                  pl.BlockSpec(memory_space=pl.ANY),
                  pl.BlockSpec((None, None, 1, D), lambda i, f: (row_fn(i), 5, 0, 0))] + w_specs,
        out_specs=tuple(out_specs),
        name="ffn_emit" if emit else "ffn",
    )(xn, h, modr, wg, wu, wd)


def _row_copy(src_hbm, row, dst, r, sem):
    return pltpu.make_async_copy(src_hbm.at[pl.ds(row, 1)], dst.at[pl.ds(r, 1)], sem)


def _expert_kernel(te_ref, nv_ref, nu_ref, pos_ref, xn_hbm, wg_ref, wu_ref, wd_ref, o_ref,
                   xb_ref, stage_ref, wgb_ref, wub_ref, wdb_ref, src_ref, sem):
    i = pl.program_id(0)
    f = pl.program_id(1)
    n_tiles = pl.num_programs(0)
    nf = pl.num_programs(1)
    used = i < nu_ref[0]
    stage_rows = stage_ref.shape[0]
    unroll = math.gcd(stage_rows, 8)

    def src_token(tile, r):
        tile = jnp.minimum(tile, n_tiles - 1)
        routed = nv_ref[tile]
        row = jnp.where(routed > 0, tile * MOE_TM + jnp.minimum(r, routed - 1), 0)
        return src_ref[row]

    def wait_stage():
        def wait(r, carry):
            _row_copy(xn_hbm, 0, stage_ref, r, sem).wait()
            return carry

        lax.fori_loop(0, stage_rows, wait, 0, unroll=unroll)

    @pl.when(f == 0)
    def _():
        @pl.when(i == 0)
        def _():
            def invert(t, carry):
                src_ref[pos_ref[2 * t]] = t
                src_ref[pos_ref[2 * t + 1]] = t
                return carry

            lax.fori_loop(0, pos_ref.shape[0] // 2, invert, 0, unroll=8)

            def issue(r, carry):
                _row_copy(xn_hbm, src_token(0, r), stage_ref, r, sem).start()
                return carry

            lax.fori_loop(0, stage_rows, issue, 0, unroll=unroll)

        @pl.when(i <= nu_ref[0])
        def _():
            wait_stage()
            xb_ref[...] = stage_ref[0:MOE_TM, :].astype(BF16)

    def issue_next_rows():
        for k in range(MOE_ISSUE):
            r = f * MOE_ISSUE + k
            _row_copy(xn_hbm, src_token(i + 1, r), stage_ref, r, sem).start()

    @pl.when(used)
    def _():
        @pl.when(f == 0)
        def _():
            o_ref[...] = jnp.zeros_like(o_ref)

        nsb = (nv_ref[i] + MOE_SB - 1) // MOE_SB
        refs = (xb_ref, wg_ref, wu_ref, wd_ref, o_ref, wgb_ref, wub_ref, wdb_ref)

        @pl.when(nsb == MOE_TM // MOE_SB)
        def _():
            _swiglu_rows(*refs, MOE_SB, MOE_TM // MOE_SB, None, issue_next_rows)

        @pl.when(nsb < MOE_TM // MOE_SB)
        def _():
            _swiglu_rows(*refs, MOE_SB, nsb, None, issue_next_rows)

    @pl.when(jnp.logical_not(used) & (f == 0))
    def _():
        o_ref[...] = jnp.zeros_like(o_ref)

    @pl.when(used & (i == n_tiles - 1) & (f == nf - 1))
    def _():
        wait_stage()


def _experts(xn, pos, te, nv, nu, wg, wu, wd):
    dff = wg.shape[2]
    nf = dff // MOE_TF
    nt = te.shape[0]
    stage_rows = nf * MOE_ISSUE
    assert stage_rows >= MOE_TM
    src_rows = nt * MOE_TM

    def f_eff(i, f, nu):
        return jnp.where(i < nu[0], f, nf - 1)

    return pl.pallas_call(
        _expert_kernel,
        out_shape=jax.ShapeDtypeStruct((nt * MOE_TM, D), F32),
        grid_spec=pltpu.PrefetchScalarGridSpec(
            num_scalar_prefetch=4,
            grid=(nt, nf),
            in_specs=[pl.BlockSpec(memory_space=pl.ANY),
                      pl.BlockSpec((None, D, MOE_TF), lambda i, f, te, nv, nu, pos: (te[i], 0, f_eff(i, f, nu))),
                      pl.BlockSpec((None, D, MOE_TF), lambda i, f, te, nv, nu, pos: (te[i], 0, f_eff(i, f, nu))),
                      pl.BlockSpec((None, MOE_TF, D), lambda i, f, te, nv, nu, pos: (te[i], f_eff(i, f, nu), 0))],
            out_specs=pl.BlockSpec((MOE_TM, D), lambda i, f, te, nv, nu, pos: (i, 0)),
            scratch_shapes=[pltpu.VMEM((MOE_TM, D), BF16), pltpu.VMEM((stage_rows, D), F32),
                            pltpu.VMEM((D, MOE_TF), BF16), pltpu.VMEM((D, MOE_TF), BF16),
                            pltpu.VMEM((MOE_TF, D), BF16), pltpu.SMEM((src_rows,), jnp.int32),
                            pltpu.SemaphoreType.DMA(())]),
        compiler_params=pltpu.CompilerParams(vmem_limit_bytes=MOE_VMEM_LIMIT),
        name="moe_experts",
    )(te, nv, nu, pos, xn, wg, wu, wd)


def _combine_kernel(pos_ref, h_ref, rw_ref, gate_ref, gfin_ref, y_hbm, o_ref, buf_ref, sem):
    i = pl.program_id(0)
    r_blk = h_ref.shape[0]
    slot = i % 2

    def issue_tile(tile, s):
        def issue(r, carry):
            t = tile * r_blk + r
            _row_copy(y_hbm, pos_ref[2 * t], buf_ref.at[s, 0], r, sem.at[s]).start()
            _row_copy(y_hbm, pos_ref[2 * t + 1], buf_ref.at[s, 1], r, sem.at[s]).start()
            return carry

        lax.fori_loop(0, r_blk, issue, 0, unroll=8)

    @pl.when(i == 0)
    def _():
        issue_tile(0, 0)

    @pl.when(i + 1 < pl.num_programs(0))
    def _():
        issue_tile(i + 1, 1 - slot)

    def wait(r, carry):
        _row_copy(y_hbm, 0, buf_ref.at[slot, 0], r, sem.at[slot]).wait()
        _row_copy(y_hbm, 0, buf_ref.at[slot, 1], r, sem.at[slot]).wait()
        return carry

    lax.fori_loop(0, r_blk, wait, 0, unroll=8)
    y = rw_ref[:, 0:1] * buf_ref[slot, 0] + rw_ref[:, 1:2] * buf_ref[slot, 1]
    hn = h_ref[...] + gate_ref[...] * y
    o_ref[...] = _rms(hn, gfin_ref[...])


def _combine(pos, h, rw, modr, row_fn, gfin, ys):
    m = h.shape[0]
    r = GATHER_R
    return pl.pallas_call(
        _combine_kernel,
        out_shape=jax.ShapeDtypeStruct((m, D), F32),
        grid_spec=pltpu.PrefetchScalarGridSpec(
            num_scalar_prefetch=1,
            grid=(m // r,),
            in_specs=[pl.BlockSpec((r, D), lambda i, pos: (i, 0)),
                      pl.BlockSpec((r, LANE), lambda i, pos: (i, 0)),
                      pl.BlockSpec((None, None, 1, D), lambda i, pos: (row_fn(i), 5, 0, 0)),
                      pl.BlockSpec((1, D), lambda i, pos: (0, 0)),
                      pl.BlockSpec(memory_space=pl.ANY)],
            out_specs=pl.BlockSpec((r, D), lambda i, pos: (i, 0)),
            scratch_shapes=[pltpu.VMEM((2, 2, r, D), F32), pltpu.SemaphoreType.DMA((2,))]),
        name="moe_combine",
    )(pos, h, rw, modr, gfin, ys)


def _moe(xn, h, ri, rw, cnt, modr, row_fn, gfin, wg, wu, wd):
    m = h.shape[0]
    tm = MOE_TM
    nt = 2 * m // tm + N_EXPERTS
    counts = cnt[0, :N_EXPERTS].astype(jnp.int32)
    nt_e = (counts + tm - 1) // tm
    t_end = jnp.cumsum(nt_e)
    t_start = t_end - nt_e
    n_used = t_end[-1]
    tid = jnp.arange(nt, dtype=jnp.int32)
    te_raw = jnp.sum((tid[:, None] >= t_end[None, :]).astype(jnp.int32), axis=1)
    te_last = jnp.sum((n_used - 1 >= t_end).astype(jnp.int32))
    te = jnp.minimum(te_raw, te_last).astype(jnp.int32)
    nv = jnp.where(tid < n_used, jnp.clip(counts[te] - (tid - t_start[te]) * tm, 0, tm), 0).astype(jnp.int32)
    pos = (t_start[ri[:, 0:2]] * tm + ri[:, 2:4]).astype(jnp.int32).reshape(-1)
    nu = n_used.reshape(1).astype(jnp.int32)
    ys = _experts(xn, pos, te, nv, nu, wg, wu, wd)
    return _combine(pos, h, rw, modr, row_fn, gfin, ys)


def _relayout_main_kernel(wt_ref, main_ref):
    main_ref[...] = wt_ref[0].T.astype(BF16)


def _relayout_small_kernel(dt_ref, glr_ref, small_ref, buf_ref):
    buf_ref[...] = jnp.zeros_like(buf_ref)
    buf_ref[0:N_DT, :] = dt_ref[0]
    buf_ref[N_DT:N_DT + N_GLR, :] = glr_ref[0]
    small_ref[...] = buf_ref[...].T.astype(BF16)


def _relayout_w_in(w_in):
    depth, rows, cols = w_in.shape
    wt = jnp.swapaxes(w_in, 1, 2)
    tc = RELAYOUT_TC
    shifted = COL_DT // tc
    assert P_MAIN // tc <= 2 * shifted
    src_row = lambda k: pl.multiple_of(k * tc + N_DT * (k // shifted), N_DT)
    main = pl.pallas_call(
        _relayout_main_kernel,
        out_shape=jax.ShapeDtypeStruct((depth, rows, P_MAIN), BF16),
        grid=(depth, P_MAIN // tc),
        in_specs=[pl.BlockSpec((pl.Element(1), pl.Element(tc), pl.Element(rows)), lambda l, k: (l, src_row(k), 0))],
        out_specs=pl.BlockSpec((None, rows, tc), lambda l, k: (l, 0, k)),
        name="relayout_w_in",
    )(wt)
    small = pl.pallas_call(
        _relayout_small_kernel,
        out_shape=jax.ShapeDtypeStruct((depth, rows, LANE), BF16),
        grid=(depth,),
        in_specs=[pl.BlockSpec((pl.Element(1), pl.Element(N_DT), pl.Element(rows)), lambda l: (l, COL_DT, 0)),
                  pl.BlockSpec((pl.Element(1), pl.Element(N_GLR), pl.Element(rows)), lambda l: (l, COL_GLR, 0))],
        out_specs=pl.BlockSpec((None, rows, LANE), lambda l: (l, 0, 0)),
        scratch_shapes=[pltpu.VMEM((LANE, rows), F32)],
        name="relayout_w_small",
    )(wt, wt)
    return main, small


def _head_row(v, direction):
    lo = SSD_HEADS * direction
    return jnp.pad(v.astype(F32).reshape(1, SSD_HEADS), ((0, 0), (lo, LANE - SSD_HEADS - lo)))


def kernel(x, c, ctx, c_ctx, w_ada, b_ada, g_mix, g_ffn, w_in, w_out, sc_conv_w, cm_w_s, cm_b_s, mb_conv_w,
           mb_conv_b, mb_a_log, mb_dt_bias, mb_d, mb_norm_g, gla_w_gate, gla_b_gate, gla_norm_g, ffn_w_gate,
           ffn_w_up, ffn_w_down, moe_router, moe_w_gate, moe_w_up, moe_w_down, g_final):
    nb, seq, _ = x.shape
    ctx_len = ctx.shape[1]
    depth = w_ada.shape[0]
    assert depth == 2, "supported stack: a dense-FFN layer followed by a last, routed-FFN layer"
    m_lat = nb * seq
    h = x.reshape(m_lat, D)
    hc = ctx.reshape(nb * ctx_len, D)

    cond = jnp.concatenate([c, c_ctx[None, :]], axis=0)
    cb = jnp.broadcast_to(cond[:, :, None], (nb + 1, D, LANE))
    mods = _ada(cb, w_ada, b_ada).reshape(depth, 8, N_MOD, 1, D)

    tm_in = IN_TM
    lat_row_in = lambda i: i // (seq // tm_in)
    lat_row_512 = lambda i: i // (seq // OUT_TM)
    lat_row_g = lambda i: i // (seq // GATHER_R)
    ctx_row = lambda i: nb

    head_of_col = np.arange(GW) // SSD_HD
    expand = [jnp.asarray(np.arange(LANE)[:, None] == 8 * d + head_of_col[None, :], dtype=BF16) for d in range(2)]
    w_in_main, w_in_small = _relayout_w_in(w_in)
    out = None
    for i in range(depth):
        last = i == depth - 1
        modr = mods[i]
        g_mix_i = g_mix[i].reshape(1, D)
        g_ffn_i = g_ffn[i].reshape(1, D)
        conv_b = mb_conv_b[i].reshape(1, -1)
        p, ps = _inproj(h, g_mix_i, modr, lat_row_in, w_in_main, w_in_small, mb_conv_w[i], conv_b, i, tm_in, GRID_W)
        pc, psc = _inproj(hc, g_mix_i, modr, ctx_row, w_in_main, w_in_small, mb_conv_w[i], conv_b, i,
                          nb * ctx_len, ctx_len)

        ws = cm_w_s[i].astype(BF16)
        bs = jnp.repeat(cm_b_s[i].T, 128, axis=1)
        ya = _scm(p, sc_conv_w[i], ws, bs, GRID_W)
        if not last:
            yac = _scm(pc, sc_conv_w[i], ws, bs, ctx_len)

        s_zero = jnp.zeros((nb, SSD_Q, GW), F32)
        skip = jnp.repeat(mb_d[i, 0] + mb_d[i, 1], SSD_HD).reshape(1, GW)
        fin = (skip, mb_norm_g[i].reshape(1, GW))
        ssd_c = [(_head_row(mb_a_log[i, d], d), _head_row(mb_dt_bias[i, d], d), expand[d]) for d in range(2)]
        y0c, s0c = _ssd(pc, psc, s_zero, ssd_c[0], nb, 0, False)
        y0, _ = _ssd(p, ps, s0c, ssd_c[0], nb, 0, False)
        if last:
            _, s1c = _ssd(pc, psc, s_zero, ssd_c[1], nb, 1, False)
        else:
            ymbc, s1c = _ssd(pc, psc, s_zero, ssd_c[1], nb, 1, True, y0c, fin)
        ymb, _ = _ssd(p, ps, s1c, ssd_c[1], nb, 1, True, y0, fin)

        g_zero = jnp.zeros((nb, GLA_DV, 256), F32)
        gla_c = [(jnp.zeros((LANE, 256), F32).at[16 + 16 * d:32 + 16 * d].set(gla_w_gate[i, d]).astype(BF16),
                  gla_b_gate[i, d].reshape(1, 256)) for d in range(2)]
        ng = gla_norm_g[i].reshape(1, GLA_DV)
        o0c, t0c = _gla(pc, psc, g_zero, gla_c[0], nb, 0, False)
        o0, _ = _gla(p, ps, t0c, gla_c[0], nb, 0, False)
        if last:
            _, t1c = _gla(pc, psc, g_zero, gla_c[1], nb, 1, False)
        else:
            yglac, t1c = _gla(pc, psc, g_zero, gla_c[1], nb, 1, True, o0c, ng)
        ygla = _gla(p, ps, t1c, gla_c[1], nb, 1, True, o0, ng)[0]

        w_out_b = w_out[i].astype(BF16)
        j = i // 2
        if i % 2 == 0:
            hcn, xcn = _outproj(hc, yac, ymbc, yglac, w_out_b, modr, ctx_row, g_ffn_i)
            hc, wg, wu, wd = _ffn(xcn, hcn, modr, ctx_row, ffn_w_gate[j], ffn_w_up[j], ffn_w_down[j],
                                  nb * ctx_len, True)
            hn, xn = _outproj(h, ya, ymb, ygla, w_out_b, modr, lat_row_512, g_ffn_i)
            h = _ffn(xn, hn, modr, lat_row_in, wg, wu, wd, tm_in, False)[0]
        else:
            w_router = jnp.pad(moe_router[j], ((0, 0), (0, LANE - N_EXPERTS)))
            hn, xn, ri, rw, cnt = _outproj(h, ya, ymb, ygla, w_out_b, modr, lat_row_512, g_ffn_i, w_router)
            if last:
                out = _moe(xn, hn, ri, rw, cnt, modr, lat_row_g, g_final.reshape(1, D),
                           moe_w_gate[j], moe_w_up[j], moe_w_down[j])
            else:
                raise NotImplementedError("routed FFN is only implemented as the last layer's channel mixer")
    return out.reshape(nb, seq, D)
```

```python
import functools
import math

import jax
import jax.numpy as jnp
import numpy as np
from jax import lax
from jax.experimental import pallas as pl
from jax.experimental.pallas import tpu as pltpu

F32 = jnp.float32
BF16 = jnp.bfloat16
EPS = 1e-6

D = 2048
GW = D // 4
GRID_W = 64
N_MOD = 6
LANE = 128
BF16_SUBLANES = 16
P_MAIN = 5632
COL_DT, N_DT = 4096, 16
COL_GLR, N_GLR = 5648, 32

SSD_Q = 128
SSD_HEADS = 8
SSD_HD = 64
GLA_C = 64
GLA_HEADS = 4
GLA_DK = 64
GLA_DV = 128
GLA_RANK = 16
N_EXPERTS = 8

FFN_SB = 256
MOE_SB = 272
MOE_TM = 4 * MOE_SB
MOE_TF = 512
MOE_ISSUE = 78
MOE_VMEM_LIMIT = 127 * 512 * 1024
GATHER_R = 256

ADA_TN = 1024
IN_TM = 1024
SCM_T = 512
SCAN_TB = 256
OUT_TM = 512
RELAYOUT_TC = 256


def _silu(x):
    return x * jax.nn.sigmoid(x)


def _softplus(x):
    return jnp.maximum(x, 0.0) + jnp.log1p(jnp.exp(-jnp.abs(x)))


def _split3(x):
    hi = x.astype(BF16)
    r = x - hi.astype(F32)
    mid = r.astype(BF16)
    lo = (r - mid.astype(F32)).astype(BF16)
    return hi, mid, lo


def _dot(a, b):
    return jnp.dot(a, b, preferred_element_type=F32)


def _dot_nt(a, b):
    return lax.dot_general(a, b, (((1,), (1,)), ((), ())), preferred_element_type=F32)


def _dot_tn(a, b):
    return lax.dot_general(a, b, (((0,), (0,)), ((), ())), preferred_element_type=F32)


def _dot01_left(m01, x):
    hi, mid, lo = _split3(x)
    return _dot(m01, hi) + _dot(m01, mid) + _dot(m01, lo)


def _dot01_right(x, m01):
    hi, mid, lo = _split3(x)
    return _dot(hi, m01) + _dot(mid, m01) + _dot(lo, m01)


def _rms(x, g):
    return x * lax.rsqrt(jnp.mean(x * x, axis=-1, keepdims=True) + EPS) * g


def _conv3(x, w, seg):
    n = x.shape[0]
    t = lax.broadcasted_iota(jnp.int32, (n, 1), 0) & (seg - 1)
    prev = jnp.where(t == 0, 0.0, pltpu.roll(x, 1, 0))
    nxt = jnp.where(t == seg - 1, 0.0, pltpu.roll(x, n - 1, 0))
    return prev * w[0:1] + x * w[1:2] + nxt * w[2:3]


def _ada_kernel(cb_ref, w_ref, b_ref, o_ref, s_ref):
    @pl.when((pl.program_id(0) == 0) & (pl.program_id(1) == 0))
    def _():
        s_ref[...] = _silu(cb_ref[...])

    tn = w_ref.shape[1]
    o_ref[...] = jnp.zeros_like(o_ref)
    for j in range(tn // LANE):
        wj = w_ref[:, j * LANE:(j + 1) * LANE]
        for m in range(3):
            o_ref[m:m + 1, j * LANE:(j + 1) * LANE] = (
                jnp.sum(wj * s_ref[m], axis=0, keepdims=True) + b_ref[:, j * LANE:(j + 1) * LANE])


def _ada(cb, w_ada, b_ada):
    depth = w_ada.shape[0]
    n = w_ada.shape[2]
    tn = ADA_TN
    return pl.pallas_call(
        _ada_kernel,
        out_shape=jax.ShapeDtypeStruct((depth, 8, n), F32),
        grid=(depth, n // tn),
        in_specs=[pl.BlockSpec((3, D, LANE), lambda l, j: (0, 0, 0)),
                  pl.BlockSpec((None, D, tn), lambda l, j: (l, 0, j)),
                  pl.BlockSpec((None, 1, tn), lambda l, j: (l, 0, j))],
        out_specs=pl.BlockSpec((None, 8, tn), lambda l, j: (l, 0, j)),
        scratch_shapes=[pltpu.VMEM((3, D, LANE), F32)],
        name="ada",
    )(cb, w_ada, b_ada.reshape(depth, 1, n))


def _inproj_kernel(h_ref, g_ref, sh_ref, sc_ref, w_ref, ws_ref, o_ref, os_ref, xn_ref):
    @pl.when(pl.program_id(1) == 0)
    def _():
        xn = _rms(h_ref[...], g_ref[...])
        xn_ref[...] = (xn * (1.0 + sc_ref[...]) + sh_ref[...]).astype(BF16)
        os_ref[...] = _dot(xn_ref[...], ws_ref[...])

    o_ref[...] = _dot(xn_ref[...], w_ref[...]).astype(BF16)


def _mod_spec(row_fn, k):
    return pl.BlockSpec((None, None, 1, D), lambda i, j: (row_fn(i), k, 0, 0))


def _inproj(h, g, modr, row_fn, w, w_small, layer, tm):
    m = h.shape[0]
    tn = P_MAIN // 4
    return pl.pallas_call(
        _inproj_kernel,
        out_shape=(jax.ShapeDtypeStruct((m, P_MAIN), BF16), jax.ShapeDtypeStruct((m, LANE), F32)),
        grid=(m // tm, P_MAIN // tn),
        in_specs=[pl.BlockSpec((tm, D), lambda i, j: (i, 0)),
                  pl.BlockSpec((1, D), lambda i, j: (0, 0)),
                  _mod_spec(row_fn, 0), _mod_spec(row_fn, 1),
                  pl.BlockSpec((None, D, tn), lambda i, j: (layer, 0, j)),
                  pl.BlockSpec((None, D, LANE), lambda i, j: (layer, 0, 0))],
        out_specs=(pl.BlockSpec((tm, tn), lambda i, j: (i, j)),
                   pl.BlockSpec((tm, LANE), lambda i, j: (i, 0))),
        scratch_shapes=[pltpu.VMEM((tm, D), BF16)],
        name="inproj",
    )(h, g, modr, modr, w, w_small)


def _gelu_tanh(x):
    c = 0.7978845608028654
    return x * (0.5 * (1.0 + jnp.tanh(c * (x + 0.044715 * (x * x * x)))))


def _scm_kernel(sc_ref, u_ref, v_ref, cw_ref, ws_ref, bs_ref, o_ref, *, seg):
    t = sc_ref.shape[0]
    bgate = sc_ref[:, 0:GW].astype(F32)
    gated = sc_ref[:, GW:2 * GW].astype(F32) * sc_ref[:, 2 * GW:3 * GW].astype(F32)
    y_sc = bgate * _conv3(gated, cw_ref[...], seg)
    o_ref[:, 0:GW] = y_sc.astype(BF16)

    u = _gelu_tanh(u_ref[...].astype(F32))
    v = _gelu_tanh(v_ref[...].astype(F32)).astype(BF16)
    for c in range(t // 128):
        rows = slice(c * 128, (c + 1) * 128)
        for g in range(4):
            cols = slice(g * 128, (g + 1) * 128)
            s = _dot(ws_ref[g], v[rows, cols]) + bs_ref[:, cols]
            o_ref[rows, GW + g * 128:GW + (g + 1) * 128] = (u[rows, cols] * s).astype(BF16)


def _scm(p, cw, ws, bs, seg):
    m = p.shape[0]
    t = SCM_T
    return pl.pallas_call(
        functools.partial(_scm_kernel, seg=seg),
        out_shape=jax.ShapeDtypeStruct((m, 2 * GW), BF16),
        grid=(m // t,),
        in_specs=[pl.BlockSpec((t, 3 * GW), lambda i: (i, 0)),
                  pl.BlockSpec((t, GW), lambda i: (i, 3)),
                  pl.BlockSpec((t, GW), lambda i: (i, 4)),
                  pl.BlockSpec((3, GW), lambda i: (0, 0)),
                  pl.BlockSpec((4, 128, 128), lambda i: (0, 0, 0)),
                  pl.BlockSpec((128, GW), lambda i: (0, 0))],
        out_specs=pl.BlockSpec((t, 2 * GW), lambda i: (i, 0)),
        name="scm",
    )(p, p, p, cw, ws, bs)


def _ssd_kernel(*refs, direction, final, seg):
    if final:
        (xbc_ref, sm_ref, cw_ref, cb_ref, alog_ref, dtb_ref, e_ref, s0_ref,
         y0_ref, z_ref, skip_ref, ng_ref, y_ref, sfin_ref, st_ref) = refs
    else:
        (xbc_ref, sm_ref, cw_ref, cb_ref, alog_ref, dtb_ref, e_ref, s0_ref,
         y_ref, sfin_ref, st_ref) = refs
    j = pl.program_id(0)
    nb, tb = xbc_ref.shape[0], xbc_ref.shape[1]
    q = SSD_Q

    @pl.when(j == 0)
    def _():
        st_ref[...] = s0_ref[...]

    a_neg = -jnp.exp(alog_ref[...])
    ii = lax.broadcasted_iota(jnp.int32, (q, q), 0)
    jj = lax.broadcasted_iota(jnp.int32, (q, q), 1)
    mask = (jj <= ii) if direction == 0 else (jj >= ii)
    tri = mask.astype(BF16)
    last = q - 1 if direction == 0 else 0
    lane0 = 8 * direction
    upper_half = lax.broadcasted_iota(jnp.int32, (q, 128), 1) >= SSD_HD
    expand = e_ref[...]

    xcs = [_silu(_conv3(xbc_ref[b].astype(F32), cw_ref[...], seg) + cb_ref[...])
           for b in range(nb)]
    states = [st_ref[b] for b in range(nb)]

    def chunk(b, c, st):
        rows = slice(c * q, (c + 1) * q)
        xc = xcs[b]
        dt = _softplus(sm_ref[b, rows, :] + dtb_ref[...])
        acs = _dot01_left(tri, dt * a_neg)
        tot = acs[last:last + 1, :]
        acs_t = acs.T
        dt_t = dt.T
        wx = _dot01_right(jnp.exp(tot - acs) * dt, expand)
        ea = _dot01_right(jnp.exp(acs), expand)
        xs = xc[rows, 0:GW]
        xw = (wx * xs).astype(BF16)
        stb = st.astype(BF16)
        ys = []
        new_st = []
        for g in range(2):
            bg = xc[rows, GW + g * 128:GW + (g + 1) * 128].astype(BF16)
            cg = xc[rows, GW + 256 + g * 128:GW + 256 + (g + 1) * 128].astype(BF16)
            cb = _dot_nt(cg, bg)
            gcols = slice(g * 256, (g + 1) * 256)
            yint = _dot(cg, stb[:, gcols])
            for pr in range(2):
                pidx = 2 * g + pr
                pcols = slice(pidx * 128, (pidx + 1) * 128)
                xp = xs[:, pcols]
                acc = yint[:, pr * 128:(pr + 1) * 128] * ea[:, pcols]
                for hh in range(2):
                    ln = lane0 + 2 * pidx + hh
                    sg = acs[:, ln:ln + 1] - acs_t[ln:ln + 1, :]
                    decay = jnp.where(mask, jnp.exp(jnp.where(mask, sg, 0.0)), 0.0)
                    scores = (cb * decay * dt_t[ln:ln + 1, :]).astype(BF16)
                    xh = jnp.where(upper_half == (hh == 1), xp, 0.0).astype(BF16)
                    acc = acc + _dot(scores, xh)
                ys.append(acc)
            new_st.append(ea[last:last + 1, gcols] * st[:, gcols] + _dot_tn(bg, xw[:, gcols]))
        y = jnp.concatenate(ys, axis=1)
        if final:
            y = y0_ref[b, rows, :] + y + xs * skip_ref[...]
            y = y * _silu(z_ref[b, rows, :].astype(F32))
            y_ref[b, rows, :] = _rms(y, ng_ref[...]).astype(y_ref.dtype)
        else:
            y_ref[b, rows, :] = y
        return jnp.concatenate(new_st, axis=1)

    nchunk = tb // q
    order = range(nchunk) if direction == 0 else range(nchunk - 1, -1, -1)
    for c in order:
        for b in range(nb):
            states[b] = chunk(b, c, states[b])
    for b in range(nb):
        st_ref[b] = states[b]

    @pl.when(j == pl.num_programs(0) - 1)
    def _():
        sfin_ref[...] = st_ref[...]


def _scan_block_map(nblk, direction):
    return (lambda j: j) if direction == 0 else (lambda j: nblk - 1 - j)


def _ssd(p, ps, s0, consts, nb, seg, direction, final, y0=None, fin=None):
    m = p.shape[0]
    seq = m // nb
    tb = SCAN_TB
    nblk = seq // tb
    blk = _scan_block_map(nblk, direction)
    p3 = p.reshape(nb, seq, P_MAIN)
    full = lambda shape: pl.BlockSpec(shape, lambda j: (0,) * len(shape))
    rows = lambda width, col: pl.BlockSpec((nb, tb, width), lambda j: (0, blk(j), col))
    in_specs = [rows(1024, 3), rows(LANE, 0),
                full((3, 1024)), full((1, 1024)), full((1, LANE)), full((1, LANE)), full((LANE, GW)),
                full((nb, SSD_Q, GW))]
    args = [p3, ps.reshape(nb, seq, LANE), *consts, s0]
    if final:
        in_specs += [rows(GW, 0), rows(GW, 5), full((1, GW)), full((1, GW))]
        args += [y0.reshape(nb, seq, GW), p3, *fin]
    y, s_fin = pl.pallas_call(
        functools.partial(_ssd_kernel, direction=direction, final=final, seg=seg),
        out_shape=(jax.ShapeDtypeStruct((nb, seq, GW), BF16 if final else F32),
                   jax.ShapeDtypeStruct((nb, SSD_Q, GW), F32)),
        grid=(nblk,),
        in_specs=in_specs,
        out_specs=(rows(GW, 0), full((nb, SSD_Q, GW))),
        scratch_shapes=[pltpu.VMEM((nb, SSD_Q, GW), F32)],
        name=f"ssd_d{direction}",
    )(*args)
    return y.reshape(m, GW), s_fin


def _gla_kernel(*refs, direction, final):
    if final:
        (q_ref, k_ref, v_ref, sm_ref, wg_ref, bg_ref, s0_ref, o0_ref, g_ref, ng_ref,
         o_ref, sfin_ref, st_ref) = refs
    else:
        (q_ref, k_ref, v_ref, sm_ref, wg_ref, bg_ref, s0_ref, o_ref, sfin_ref, st_ref) = refs
    j = pl.program_id(0)
    nb, tb = q_ref.shape[0], q_ref.shape[1]
    c_len = GLA_C

    @pl.when(j == 0)
    def _():
        st_ref[...] = s0_ref[...]

    ii = lax.broadcasted_iota(jnp.int32, (tb, tb), 0)
    jj = lax.broadcasted_iota(jnp.int32, (tb, tb), 1)
    same = (ii >> 6) == (jj >> 6)
    btri = (same & ((jj <= ii) if direction == 0 else (jj >= ii))).astype(BF16)
    gcums = []
    for b in range(nb):
        gk = -_softplus(-(_dot(sm_ref[b].astype(BF16), wg_ref[...]) + bg_ref[...])) * (1.0 / 16.0)
        gcums.append(_dot01_left(btri, gk))

    hs = GLA_HEADS * c_len
    ri = lax.broadcasted_iota(jnp.int32, (hs, hs), 0)
    ci = lax.broadcasted_iota(jnp.int32, (hs, hs), 1)
    own_lanes = (ri >> 6) == (ci >> 6)
    ti, tj = ri & (c_len - 1), ci & (c_len - 1)
    own_causal = own_lanes & ((tj <= ti) if direction == 0 else (tj >= ti))
    last = c_len - 1 if direction == 0 else 0
    ref = c_len // 2 if direction == 0 else c_len - 1 - c_len // 2

    def stack_heads(x):
        return jnp.where(own_lanes, jnp.concatenate([x] * GLA_HEADS, axis=0), 0.0).astype(BF16)

    states = [st_ref[b] for b in range(nb)]

    def chunk(b, c, st):
        rows = slice(c * c_len, (c + 1) * c_len)
        gc = gcums[b][rows, :]
        gl = gc[last:last + 1, :]
        gr = gc[ref:ref + 1, :]
        qc = q_ref[b, rows, :].astype(F32) * (GLA_DK ** -0.5)
        kc = k_ref[b, rows, :].astype(F32)
        vc = v_ref[b, rows, :]
        v_rows = jnp.concatenate([vc[:, h * GLA_DV:(h + 1) * GLA_DV] for h in range(GLA_HEADS)], axis=0)
        kb = (kc * jnp.exp(gr - gc)).astype(BF16)
        att = _dot_nt(stack_heads(qc * jnp.exp(gc - gr)), jnp.concatenate([kb] * GLA_HEADS, axis=0))
        att = jnp.where(own_causal, att, 0.0).astype(BF16)
        y = _dot(att, v_rows) + _dot_nt(stack_heads(qc * jnp.exp(gc)), st.astype(BF16))
        contrib = _dot_tn(v_rows, stack_heads(kc * jnp.exp(gl - gc)))
        new_st = jnp.exp(gl) * st + contrib
        o = jnp.concatenate([y[h * c_len:(h + 1) * c_len, :] for h in range(GLA_HEADS)], axis=1)
        if final:
            o = o0_ref[b, rows, :] + o
            normed = [_rms(o[:, h * GLA_DV:(h + 1) * GLA_DV], ng_ref[...]) for h in range(GLA_HEADS)]
            gate = _silu(g_ref[b, rows, :].astype(F32))
            o_ref[b, rows, :] = (jnp.concatenate(normed, axis=1) * gate).astype(o_ref.dtype)
        else:
            o_ref[b, rows, :] = o
        return new_st

    nchunk = tb // c_len
    order = range(nchunk) if direction == 0 else range(nchunk - 1, -1, -1)
    for c in order:
        for b in range(nb):
            states[b] = chunk(b, c, states[b])
    for b in range(nb):
        st_ref[b] = states[b]

    @pl.when(j == pl.num_programs(0) - 1)
    def _():
        sfin_ref[...] = st_ref[...]


def _gla(p, ps, s0, consts, nb, direction, final, o0=None, ng=None):
    m = p.shape[0]
    seq = m // nb
    tb = SCAN_TB
    nblk = seq // tb
    blk = _scan_block_map(nblk, direction)
    p3 = p.reshape(nb, seq, P_MAIN)
    full = lambda shape: pl.BlockSpec(shape, lambda j: (0,) * len(shape))
    rows = lambda width, col: pl.BlockSpec((nb, tb, width), lambda j: (0, blk(j), col))
    in_specs = [rows(256, 16), rows(256, 17), rows(GW, 9), rows(LANE, 0),
                full((LANE, 256)), full((1, 256)), full((nb, GLA_DV, 256))]
    args = [p3, p3, p3, ps.reshape(nb, seq, LANE), *consts, s0]
    if final:
        in_specs += [rows(GW, 0), rows(GW, 10), full((1, GLA_DV))]
        args += [o0.reshape(nb, seq, GW), p3, ng]
    o, s_fin = pl.pallas_call(
        functools.partial(_gla_kernel, direction=direction, final=final),
        out_shape=(jax.ShapeDtypeStruct((nb, seq, GW), BF16 if final else F32),
                   jax.ShapeDtypeStruct((nb, GLA_DV, 256), F32)),
        grid=(nblk,),
        in_specs=in_specs,
        out_specs=(rows(GW, 0), full((nb, GLA_DV, 256))),
        scratch_shapes=[pltpu.VMEM((nb, GLA_DV, 256), F32)],
        name=f"gla_d{direction}",
    )(*args)
    return o.reshape(m, GW), s_fin


def _outproj_kernel(*refs, route):
    if route:
        (h_ref, ya_ref, ymb_ref, ygla_ref, w_ref, gate_ref, g_ref, sh_ref, sc_ref, wr_ref,
         hn_ref, xn_ref, ri_ref, rw_ref, cnt_ref, carry_ref, xprev_ref) = refs
    else:
        (h_ref, ya_ref, ymb_ref, ygla_ref, w_ref, gate_ref, g_ref, sh_ref, sc_ref,
         hn_ref, xn_ref) = refs

    def project():
        acc = (_dot(ya_ref[...], w_ref[0:2 * GW, :]) + _dot(ymb_ref[...], w_ref[2 * GW:3 * GW, :])
               + _dot(ygla_ref[...], w_ref[3 * GW:4 * GW, :]))
        hn = h_ref[...] + gate_ref[...] * acc
        hn_ref[...] = hn
        xn = _rms(hn, g_ref[...]) * (1.0 + sc_ref[...]) + sh_ref[...]
        xn_ref[...] = xn.astype(xn_ref.dtype)
        return xn

    if not route:
        project()
        return

    i = pl.program_id(0)
    n_tiles = pl.num_programs(0) - 1

    def route_prev():
        _route_tile(xprev_ref[...], wr_ref, ri_ref, rw_ref, cnt_ref, carry_ref)

    @pl.when(i == 0)
    def _():
        carry_ref[...] = jnp.zeros_like(carry_ref)
        xprev_ref[...] = project()

    @pl.when((i > 0) & (i < n_tiles))
    def _():
        route_prev()
        xprev_ref[...] = project()

    @pl.when(i == n_tiles)
    def _():
        route_prev()


def _route_tile(xn, wr_ref, ri_ref, rw_ref, cnt_ref, carry_ref):
    tm = xn.shape[0]
    xh = xn.astype(BF16)
    xl = (xn - xh.astype(F32)).astype(BF16)
    wr = wr_ref[...]
    wh = wr.astype(BF16)
    wl = (wr - wh.astype(F32)).astype(BF16)
    hh_hl = _dot(xh, jnp.concatenate([wh, wl], axis=1))
    logits = hh_hl[:, 0:LANE] + hh_hl[:, LANE:2 * LANE] + _dot(xl, wh)
    lane = lax.broadcasted_iota(jnp.int32, (tm, LANE), 1)
    lane_f = lane.astype(F32)
    neg = jnp.float32(-3.0e38)
    lm = jnp.where(lane < N_EXPERTS, logits, neg)
    m1 = jnp.max(lm, axis=-1, keepdims=True)
    i1 = jnp.min(jnp.where(lm == m1, lane_f, float(LANE)), axis=-1, keepdims=True)
    lm2 = jnp.where(lane_f == i1, neg, lm)
    m2 = jnp.max(lm2, axis=-1, keepdims=True)
    i2 = jnp.min(jnp.where(lm2 == m2, lane_f, float(LANE)), axis=-1, keepdims=True)
    e = jnp.exp(m2 - m1)
    w1 = 1.0 / (1.0 + e)
    w2 = e / (1.0 + e)
    sel1 = lane_f == i1
    sel2 = lane_f == i2
    onehot = jnp.where(sel1 | sel2, 1.0, 0.0)
    ti = lax.broadcasted_iota(jnp.int32, (tm, tm), 0)
    tj = lax.broadcasted_iota(jnp.int32, (tm, tm), 1)
    incl = _dot((tj <= ti).astype(BF16), onehot.astype(BF16)) + carry_ref[...]
    excl = incl - onehot
    r1 = jnp.sum(jnp.where(sel1, excl, 0.0), axis=-1, keepdims=True)
    r2 = jnp.sum(jnp.where(sel2, excl, 0.0), axis=-1, keepdims=True)
    info = jnp.where(lane == 0, i1, jnp.where(lane == 1, i2, jnp.where(lane == 2, r1, jnp.where(lane == 3, r2, 0.0))))
    ri_ref[...] = info.astype(jnp.int32)
    rw_ref[...] = jnp.where(lane == 0, w1, jnp.where(lane == 1, w2, 0.0))
    carry_ref[...] = incl[tm - 1:tm, :]
    cnt_ref[...] = incl[tm - 1:tm, :]


def _outproj(h, ya, ymb, ygla, w, modr, row_fn, g, w_router=None):
    m = h.shape[0]
    tm = OUT_TM
    route = w_router is not None
    n_tiles = m // tm
    cur = (lambda i: jnp.minimum(i, n_tiles - 1)) if route else (lambda i: i)
    prev = lambda i: jnp.maximum(i - 1, 0)
    mod = lambda k: pl.BlockSpec((None, None, 1, D), lambda i: (row_fn(cur(i)), k, 0, 0))
    in_specs = [pl.BlockSpec((tm, D), lambda i: (cur(i), 0)),
                pl.BlockSpec((tm, 2 * GW), lambda i: (cur(i), 0)),
                pl.BlockSpec((tm, GW), lambda i: (cur(i), 0)),
                pl.BlockSpec((tm, GW), lambda i: (cur(i), 0)),
                pl.BlockSpec((D, D), lambda i: (0, 0)),
                mod(2), pl.BlockSpec((1, D), lambda i: (0, 0)), mod(3), mod(4)]
    args = [h, ya, ymb, ygla, w, modr, g, modr, modr]
    out_shape = [jax.ShapeDtypeStruct((m, D), F32), jax.ShapeDtypeStruct((m, D), F32 if route else BF16)]
    out_specs = [pl.BlockSpec((tm, D), lambda i: (cur(i), 0)), pl.BlockSpec((tm, D), lambda i: (cur(i), 0))]
    scratch = []
    if route:
        in_specs.append(pl.BlockSpec((D, LANE), lambda i: (0, 0)))
        args.append(w_router)
        out_shape += [jax.ShapeDtypeStruct((m, LANE), jnp.int32), jax.ShapeDtypeStruct((m, LANE), F32),
                      jax.ShapeDtypeStruct((1, LANE), F32)]
        out_specs += [pl.BlockSpec((tm, LANE), lambda i: (prev(i), 0)), pl.BlockSpec((tm, LANE), lambda i: (prev(i), 0)),
                      pl.BlockSpec((1, LANE), lambda i: (0, 0))]
        scratch = [pltpu.VMEM((1, LANE), F32), pltpu.VMEM((tm, D), F32)]
    return pl.pallas_call(
        functools.partial(_outproj_kernel, route=route),
        out_shape=tuple(out_shape),
        grid=(n_tiles + 1 if route else n_tiles,),
        in_specs=in_specs,
        out_specs=tuple(out_specs),
        scratch_shapes=scratch,
        name="outproj_route" if route else "outproj",
    )(*args)


def _swiglu_rows(x_ref, wg_ref, wu_ref, wd_ref, o_ref, wgb_ref, wub_ref, wdb_ref, sb_rows, nsb, gate,
                 prologue=None):
    def sub_block(sb):
        start = sb * sb_rows
        return pl.ds(start if isinstance(sb, int) else pl.multiple_of(start, BF16_SUBLANES), sb_rows)

    def up(sb):
        x = x_ref[sub_block(sb), :]
        return (_silu(_dot(x, wgb_ref[...])) * _dot(x, wub_ref[...])).astype(BF16)

    def down(sb, hidden):
        rows = sub_block(sb)
        y = _dot(hidden, wdb_ref[...])
        o_ref[rows, :] += y if gate is None else gate * y

    if prologue is not None:
        prologue()
    if wg_ref is not None:
        wgb_ref[...] = wg_ref[...].astype(BF16)
        wub_ref[...] = wu_ref[...].astype(BF16)
    hidden = up(0)
    if wd_ref is not None:
        wdb_ref[...] = wd_ref[...].astype(BF16)
    if isinstance(nsb, int):
        for sb in range(1, nsb):
            nxt = up(sb)
            down(sb - 1, hidden)
            hidden = nxt
    else:
        def body(sb, hid):
            nxt = up(sb)
            down(sb - 1, hid)
            return nxt

        hidden = lax.fori_loop(1, nsb, body, hidden)
    down(nsb - 1, hidden)


def _ffn_kernel(*refs, emit):
    if emit:
        x_ref, h_hbm, gate_ref, wg_ref, wu_ref, wd_ref, o_ref, wgb_ref, wub_ref, wdb_ref = refs
    else:
        x_ref, h_hbm, gate_ref, wgb_ref, wub_ref, wdb_ref, o_ref = refs
        wg_ref = wu_ref = wd_ref = None
    tm = o_ref.shape[0]

    @pl.when(pl.program_id(1) == 0)
    def _():
        pltpu.sync_copy(h_hbm.at[pl.ds(pl.program_id(0) * tm, tm)], o_ref)

    _swiglu_rows(x_ref, wg_ref, wu_ref, wd_ref, o_ref, wgb_ref, wub_ref, wdb_ref, FFN_SB, tm // FFN_SB,
                 gate_ref[...])


def _ffn(xn, h, modr, row_fn, wg, wu, wd, tm, emit):
    m = h.shape[0]
    tf = MOE_TF
    dff = wg.shape[1]
    assert not emit or m == tm
    w_specs = [pl.BlockSpec((D, tf), lambda i, f: (0, f)),
               pl.BlockSpec((D, tf), lambda i, f: (0, f)),
               pl.BlockSpec((tf, D), lambda i, f: (f, 0))]
    out_shape = [jax.ShapeDtypeStruct((m, D), F32)]
    out_specs = [pl.BlockSpec((tm, D), lambda i, f: (i, 0))]
    if emit:
        out_shape += [jax.ShapeDtypeStruct((D, dff), BF16), jax.ShapeDtypeStruct((D, dff), BF16),
                      jax.ShapeDtypeStruct((dff, D), BF16)]
        out_specs += w_specs
    return pl.pallas_call(
        functools.partial(_ffn_kernel, emit=emit),
        out_shape=tuple(out_shape),
        grid=(m // tm, dff // tf),
        in_specs=[pl.BlockSpec((tm, D), lambda i, f: (i, 0)),
                  pl.BlockSpec(memory_space=pl.ANY),
                  pl.BlockSpec((None, None, 1, D), lambda i, f: (row_fn(i), 5, 0, 0))] + w_specs,
        out_specs=tuple(out_specs),
        name="ffn_emit" if emit else "ffn",
    )(xn, h, modr, wg, wu, wd)


def _row_copy(src_hbm, row, dst, r, sem):
    return pltpu.make_async_copy(src_hbm.at[pl.ds(row, 1)], dst.at[pl.ds(r, 1)], sem)


def _expert_kernel(te_ref, nv_ref, nu_ref, pos_ref, xn_hbm, wg_ref, wu_ref, wd_ref, o_ref,
                   xb_ref, stage_ref, wgb_ref, wub_ref, wdb_ref, src_ref, sem):
    i = pl.program_id(0)
    f = pl.program_id(1)
    n_tiles = pl.num_programs(0)
    nf = pl.num_programs(1)
    used = i < nu_ref[0]
    stage_rows = stage_ref.shape[0]
    unroll = math.gcd(stage_rows, 8)

    def src_token(tile, r):
        tile = jnp.minimum(tile, n_tiles - 1)
        routed = nv_ref[tile]
        row = jnp.where(routed > 0, tile * MOE_TM + jnp.minimum(r, routed - 1), 0)
        return src_ref[row]

    def wait_stage():
        def wait(r, carry):
            _row_copy(xn_hbm, 0, stage_ref, r, sem).wait()
            return carry

        lax.fori_loop(0, stage_rows, wait, 0, unroll=unroll)

    @pl.when(f == 0)
    def _():
        @pl.when(i == 0)
        def _():
            def invert(t, carry):
                src_ref[pos_ref[2 * t]] = t
                src_ref[pos_ref[2 * t + 1]] = t
                return carry

            lax.fori_loop(0, pos_ref.shape[0] // 2, invert, 0, unroll=8)

            def issue(r, carry):
                _row_copy(xn_hbm, src_token(0, r), stage_ref, r, sem).start()
                return carry

            lax.fori_loop(0, stage_rows, issue, 0, unroll=unroll)

        @pl.when(i <= nu_ref[0])
        def _():
            wait_stage()
            xb_ref[...] = stage_ref[0:MOE_TM, :].astype(BF16)

    def issue_next_rows():
        for k in range(MOE_ISSUE):
            r = f * MOE_ISSUE + k
            _row_copy(xn_hbm, src_token(i + 1, r), stage_ref, r, sem).start()

    @pl.when(used)
    def _():
        @pl.when(f == 0)
        def _():
            o_ref[...] = jnp.zeros_like(o_ref)

        nsb = (nv_ref[i] + MOE_SB - 1) // MOE_SB
        refs = (xb_ref, wg_ref, wu_ref, wd_ref, o_ref, wgb_ref, wub_ref, wdb_ref)

        @pl.when(nsb == MOE_TM // MOE_SB)
        def _():
            _swiglu_rows(*refs, MOE_SB, MOE_TM // MOE_SB, None, issue_next_rows)

        @pl.when(nsb < MOE_TM // MOE_SB)
        def _():
            _swiglu_rows(*refs, MOE_SB, nsb, None, issue_next_rows)

    @pl.when(jnp.logical_not(used) & (f == 0))
    def _():
        o_ref[...] = jnp.zeros_like(o_ref)

    @pl.when(used & (i == n_tiles - 1) & (f == nf - 1))
    def _():
        wait_stage()


def _experts(xn, pos, te, nv, nu, wg, wu, wd):
    dff = wg.shape[2]
    nf = dff // MOE_TF
    nt = te.shape[0]
    stage_rows = nf * MOE_ISSUE
    assert stage_rows >= MOE_TM
    src_rows = nt * MOE_TM

    def f_eff(i, f, nu):
        return jnp.where(i < nu[0], f, nf - 1)

    return pl.pallas_call(
        _expert_kernel,
        out_shape=jax.ShapeDtypeStruct((nt * MOE_TM, D), F32),
        grid_spec=pltpu.PrefetchScalarGridSpec(
            num_scalar_prefetch=4,
            grid=(nt, nf),
            in_specs=[pl.BlockSpec(memory_space=pl.ANY),
                      pl.BlockSpec((None, D, MOE_TF), lambda i, f, te, nv, nu, pos: (te[i], 0, f_eff(i, f, nu))),
                      pl.BlockSpec((None, D, MOE_TF), lambda i, f, te, nv, nu, pos: (te[i], 0, f_eff(i, f, nu))),
                      pl.BlockSpec((None, MOE_TF, D), lambda i, f, te, nv, nu, pos: (te[i], f_eff(i, f, nu), 0))],
            out_specs=pl.BlockSpec((MOE_TM, D), lambda i, f, te, nv, nu, pos: (i, 0)),
            scratch_shapes=[pltpu.VMEM((MOE_TM, D), BF16), pltpu.VMEM((stage_rows, D), F32),
                            pltpu.VMEM((D, MOE_TF), BF16), pltpu.VMEM((D, MOE_TF), BF16),
                            pltpu.VMEM((MOE_TF, D), BF16), pltpu.SMEM((src_rows,), jnp.int32),
                            pltpu.SemaphoreType.DMA(())]),
        compiler_params=pltpu.CompilerParams(vmem_limit_bytes=MOE_VMEM_LIMIT),
        name="moe_experts",
    )(te, nv, nu, pos, xn, wg, wu, wd)


def _combine_kernel(pos_ref, h_ref, rw_ref, gate_ref, gfin_ref, y_hbm, o_ref, buf_ref, sem):
    i = pl.program_id(0)
    r_blk = h_ref.shape[0]
    slot = i % 2

    def issue_tile(tile, s):
        def issue(r, carry):
            t = tile * r_blk + r
            for k in range(2):
                pltpu.async_copy(y_hbm.at[pl.ds(pos_ref[2 * t + k], 1)], buf_ref.at[s, k].at[pl.ds(r, 1)],
                                 sem.at[s], priority=k)
            return carry

        lax.fori_loop(0, r_blk, issue, 0, unroll=8)

    @pl.when(i == 0)
    def _():
        issue_tile(0, 0)

    @pl.when(i + 1 < pl.num_programs(0))
    def _():
        issue_tile(i + 1, 1 - slot)

    def wait(r, carry):
        _row_copy(y_hbm, 0, buf_ref.at[slot, 0], r, sem.at[slot]).wait()
        _row_copy(y_hbm, 0, buf_ref.at[slot, 1], r, sem.at[slot]).wait()
        return carry

    lax.fori_loop(0, r_blk, wait, 0, unroll=8)
    y = rw_ref[:, 0:1] * buf_ref[slot, 0] + rw_ref[:, 1:2] * buf_ref[slot, 1]
    hn = h_ref[...] + gate_ref[...] * y
    o_ref[...] = _rms(hn, gfin_ref[...])


def _combine(pos, h, rw, modr, row_fn, gfin, ys):
    m = h.shape[0]
    r = GATHER_R
    return pl.pallas_call(
        _combine_kernel,
        out_shape=jax.ShapeDtypeStruct((m, D), F32),
        grid_spec=pltpu.PrefetchScalarGridSpec(
            num_scalar_prefetch=1,
            grid=(m // r,),
            in_specs=[pl.BlockSpec((r, D), lambda i, pos: (i, 0)),
                      pl.BlockSpec((r, LANE), lambda i, pos: (i, 0)),
                      pl.BlockSpec((None, None, 1, D), lambda i, pos: (row_fn(i), 5, 0, 0)),
                      pl.BlockSpec((1, D), lambda i, pos: (0, 0)),
                      pl.BlockSpec(memory_space=pl.ANY)],
            out_specs=pl.BlockSpec((r, D), lambda i, pos: (i, 0)),
            scratch_shapes=[pltpu.VMEM((2, 2, r, D), F32), pltpu.SemaphoreType.DMA((2,))]),
        name="moe_combine",
    )(pos, h, rw, modr, gfin, ys)


def _moe(xn, h, ri, rw, cnt, modr, row_fn, gfin, wg, wu, wd):
    m = h.shape[0]
    tm = MOE_TM
    nt = 2 * m // tm + N_EXPERTS
    counts = cnt[0, :N_EXPERTS].astype(jnp.int32)
    nt_e = (counts + tm - 1) // tm
    t_end = jnp.cumsum(nt_e)
    t_start = t_end - nt_e
    n_used = t_end[-1]
    tid = jnp.arange(nt, dtype=jnp.int32)
    te_raw = jnp.sum((tid[:, None] >= t_end[None, :]).astype(jnp.int32), axis=1)
    te_last = jnp.sum((n_used - 1 >= t_end).astype(jnp.int32))
    te = jnp.minimum(te_raw, te_last).astype(jnp.int32)
    nv = jnp.where(tid < n_used, jnp.clip(counts[te] - (tid - t_start[te]) * tm, 0, tm), 0).astype(jnp.int32)
    pos = (t_start[ri[:, 0:2]] * tm + ri[:, 2:4]).astype(jnp.int32).reshape(-1)
    nu = n_used.reshape(1).astype(jnp.int32)
    ys = _experts(xn, pos, te, nv, nu, wg, wu, wd)
    return _combine(pos, h, rw, modr, row_fn, gfin, ys)


def _relayout_main_kernel(wt_ref, main_ref):
    main_ref[...] = wt_ref[0].T.astype(BF16)


def _relayout_small_kernel(dt_ref, glr_ref, small_ref, buf_ref):
    buf_ref[...] = jnp.zeros_like(buf_ref)
    buf_ref[0:N_DT, :] = dt_ref[0]
    buf_ref[N_DT:N_DT + N_GLR, :] = glr_ref[0]
    small_ref[...] = buf_ref[...].T.astype(BF16)


def _relayout_w_in(w_in):
    depth, rows, cols = w_in.shape
    wt = jnp.swapaxes(w_in, 1, 2)
    tc = RELAYOUT_TC
    shifted = COL_DT // tc
    assert P_MAIN // tc <= 2 * shifted
    src_row = lambda k: pl.multiple_of(k * tc + N_DT * (k // shifted), N_DT)
    main = pl.pallas_call(
        _relayout_main_kernel,
        out_shape=jax.ShapeDtypeStruct((depth, rows, P_MAIN), BF16),
        grid=(depth, P_MAIN // tc),
        in_specs=[pl.BlockSpec((pl.Element(1), pl.Element(tc), pl.Element(rows)), lambda l, k: (l, src_row(k), 0))],
        out_specs=pl.BlockSpec((None, rows, tc), lambda l, k: (l, 0, k)),
        name="relayout_w_in",
    )(wt)
    small = pl.pallas_call(
        _relayout_small_kernel,
        out_shape=jax.ShapeDtypeStruct((depth, rows, LANE), BF16),
        grid=(depth,),
        in_specs=[pl.BlockSpec((pl.Element(1), pl.Element(N_DT), pl.Element(rows)), lambda l: (l, COL_DT, 0)),
                  pl.BlockSpec((pl.Element(1), pl.Element(N_GLR), pl.Element(rows)), lambda l: (l, COL_GLR, 0))],
        out_specs=pl.BlockSpec((None, rows, LANE), lambda l: (l, 0, 0)),
        scratch_shapes=[pltpu.VMEM((LANE, rows), F32)],
        name="relayout_w_small",
    )(wt, wt)
    return main, small


def _head_row(v, direction):
    lo = SSD_HEADS * direction
    return jnp.pad(v.astype(F32).reshape(1, SSD_HEADS), ((0, 0), (lo, LANE - SSD_HEADS - lo)))


def kernel(x, c, ctx, c_ctx, w_ada, b_ada, g_mix, g_ffn, w_in, w_out, sc_conv_w, cm_w_s, cm_b_s, mb_conv_w,
           mb_conv_b, mb_a_log, mb_dt_bias, mb_d, mb_norm_g, gla_w_gate, gla_b_gate, gla_norm_g, ffn_w_gate,
           ffn_w_up, ffn_w_down, moe_router, moe_w_gate, moe_w_up, moe_w_down, g_final):
    nb, seq, _ = x.shape
    ctx_len = ctx.shape[1]
    depth = w_ada.shape[0]
    assert depth == 2, "supported stack: a dense-FFN layer followed by a last, routed-FFN layer"
    m_lat = nb * seq
    h = x.reshape(m_lat, D)
    hc = ctx.reshape(nb * ctx_len, D)

    cond = jnp.concatenate([c, c_ctx[None, :]], axis=0)
    cb = jnp.broadcast_to(cond[:, :, None], (nb + 1, D, LANE))
    mods = _ada(cb, w_ada, b_ada).reshape(depth, 8, N_MOD, 1, D)

    tm_in = IN_TM
    lat_row_in = lambda i: i // (seq // tm_in)
    lat_row_512 = lambda i: i // (seq // OUT_TM)
    lat_row_g = lambda i: i // (seq // GATHER_R)
    ctx_row = lambda i: nb

    head_of_col = np.arange(GW) // SSD_HD
    expand = [jnp.asarray(np.arange(LANE)[:, None] == 8 * d + head_of_col[None, :], dtype=BF16) for d in range(2)]
    w_in_main, w_in_small = _relayout_w_in(w_in)
    out = None
    for i in range(depth):
        last = i == depth - 1
        modr = mods[i]
        g_mix_i = g_mix[i].reshape(1, D)
        g_ffn_i = g_ffn[i].reshape(1, D)
        p, ps = _inproj(h, g_mix_i, modr, lat_row_in, w_in_main, w_in_small, i, tm_in)
        pc, psc = _inproj(hc, g_mix_i, modr, ctx_row, w_in_main, w_in_small, i, nb * ctx_len)

        ws = cm_w_s[i].astype(BF16)
        bs = jnp.repeat(cm_b_s[i].T, 128, axis=1)
        ya = _scm(p, sc_conv_w[i], ws, bs, GRID_W)
        if not last:
            yac = _scm(pc, sc_conv_w[i], ws, bs, ctx_len)

        conv_b = mb_conv_b[i].reshape(1, -1)
        s_zero = jnp.zeros((nb, SSD_Q, GW), F32)
        skip = jnp.repeat(mb_d[i, 0] + mb_d[i, 1], SSD_HD).reshape(1, GW)
        fin = (skip, mb_norm_g[i].reshape(1, GW))
        ssd_c = [(mb_conv_w[i], conv_b, _head_row(mb_a_log[i, d], d), _head_row(mb_dt_bias[i, d], d), expand[d])
                 for d in range(2)]
        y0c, s0c = _ssd(pc, psc, s_zero, ssd_c[0], nb, ctx_len, 0, False)
        y0, _ = _ssd(p, ps, s0c, ssd_c[0], nb, GRID_W, 0, False)
        if last:
            _, s1c = _ssd(pc, psc, s_zero, ssd_c[1], nb, ctx_len, 1, False)
        else:
            ymbc, s1c = _ssd(pc, psc, s_zero, ssd_c[1], nb, ctx_len, 1, True, y0c, fin)
        ymb, _ = _ssd(p, ps, s1c, ssd_c[1], nb, GRID_W, 1, True, y0, fin)

        g_zero = jnp.zeros((nb, GLA_DV, 256), F32)
        gla_c = [(jnp.zeros((LANE, 256), F32).at[16 + 16 * d:32 + 16 * d].set(gla_w_gate[i, d]).astype(BF16),
                  gla_b_gate[i, d].reshape(1, 256)) for d in range(2)]
        ng = gla_norm_g[i].reshape(1, GLA_DV)
        o0c, t0c = _gla(pc, psc, g_zero, gla_c[0], nb, 0, False)
        o0, _ = _gla(p, ps, t0c, gla_c[0], nb, 0, False)
        if last:
            _, t1c = _gla(pc, psc, g_zero, gla_c[1], nb, 1, False)
        else:
            yglac, t1c = _gla(pc, psc, g_zero, gla_c[1], nb, 1, True, o0c, ng)
        ygla = _gla(p, ps, t1c, gla_c[1], nb, 1, True, o0, ng)[0]

        w_out_b = w_out[i].astype(BF16)
        j = i // 2
        if i % 2 == 0:
            hcn, xcn = _outproj(hc, yac, ymbc, yglac, w_out_b, modr, ctx_row, g_ffn_i)
            hc, wg, wu, wd = _ffn(xcn, hcn, modr, ctx_row, ffn_w_gate[j], ffn_w_up[j], ffn_w_down[j],
                                  nb * ctx_len, True)
            hn, xn = _outproj(h, ya, ymb, ygla, w_out_b, modr, lat_row_512, g_ffn_i)
            h = _ffn(xn, hn, modr, lat_row_in, wg, wu, wd, tm_in, False)[0]
        else:
            w_router = jnp.pad(moe_router[j], ((0, 0), (0, LANE - N_EXPERTS)))
            hn, xn, ri, rw, cnt = _outproj(h, ya, ymb, ygla, w_out_b, modr, lat_row_512, g_ffn_i, w_router)
            if last:
                out = _moe(xn, hn, ri, rw, cnt, modr, lat_row_g, g_final.reshape(1, D),
                           moe_w_gate[j], moe_w_up[j], moe_w_down[j])
            else:
                raise NotImplementedError("routed FFN is only implemented as the last layer's channel mixer")
    return out.reshape(nb, seq, D)
```
